```python
import math
import jax
import jax.numpy as jnp
from jax import lax
import numpy as np

D_MODEL = 4096
BATCH = 4
SEQ = 4096
DEPTH = 4

HEAD_DIM = 128
SB_HEADS = 16
SB_WIDTH = SB_HEADS * HEAD_DIM
HG_HEADS = 16
HG_DK = 128
HG_DV = 128
HG_WIDTH = HG_HEADS * HG_DK
EVEN_IN = 3 * SB_WIDTH + 4 * HG_WIDTH
EVEN_MIX = SB_WIDTH + HG_HEADS * HG_DV
Q_BLOCK = 128
HG_CHUNK = 64
F_MIN = 1e-6
NSA_HEADS = 32
NSA_KV_GROUPS = 4
NSA_HPG = NSA_HEADS // NSA_KV_GROUPS
NSA_WIDTH = NSA_HEADS * HEAD_DIM
NSA_KV_WIDTH = NSA_KV_GROUPS * HEAD_DIM
ODD_IN = NSA_WIDTH + 6 * NSA_KV_WIDTH + 3 * NSA_HEADS
CMP_LEN = 32
CMP_STRIDE = 16
SLC_BLOCK = 64
N_SELECT = 16
WINDOW = 512
NSA_Q_BLOCK = 32
REL_BUCKETS = 32
REL_MAX_DIST = 128
N_EXPERTS = 32
TOP_K = 4
EXPERT_FF = 384
SWIGLU_LIMIT = 7.0
SWIGLU_ALPHA = 1.702
DN_ALPHA = (2 * DEPTH) ** 0.25
DN_BETA = (8 * DEPTH) ** -0.25
LN_EPS = 1e-5
NEG_INF = -1e30
FORCED = 1e9
N_EVEN = (DEPTH + 1) // 2
N_ODD = DEPTH // 2

kernel_name = "hybrid_sb_hgrn2_nsa_moe_deepnorm"

F32 = jnp.float32


def layer_norm(x, g, b):
    xf = x.astype(F32)
    mu = jnp.mean(xf, -1, keepdims=True)
    var = jnp.mean(jnp.square(xf - mu), -1, keepdims=True)
    return (xf - mu) * lax.rsqrt(var + LN_EPS) * g.astype(F32) + b.astype(F32)


def rms_norm(x, g):
    xf = x.astype(F32)
    return xf * lax.rsqrt(jnp.mean(xf * xf, -1, keepdims=True) + LN_EPS) * g.astype(F32)


def rel_bucket(dist):
    dist = jnp.maximum(dist, 0)
    max_exact = REL_BUCKETS // 2
    ratio = jnp.log(jnp.maximum(dist, max_exact).astype(F32) / max_exact) / math.log(REL_MAX_DIST / max_exact)
    large = jnp.minimum(max_exact + (ratio * (REL_BUCKETS - max_exact)).astype(jnp.int32), REL_BUCKETS - 1)
    return jnp.where(dist < max_exact, dist, large)


def stick_breaking_attention(q, k, v):
    b, t, h, d = q.shape
    nblk = t // Q_BLOCK
    scale = d ** -0.5
    qb = jnp.moveaxis(q.reshape(b, nblk, Q_BLOCK, h, d), 1, 0)
    kpos = jnp.arange(t)

    def block(args):
        qi, i = args
        z = jnp.einsum('bqhd,bkhd->bhqk', qi, k, preferred_element_type=F32) * scale
        qpos = i * Q_BLOCK + jnp.arange(Q_BLOCK)
        causal = kpos[None, :] < qpos[:, None]
        log_keep = jnp.where(causal, jax.nn.log_sigmoid(-z), 0.0)
        between = lax.cumsum(log_keep, axis=3, reverse=True) - log_keep
        w = jnp.where(causal, jnp.exp(jax.nn.log_sigmoid(z) + between), 0.0)
        return jnp.einsum('bhqk,bkhd->bqhd', w.astype(v.dtype), v)

    out = lax.map(block, (qb, jnp.arange(nblk)))
    return jnp.moveaxis(out, 0, 1).reshape(b, t, h, d)


def hgrn2_recurrence(q, k, v, log_f):
    b, t, h, dk = q.shape
    dv = v.shape[-1]
    n = t // HG_CHUNK

    def to_chunks(a):
        a = a.astype(F32).reshape(b, n, HG_CHUNK, h, a.shape[-1])
        return jnp.moveaxis(a, 1, 0).transpose(0, 1, 3, 2, 4)

    causal = jnp.tril(jnp.ones((HG_CHUNK, HG_CHUNK), bool))[:, :, None]

    def step(state, inp):
        qc, kc, vc, gc = inp
        cum = jnp.cumsum(gc, axis=2)
        diff = cum[:, :, :, None, :] - cum[:, :, None, :, :]
        decay = jnp.where(causal, jnp.exp(jnp.minimum(diff, 0.0)), 0.0)
        scores = jnp.einsum('bhtd,bhsd,bhtsd->bhts', qc, kc, decay)
        o = (jnp.einsum('bhts,bhse->bhte', scores, vc)
             + jnp.einsum('bhtd,bhde->bhte', qc * jnp.exp(cum), state))
        last = cum[:, :, -1, :]
        state = (jnp.exp(last)[..., None] * state
                 + jnp.einsum('bhsd,bhse->bhde', kc * jnp.exp(last[:, :, None, :] - cum), vc))
        return state, o

    s0 = jnp.zeros((b, h, dk, dv), F32)
    _, o = lax.scan(step, s0, (to_chunks(q), to_chunks(k), to_chunks(v), to_chunks(log_f)))
    return o.transpose(1, 0, 3, 2, 4).reshape(b, t, h, dv)


def even_mixer(x, w_in, w_out, lb, norm_g):
    b, t, _ = x.shape
    proj = x @ w_in
    sp = [SB_WIDTH, 2 * SB_WIDTH, 3 * SB_WIDTH, 3 * SB_WIDTH + HG_WIDTH,
          3 * SB_WIDTH + 2 * HG_WIDTH, 3 * SB_WIDTH + 3 * HG_WIDTH]
    qa, ka, va, qh, fh, ih, gh = jnp.split(proj, sp, axis=-1)
    heads_a = lambda a: a.reshape(b, t, SB_HEADS, HEAD_DIM)
    o_a = stick_breaking_attention(heads_a(qa), heads_a(ka), heads_a(va))
    f = lb + (1.0 - lb) * jax.nn.sigmoid(fh.astype(F32))
    log_f = jnp.log(jnp.maximum(f, F_MIN))
    k_in = 1.0 - f
    heads_b = lambda a: a.reshape(b, t, HG_HEADS, a.shape[-1] // HG_HEADS)
    o_b = hgrn2_recurrence(heads_b(jax.nn.silu(qh)), heads_b(k_in), heads_b(ih), heads_b(log_f))
    o_b = rms_norm(o_b, norm_g) * heads_b(jax.nn.silu(gh.astype(F32)))
    mixed = jnp.concatenate([o_a.reshape(b, t, SB_WIDTH).astype(F32), o_b.reshape(b, t, -1)], axis=-1)
    return (mixed.astype(x.dtype) @ w_out).astype(x.dtype)


def compress_blocks(a, idx, w1, w2, pos):
    blk = a[:, idx] + pos[:, None, :]
    b, n, l, g, d = blk.shape
    flat = jnp.moveaxis(blk, 3, 2).reshape(b, n, g, l * d)
    return jax.nn.silu(flat @ w1) @ w2


def nsa_mixer(x, w_in, w_out, ck_w1, ck_w2, ck_pos, cv_w1, cv_w2, cv_pos, rel_bias):
    b, t, _ = x.shape
    G, R, Dh = NSA_KV_GROUPS, NSA_HPG, HEAD_DIM
    proj = x @ w_in
    sp = [NSA_WIDTH + i * NSA_KV_WIDTH for i in range(7)]
    q, kc, vc, ks, vs, kw, vw, gates = jnp.split(proj, sp, axis=-1)
    kv = lambda a: a.reshape(b, t, G, Dh)
    q = q.reshape(b, t, G, R, Dh)
    gates = jax.nn.sigmoid(gates.astype(F32)).reshape(b, t, G, R, 3)
    n_cmp = (t - CMP_LEN) // CMP_STRIDE + 1
    cmp_start = jnp.arange(n_cmp) * CMP_STRIDE
    cmp_idx = cmp_start[:, None] + jnp.arange(CMP_LEN)[None, :]
    cmp_end = cmp_start + CMP_LEN - 1
    k_cmp = compress_blocks(kv(kc), cmp_idx, ck_w1, ck_w2, ck_pos)
    v_cmp = compress_blocks(kv(vc), cmp_idx, cv_w1, cv_w2, cv_pos)
    n_slc = t // SLC_BLOCK
    n_sel = min(N_SELECT, n_slc)
    k_blk = kv(ks).reshape(b, n_slc, SLC_BLOCK, G, Dh).transpose(0, 3, 1, 2, 4)
    v_blk = kv(vs).reshape(b, n_slc, SLC_BLOCK, G, Dh).transpose(0, 3, 1, 2, 4)
    slc_start = jnp.arange(n_slc) * SLC_BLOCK
    overlap = ((cmp_start[:, None] < slc_start[None, :] + SLC_BLOCK)
               & (cmp_end[:, None] >= slc_start[None, :])).astype(F32)
    kw_pad = jnp.pad(kv(kw), ((0, 0), (WINDOW, 0), (0, 0), (0, 0)))
    vw_pad = jnp.pad(kv(vw), ((0, 0), (WINDOW, 0), (0, 0), (0, 0)))
    bias_tab = rel_bias.astype(F32).reshape(REL_BUCKETS, G, R)
    bias_tab_g = bias_tab.transpose(1, 0, 2)
    n_qb = t // NSA_Q_BLOCK
    q_blocks = jnp.moveaxis(q.reshape(b, n_qb, NSA_Q_BLOCK, G, R, Dh), 1, 0)
    g_blocks = jnp.moveaxis(gates.reshape(b, n_qb, NSA_Q_BLOCK, G, R, 3), 1, 0)
    scale = Dh ** -0.5
    b_idx = jnp.arange(b)[:, None, None, None]
    g_idx = jnp.arange(G)[None, :, None, None]

    def block(args):
        qi, gb, i = args
        qpos = i * NSA_Q_BLOCK + jnp.arange(NSA_Q_BLOCK)
        s_c = jnp.einsum('bqgrd,bngd->bgrqn', qi, k_cmp, preferred_element_type=F32) * scale
        s_c = s_c + jnp.transpose(bias_tab[rel_bucket(qpos[:, None] - cmp_end[None, :])], (2, 3, 0, 1))
        m_c = cmp_end[None, :] <= qpos[:, None]
        p_c = jax.nn.softmax(jnp.where(m_c, s_c, NEG_INF), axis=-1) * m_c
        o_c = jnp.einsum('bgrqn,bngd->bqgrd', p_c.astype(v_cmp.dtype), v_cmp)
        imp = jnp.einsum('bgrqn,nm->bgqm', p_c, overlap)
        q_blk = qpos // SLC_BLOCK
        j = jnp.arange(n_slc)[None, :]
        valid = j <= q_blk[:, None]
        forced = (j == 0) | (j == q_blk[:, None]) | (j == q_blk[:, None] - 1)
        imp = jnp.where(valid & forced, FORCED, jnp.where(valid, imp, NEG_INF))
        top_val, top_idx = lax.top_k(imp, n_sel)
        k_sel = k_blk[b_idx, g_idx, top_idx]
        v_sel = v_blk[b_idx, g_idx, top_idx]
        tok = top_idx[..., None] * SLC_BLOCK + jnp.arange(SLC_BLOCK)
        m_s = (tok <= qpos[:, None, None]) & (top_val > 0.5 * NEG_INF)[..., None]
        s_s = jnp.einsum('bqgrd,bgqnkd->bgrqnk', qi, k_sel, preferred_element_type=F32) * scale
        bias_s = bias_tab_g[g_idx[..., None], rel_bucket(qpos[:, None, None] - tok)]
        s_s = jnp.where(m_s[:, :, None], s_s + jnp.moveaxis(bias_s, -1, 2), NEG_INF)
        p_s = jax.nn.softmax(s_s.reshape(b, G, R, NSA_Q_BLOCK, -1), axis=-1).reshape(s_s.shape)
        o_s = jnp.einsum('bgrqnk,bgqnkd->bqgrd', p_s.astype(v_sel.dtype), v_sel)
        kpos = i * NSA_Q_BLOCK - WINDOW + jnp.arange(WINDOW + NSA_Q_BLOCK)
        k_win = lax.dynamic_slice_in_dim(kw_pad, i * NSA_Q_BLOCK, WINDOW + NSA_Q_BLOCK, axis=1)
        v_win = lax.dynamic_slice_in_dim(vw_pad, i * NSA_Q_BLOCK, WINDOW + NSA_Q_BLOCK, axis=1)
        dist = qpos[:, None] - kpos[None, :]
        m_w = (dist >= 0) & (dist < WINDOW) & (kpos[None, :] >= 0)
        s_w = jnp.einsum('bqgrd,bkgd->bgrqk', qi, k_win, preferred_element_type=F32) * scale
        s_w = s_w + jnp.transpose(bias_tab[rel_bucket(dist)], (2, 3, 0, 1))
        p_w = jax.nn.softmax(jnp.where(m_w, s_w, NEG_INF), axis=-1)
        o_w = jnp.einsum('bgrqk,bkgd->bqgrd', p_w.astype(v_win.dtype), v_win)
        return gb[..., 0:1] * o_c + gb[..., 1:2] * o_s + gb[..., 2:3] * o_w

    out = lax.map(block, (q_blocks, g_blocks, jnp.arange(n_qb)))
    out = jnp.moveaxis(out, 0, 1).reshape(b, t, NSA_WIDTH)
    return (out.astype(x.dtype) @ w_out).astype(x.dtype)


def moe_ffn(x, w_router, b_router, w_gu, b_gu, w_down, b_down):
    b, t, d = x.shape
    xt = x.reshape(b * t, d)
    logits = (xt @ w_router).astype(F32) + b_router.astype(F32)
    top_val, top_idx = lax.top_k(logits, TOP_K)
    top_w = jax.nn.softmax(top_val, axis=-1)
    gate = jnp.sum(jax.nn.one_hot(top_idx, N_EXPERTS, dtype=F32) * top_w[..., None], axis=1)
    out = jnp.zeros((b * t, d), F32)
    for e in range(N_EXPERTS):
        hgu = (xt @ w_gu[e] + b_gu[e]).astype(F32)
        glu = jnp.minimum(hgu[:, 0::2], SWIGLU_LIMIT)
        lin = jnp.clip(hgu[:, 1::2], -SWIGLU_LIMIT, SWIGLU_LIMIT)
        act = glu * jax.nn.sigmoid(SWIGLU_ALPHA * glu) * (lin + 1.0)
        out = out + gate[:, e:e + 1] * (act.astype(x.dtype) @ w_down[e] + b_down[e]).astype(F32)
    return out.astype(x.dtype).reshape(b, t, d)


def setup_inputs(seed: int = 0) -> dict:
    key = jax.random.key(seed)
    ks = jax.random.split(key, 24)

    def normal(k, shape, scale):
        return jax.random.normal(k, shape, F32) * scale

    L = CMP_LEN
    return {
        "x": normal(ks[0], (BATCH, SEQ, D_MODEL), 1.0),
        "ln1_g": 1.0 + normal(ks[1], (DEPTH, D_MODEL), 0.02),
        "ln1_b": normal(ks[2], (DEPTH, D_MODEL), 0.02),
        "ln2_g": 1.0 + normal(ks[3], (DEPTH, D_MODEL), 0.02),
        "ln2_b": normal(ks[4], (DEPTH, D_MODEL), 0.02),
        "ev_w_in": normal(ks[5], (N_EVEN, D_MODEL, EVEN_IN), D_MODEL ** -0.5),
        "ev_w_out": normal(ks[6], (N_EVEN, EVEN_MIX, D_MODEL), DN_BETA * EVEN_MIX ** -0.5),
        "hg_lb_raw": normal(ks[7], (DEPTH, HG_WIDTH), 0.5),
        "hg_norm_g": 1.0 + normal(ks[8], (N_EVEN, HG_DV), 0.02),
        "od_w_in": normal(ks[9], (N_ODD, D_MODEL, ODD_IN), D_MODEL ** -0.5),
        "od_w_out": normal(ks[10], (N_ODD, NSA_WIDTH, D_MODEL), DN_BETA * NSA_WIDTH ** -0.5),
        "cmp_k_w1": normal(ks[11], (N_ODD, L * HEAD_DIM, HEAD_DIM), (L * HEAD_DIM) ** -0.5),
        "cmp_k_w2": normal(ks[12], (N_ODD, HEAD_DIM, HEAD_DIM), HEAD_DIM ** -0.5),
        "cmp_k_pos": normal(ks[13], (N_ODD, L, HEAD_DIM), 0.1),
        "cmp_v_w1": normal(ks[14], (N_ODD, L * HEAD_DIM, HEAD_DIM), (L * HEAD_DIM) ** -0.5),
        "cmp_v_w2": normal(ks[15], (N_ODD, HEAD_DIM, HEAD_DIM), HEAD_DIM ** -0.5),
        "cmp_v_pos": normal(ks[16], (N_ODD, L, HEAD_DIM), 0.1),
        "rel_bias": normal(ks[17], (REL_BUCKETS, NSA_HEADS), 0.2),
        "router_w": normal(ks[18], (DEPTH, D_MODEL, N_EXPERTS), D_MODEL ** -0.5),
        "router_b": normal(ks[19], (DEPTH, N_EXPERTS), 0.01),
        "exp_w_gu": normal(ks[20], (DEPTH, N_EXPERTS, D_MODEL, 2 * EXPERT_FF), D_MODEL ** -0.5),
        "exp_b_gu": normal(ks[21], (DEPTH, N_EXPERTS, 2 * EXPERT_FF), 0.02),
        "exp_w_down": normal(ks[22], (DEPTH, N_EXPERTS, EXPERT_FF, D_MODEL), DN_BETA * EXPERT_FF ** -0.5),
        "exp_b_down": normal(ks[23], (DEPTH, N_EXPERTS, D_MODEL), 0.02),
    }


def reference(x, ln1_g, ln1_b, ln2_g, ln2_b, ev_w_in, ev_w_out, hg_lb_raw, hg_norm_g,
              od_w_in, od_w_out, cmp_k_w1, cmp_k_w2, cmp_k_pos, cmp_v_w1, cmp_v_w2, cmp_v_pos,
              rel_bias, router_w, router_b, exp_w_gu, exp_b_gu, exp_w_down, exp_b_down):
    lb_soft = jax.nn.softmax(hg_lb_raw.astype(F32), axis=0)
    lower_bounds = jnp.cumsum(lb_soft, axis=0) - lb_soft[0]
    h = x
    for layer in range(DEPTH):
        if layer % 2 == 0:
            e = layer // 2
            mix = even_mixer(h, ev_w_in[e], ev_w_out[e], lower_bounds[layer], hg_norm_g[e])
        else:
            o = layer // 2
            mix = nsa_mixer(h, od_w_in[o], od_w_out[o], cmp_k_w1[o], cmp_k_w2[o], cmp_k_pos[o],
                            cmp_v_w1[o], cmp_v_w2[o], cmp_v_pos[o], rel_bias)
        h = layer_norm(DN_ALPHA * h + mix, ln1_g[layer], ln1_b[layer]).astype(x.dtype)
        ffn = moe_ffn(h, router_w[layer], router_b[layer], exp_w_gu[layer], exp_b_gu[layer],
                      exp_w_down[layer], exp_b_down[layer])
        h = layer_norm(DN_ALPHA * h + ffn, ln2_g[layer], ln2_b[layer]).astype(x.dtype)
    return h
```

```python
import functools
import math

import jax
import jax.numpy as jnp
import numpy as np
from jax import lax
from jax.experimental import pallas as pl
from jax.experimental.pallas import tpu as pltpu

F32 = jnp.float32
BF16 = jnp.bfloat16
I32 = jnp.int32

HEAD_DIM = 128
SB_HEADS = 16
HG_HEADS = 16
SB_WIDTH = SB_HEADS * HEAD_DIM
HG_WIDTH = HG_HEADS * HEAD_DIM
F_MIN = 1e-6
NSA_HEADS = 32
NSA_KV_GROUPS = 4
NSA_HPG = NSA_HEADS // NSA_KV_GROUPS
NSA_WIDTH = NSA_HEADS * HEAD_DIM
NSA_KV_WIDTH = NSA_KV_GROUPS * HEAD_DIM
CMP_LEN = 32
CMP_STRIDE = 16
SLC_BLOCK = 64
N_SELECT = 16
WINDOW = 512
REL_BUCKETS = 32
REL_MAX_DIST = 128
N_EXPERTS = 32
TOP_K = 4
EXPERT_FF = 384
SWIGLU_LIMIT = 7.0
SWIGLU_ALPHA = 1.702
LN_EPS = 1e-5
NEG_INF = -1e30
FORCED = 1e9

LANES = 128
VMEM_LIMIT = 56 * 1024 * 1024

HI = lax.Precision.HIGHEST


def _cparams(sem):
    return pltpu.CompilerParams(dimension_semantics=sem, vmem_limit_bytes=VMEM_LIMIT)


def _dot_nt(a, b, **kw):
    return lax.dot_general(a, b, (((1,), (1,)), ((), ())), preferred_element_type=F32, **kw)


def _dot(a, b, **kw):
    return jnp.dot(a, b, preferred_element_type=F32, **kw)


def _mm_kernel(a_ref, w_ref, o_ref):
    o_ref[...] = _dot(a_ref[...], w_ref[...]).astype(o_ref.dtype)


def _matmul(a, w, *, tm=1024, tn=512, out_dtype=F32):
    m, k = a.shape
    n = w.shape[1]
    tn = min(tn, n)
    assert m % tm == 0 and n % tn == 0
    return pl.pallas_call(
        _mm_kernel,
        grid=(m // tm, n // tn),
        in_specs=[pl.BlockSpec((tm, k), lambda i, j: (i, 0)),
                  pl.BlockSpec((k, tn), lambda i, j: (0, j))],
        out_specs=pl.BlockSpec((tm, tn), lambda i, j: (i, j)),
        out_shape=jax.ShapeDtypeStruct((m, n), out_dtype),
        compiler_params=_cparams(("parallel", "arbitrary")),
        name="matmul",
    )(a, w)


def _mm2_kernel(a1_ref, a2_ref, w1_ref, w2_ref, o_ref):
    o_ref[...] = _dot(a1_ref[...], w1_ref[...]) + _dot(a2_ref[...], w2_ref[...])


def _matmul2(a1, a2, w, *, tm=1024, tn=512):
    m, k1 = a1.shape
    k2 = a2.shape[1]
    n = w.shape[1]
    assert k1 == k2 and w.shape[0] == k1 + k2
    return pl.pallas_call(
        _mm2_kernel,
        grid=(m // tm, n // tn),
        in_specs=[pl.BlockSpec((tm, k1), lambda i, j: (i, 0)),
                  pl.BlockSpec((tm, k2), lambda i, j: (i, 0)),
                  pl.BlockSpec((k1, tn), lambda i, j: (0, j)),
                  pl.BlockSpec((k2, tn), lambda i, j: (1, j))],
        out_specs=pl.BlockSpec((tm, tn), lambda i, j: (i, j)),
        out_shape=jax.ShapeDtypeStruct((m, n), F32),
        compiler_params=_cparams(("parallel", "arbitrary")),
        name="matmul2",
    )(a1, a2, w, w)


def _add_ln_kernel(h_ref, y_ref, g_ref, b_ref, of_ref, ob_ref, *, alpha):
    x = alpha * h_ref[...] + y_ref[...]
    mu = jnp.mean(x, axis=-1, keepdims=True)
    xc = x - mu
    var = jnp.mean(xc * xc, axis=-1, keepdims=True)
    out = xc * lax.rsqrt(var + LN_EPS) * g_ref[...] + b_ref[...]
    of_ref[...] = out
    ob_ref[...] = out.astype(BF16)


def _add_ln(h, y, g, b, alpha, *, tm=256):
    n, d = h.shape
    row = pl.BlockSpec((tm, d), lambda i: (i, 0))
    vec = pl.BlockSpec((1, d), lambda i: (0, 0))
    return pl.pallas_call(
        functools.partial(_add_ln_kernel, alpha=alpha),
        grid=(n // tm,),
        in_specs=[row, row, vec, vec],
        out_specs=[row, row],
        out_shape=[jax.ShapeDtypeStruct((n, d), F32), jax.ShapeDtypeStruct((n, d), BF16)],
        compiler_params=_cparams(("parallel",)),
        name="add_ln",
    )(h, y, g.reshape(1, d), b.reshape(1, d))


SB_TQ = 256
SB_KB = 128


def _sb_kernel(q_ref, k_ref, v_ref, u_ref, o_ref, acc_ref, carry_ref, *, tq, kb, scale):
    i = pl.program_id(2)
    q = q_ref[...].astype(BF16)
    acc_ref[...] = jnp.zeros_like(acc_ref)
    carry_ref[...] = jnp.zeros_like(carry_ref)
    nblk = (i + 1) * (tq // kb)
    qpos = i * tq + lax.broadcasted_iota(I32, (tq, kb), 0)
    kcol = lax.broadcasted_iota(I32, (tq, kb), 1)
    u = u_ref[...]

    def body(n, c):
        start = pl.multiple_of((nblk - 1 - n) * kb, kb)
        k = k_ref[pl.ds(start, kb), :].astype(BF16)
        v = v_ref[pl.ds(start, kb), :].astype(BF16)
        z = _dot_nt(q, k) * scale
        causal = (kcol + start) < qpos
        sp = jnp.log(1.0 + jnp.exp(-jnp.abs(z)))
        ls = jnp.minimum(z, 0.0) - sp
        lk = jnp.where(causal, ls - z, 0.0)
        hi = lk.astype(BF16)
        lo = (lk - hi.astype(F32)).astype(BF16)
        rt = _dot(jnp.concatenate([hi, lo], axis=1), u)
        carry = carry_ref[...]
        between = rt[:, :kb] - lk + carry
        w = jnp.where(causal, jnp.exp(ls + between), 0.0)
        acc_ref[...] += _dot(w.astype(BF16), v)
        carry_ref[...] = carry + rt[:, kb:]
        return c

    lax.fori_loop(0, nblk, body, 0)
    o_ref[...] = acc_ref[...].astype(o_ref.dtype)


def _sb_attention(proj, b, t):
    tq, kb = SB_TQ, SB_KB
    nq = t // tq
    r = np.arange(2 * kb)[:, None] % kb
    c = np.arange(2 * kb)[None, :]
    u = jnp.asarray(np.where(c < kb, r >= c, True), BF16)
    kv_spec = lambda off: pl.BlockSpec((t, HEAD_DIM), lambda bi, h, i: (bi, off + h))
    return pl.pallas_call(
        functools.partial(_sb_kernel, tq=tq, kb=kb, scale=HEAD_DIM ** -0.5),
        grid=(b, SB_HEADS, nq),
        in_specs=[pl.BlockSpec((tq, HEAD_DIM), lambda bi, h, i: (bi * nq + i, h)),
                  kv_spec(SB_HEADS), kv_spec(2 * SB_HEADS),
                  pl.BlockSpec((2 * kb, 2 * kb), lambda bi, h, i: (0, 0))],
        out_specs=pl.BlockSpec((tq, HEAD_DIM), lambda bi, h, i: (bi * nq + i, h)),
        out_shape=jax.ShapeDtypeStruct((b * t, SB_WIDTH), BF16),
        scratch_shapes=[pltpu.VMEM((tq, HEAD_DIM), F32), pltpu.VMEM((tq, kb), F32)],
        compiler_params=_cparams(("parallel", "parallel", "arbitrary")),
        name="sb_attention",
    )(proj, proj, proj, u)


HG_C = 64
HG_TB = 512


def _hg_tables(c):
    levels = []
    m = c // 2
    while m >= 1:
        levels.append(m)
        m //= 2
    nl = len(levels)
    e = np.zeros((nl + 2, c, c), np.float32)
    mask = np.zeros((nl + 1, c, c), np.float32)
    p = np.arange(c)
    for li, m in enumerate(levels):
        blk = p // (2 * m)
        half = (p // m) % 2
        mid = blk * 2 * m + m - 1
        for r in range(c):
            if half[r] == 1:
                e[li, r, mid[r] + 1:r + 1] = 1.0
            else:
                e[li, r, r + 1:mid[r] + 1] = 1.0
        mask[li] = ((half[:, None] == 1) & (half[None, :] == 0) & (blk[:, None] == blk[None, :]))
    mask[nl] = np.eye(c)
    e[nl] = np.tril(np.ones((c, c)))
    e[nl + 1] = np.triu(np.ones((c, c)), 1)
    return e.reshape((nl + 2) * c, c), mask, nl


def _hg_kernel(qh_ref, fh_ref, ih_ref, gh_ref, lb_ref, ng_ref, e_ref, mask_ref, o_ref, st_ref,
               *, c, tb, nl):
    @pl.when(pl.program_id(2) == 0)
    def _():
        st_ref[...] = jnp.zeros_like(st_ref)

    lb = lb_ref[...]
    ng = ng_ref[...]
    emat = e_ref[...]

    def chunk(ci, carry):
        r0 = pl.multiple_of(ci * c, c)
        rows = pl.ds(r0, c)
        qh = qh_ref[rows, :]
        f = lb + (1.0 - lb) * jax.nn.sigmoid(fh_ref[rows, :])
        g = jnp.log(jnp.maximum(f, F_MIN))
        kk = 1.0 - f
        q = qh * jax.nn.sigmoid(qh)
        v = ih_ref[rows, :]
        vb = v.astype(BF16)
        ex = jnp.exp(_dot(emat, g, precision=HI))
        scores = mask_ref[nl] * _dot_nt(q.astype(BF16), kk.astype(BF16))
        for li in range(nl):
            a = ex[li * c:(li + 1) * c]
            scores = scores + mask_ref[li] * _dot_nt((q * a).astype(BF16), (kk * a).astype(BF16))
        ecum = ex[nl * c:(nl + 1) * c]
        erest = ex[(nl + 1) * c:(nl + 2) * c]
        st = st_ref[...]
        o = _dot(scores.astype(BF16), vb) + _dot_nt((q * ecum).astype(BF16), st.astype(BF16))
        st_ref[...] = ecum[c - 1:c, :] * st + _dot(v.T.astype(BF16), (kk * erest).astype(BF16))
        gh = gh_ref[rows, :]
        o = o * lax.rsqrt(jnp.mean(o * o, axis=-1, keepdims=True) + LN_EPS) * ng
        o_ref[rows, :] = (o * (gh * jax.nn.sigmoid(gh))).astype(o_ref.dtype)
        return carry

    lax.fori_loop(0, tb // c, chunk, 0)


def _hgrn2(proj, lb, norm_g, b, t):
    c, tb = HG_C, min(HG_TB, t)
    e, mask, nl = _hg_tables(c)
    nt = t // tb
    base = 3 * SB_HEADS
    col = lambda k: pl.BlockSpec((tb, HEAD_DIM), lambda bi, h, i: (bi * nt + i, base + k * HG_HEADS + h))
    const2 = lambda a: pl.BlockSpec(a.shape, lambda bi, h, i: (0,) * a.ndim)
    return pl.pallas_call(
        functools.partial(_hg_kernel, c=c, tb=tb, nl=nl),
        grid=(b, HG_HEADS, nt),
        in_specs=[col(0), col(1), col(2), col(3),
                  pl.BlockSpec((1, HEAD_DIM), lambda bi, h, i: (0, h)),
                  pl.BlockSpec((1, HEAD_DIM), lambda bi, h, i: (0, 0)),
                  const2(e), const2(mask)],
        out_specs=pl.BlockSpec((tb, HEAD_DIM), lambda bi, h, i: (bi * nt + i, h)),
        out_shape=jax.ShapeDtypeStruct((b * t, HG_WIDTH), BF16),
        scratch_shapes=[pltpu.VMEM((HEAD_DIM, HEAD_DIM), F32)],
        compiler_params=_cparams(("parallel", "parallel", "arbitrary")),
        name="hgrn2",
    )(proj, proj, proj, proj, lb.reshape(1, HG_WIDTH), norm_g.reshape(1, HEAD_DIM),
      jnp.asarray(e), jnp.asarray(mask))


def _cmp_kernel(a_ref, w1_ref, w2_ref, pos_ref, o_ref, *, ncp):
    half = CMP_LEN // 2
    p = jnp.zeros((ncp, HEAD_DIM), F32)
    q = jnp.zeros((ncp, HEAD_DIM), F32)
    for j in range(half):
        s = a_ref[pl.ds(j, ncp, stride=CMP_STRIDE), :]
        p = p + _dot((s + pos_ref[j:j + 1, :]).astype(BF16), w1_ref[j].astype(BF16))
        q = q + _dot((s + pos_ref[half + j:half + j + 1, :]).astype(BF16), w1_ref[half + j].astype(BF16))
    hid = p + pltpu.roll(q, ncp - 1, 0)
    hid = hid * jax.nn.sigmoid(hid)
    out = _dot(hid.astype(BF16), w2_ref[...].astype(BF16))
    row = lax.broadcasted_iota(I32, (ncp, HEAD_DIM), 0)
    o_ref[0, 0] = jnp.where(row < ncp - 1, out, 0.0)


def _compress(proj3, col0, w1, w2, pos):
    b, t, _ = proj3.shape
    ncp = t // CMP_STRIDE
    assert CMP_LEN == 2 * CMP_STRIDE
    return pl.pallas_call(
        functools.partial(_cmp_kernel, ncp=ncp),
        grid=(b, NSA_KV_GROUPS),
        in_specs=[pl.BlockSpec((None, t, HEAD_DIM), lambda bi, g: (bi, 0, col0 + g)),
                  pl.BlockSpec((CMP_LEN, HEAD_DIM, HEAD_DIM), lambda bi, g: (0, 0, 0)),
                  pl.BlockSpec((HEAD_DIM, HEAD_DIM), lambda bi, g: (0, 0)),
                  pl.BlockSpec((CMP_LEN, HEAD_DIM), lambda bi, g: (0, 0))],
        out_specs=pl.BlockSpec((1, 1, ncp, HEAD_DIM), lambda bi, g: (bi, g, 0, 0)),
        out_shape=jax.ShapeDtypeStruct((b, NSA_KV_GROUPS, ncp, HEAD_DIM), F32),
        compiler_params=_cparams(("parallel", "parallel")),
        name="nsa_compress",
    )(proj3, w1.reshape(CMP_LEN, HEAD_DIM, HEAD_DIM), w2, pos)


NSA_TQ = 128


def _rel_bucket_np(dist):
    dist = np.maximum(dist, 0)
    max_exact = REL_BUCKETS // 2
    ratio = (np.log(np.maximum(dist, max_exact).astype(np.float32) / np.float32(max_exact))
             / np.float32(math.log(REL_MAX_DIST / max_exact)))
    large = np.minimum(max_exact + (ratio * np.float32(REL_BUCKETS - max_exact)).astype(np.int32),
                       REL_BUCKETS - 1)
    return np.where(dist < max_exact, dist, large).astype(np.int32)


def _cmpsel_kernel(q_ref, kc_ref, vc_ref, bias_ref, ovl_ref, oc_ref, sel_ref,
                   *, tq, ncp, nslc, nsel, scale):
    i = pl.program_id(2)
    kc = kc_ref[0, 0].astype(BF16)
    vc = vc_ref[0, 0].astype(BF16)
    qpos = i * tq + lax.broadcasted_iota(I32, (tq, ncp), 0)
    ncol = lax.broadcasted_iota(I32, (tq, ncp), 1)
    valid = (ncol * CMP_STRIDE + (CMP_LEN - 1) <= qpos) & (ncol < ncp - 1)
    psum = jnp.zeros((tq, ncp), F32)
    for r in range(NSA_HPG):
        hs = slice(r * HEAD_DIM, (r + 1) * HEAD_DIM)
        s = _dot_nt(q_ref[0, :, hs].astype(BF16), kc) * scale + bias_ref[r]
        s = jnp.where(valid, s, NEG_INF)
        mx = jnp.max(s, axis=-1, keepdims=True)
        e = jnp.where(valid, jnp.exp(s - mx), 0.0)
        den = jnp.sum(e, axis=-1, keepdims=True)
        p = e / jnp.maximum(den, 1e-30)
        oc_ref[0, :, hs] = _dot(p.astype(BF16), vc)
        psum = psum + p
    imp = _dot_nt(ovl_ref[...], psum, precision=HI)
    jrow = lax.broadcasted_iota(I32, (nslc, tq), 0)
    qblk = (i * tq + lax.broadcasted_iota(I32, (nslc, tq), 1)) // SLC_BLOCK
    ok = jrow <= qblk
    forced = (jrow == 0) | (jrow == qblk) | (jrow == qblk - 1)
    imp = jnp.where(ok, jnp.where(forced, FORCED, imp), NEG_INF)
    rank = jnp.zeros((nslc, tq), F32)
    for j2 in range(nslc):
        row = imp[j2:j2 + 1, :]
        tie = jnp.where(jrow > j2, 1.0, 0.0)
        rank = rank + jnp.where(row > imp, 1.0, jnp.where(row == imp, tie, 0.0))
    sel_ref[0, 0] = jnp.where((rank < nsel) & ok, 1.0, 0.0)


def _cmp_select(proj3, k_cmp, v_cmp, bias_c, b, t):
    tq = NSA_TQ
    ncp = t // CMP_STRIDE
    nslc = t // SLC_BLOCK
    nsel = min(N_SELECT, nslc)
    n_idx = np.arange(ncp)
    slc_start = np.arange(nslc) * SLC_BLOCK
    cs = n_idx * CMP_STRIDE
    ovl = ((cs[None, :] < slc_start[:, None] + SLC_BLOCK)
           & (cs[None, :] + CMP_LEN - 1 >= slc_start[:, None])
           & (n_idx[None, :] < ncp - 1)).astype(np.float32)
    gw = NSA_HPG * HEAD_DIM
    return pl.pallas_call(
        functools.partial(_cmpsel_kernel, tq=tq, ncp=ncp, nslc=nslc, nsel=nsel, scale=HEAD_DIM ** -0.5),
        grid=(b, NSA_KV_GROUPS, t // tq),
        in_specs=[pl.BlockSpec((1, tq, gw), lambda bi, g, i: (bi, i, g)),
                  pl.BlockSpec((1, 1, ncp, HEAD_DIM), lambda bi, g, i: (bi, g, 0, 0)),
                  pl.BlockSpec((1, 1, ncp, HEAD_DIM), lambda bi, g, i: (bi, g, 0, 0)),
                  pl.BlockSpec((NSA_HPG, tq, ncp), lambda bi, g, i: (g, i, 0)),
                  pl.BlockSpec((nslc, ncp), lambda bi, g, i: (0, 0))],
        out_specs=[pl.BlockSpec((1, tq, gw), lambda bi, g, i: (bi, i, g)),
                   pl.BlockSpec((1, 1, nslc, tq), lambda bi, g, i: (bi, g, 0, i))],
        out_shape=[jax.ShapeDtypeStruct((b, t, NSA_WIDTH), F32),
                   jax.ShapeDtypeStruct((b, NSA_KV_GROUPS, nslc, t), F32)],
        compiler_params=_cparams(("parallel", "parallel", "parallel")),
        name="nsa_cmp_select",
    )(proj3, k_cmp, v_cmp, bias_c, jnp.asarray(ovl))


def _flash_kernel(q_ref, k_ref, v_ref, bpat_ref, sel_ref, o_ref, qs_ref, m_ref, l_ref, acc_ref,
                  *, tq, window, scale):
    i = pl.program_id(2)
    hpg = NSA_HPG
    for r in range(hpg):
        qs_ref[r * tq:(r + 1) * tq, :] = q_ref[0, :, r * HEAD_DIM:(r + 1) * HEAD_DIM].astype(BF16)
    m_ref[...] = jnp.full_like(m_ref, NEG_INF)
    l_ref[...] = jnp.zeros_like(l_ref)
    acc_ref[...] = jnp.zeros_like(acc_ref)
    qi = lax.broadcasted_iota(I32, (tq, tq), 0)
    kj = lax.broadcasted_iota(I32, (tq, tq), 1)
    if window:
        lo = jnp.maximum(i - (window + tq - 1) // tq, 0)
    else:
        lo = 0
        selq = sel_ref[0, 0].astype(BF16)
        nslc = selq.shape[1]
        er = lax.broadcasted_iota(I32, (nslc, tq), 0)
        ec = lax.broadcasted_iota(I32, (nslc, tq), 1) // SLC_BLOCK

    def body(kt, c):
        start = pl.multiple_of(kt * tq, tq)
        k = k_ref[0, pl.ds(start, tq), :].astype(BF16)
        v = v_ref[0, pl.ds(start, tq), :].astype(BF16)
        d = i - kt
        dist = d * tq + qi - kj
        if window:
            mask = (dist >= 0) & (dist < window)
        else:
            expand = jnp.where(er == kt * (tq // SLC_BLOCK) + ec, 1.0, 0.0).astype(BF16)
            mask = (_dot(selq, expand) > 0.5) & (dist >= 0)
        s = _dot_nt(qs_ref[...], k) * scale + bpat_ref[0, jnp.minimum(d, 2)]
        s3 = jnp.where(mask[None], s.reshape(hpg, tq, tq), NEG_INF)
        m_old = m_ref[...].reshape(hpg, tq, 1)
        m_new = jnp.maximum(m_old, jnp.max(s3, axis=-1, keepdims=True))
        alpha = jnp.exp(m_old - m_new)
        p = jnp.where(mask[None], jnp.exp(s3 - m_new), 0.0)
        l_ref[...] = (alpha * l_ref[...].reshape(hpg, tq, 1)
                      + jnp.sum(p, axis=-1, keepdims=True)).reshape(hpg * tq, 1)
        m_ref[...] = m_new.reshape(hpg * tq, 1)
        pv = _dot(p.reshape(hpg * tq, tq).astype(BF16), v)
        acc_ref[...] = alpha.reshape(hpg * tq, 1) * acc_ref[...] + pv
        return c

    lax.fori_loop(lo, i + 1, body, 0)
    out = acc_ref[...] / l_ref[...]
    for r in range(hpg):
        o_ref[0, :, r * HEAD_DIM:(r + 1) * HEAD_DIM] = out[r * tq:(r + 1) * tq, :]


def _nsa_flash(proj3, kcol, vcol, bpat, sel, b, t, window):
    tq = NSA_TQ
    gw = NSA_HPG * HEAD_DIM
    nslc = t // SLC_BLOCK
    kv = lambda col: pl.BlockSpec((1, t, HEAD_DIM), lambda bi, g, i: (bi, 0, col + g))
    return pl.pallas_call(
        functools.partial(_flash_kernel, tq=tq, window=window, scale=HEAD_DIM ** -0.5),
        grid=(b, NSA_KV_GROUPS, t // tq),
        in_specs=[pl.BlockSpec((1, tq, gw), lambda bi, g, i: (bi, i, g)),
                  kv(kcol), kv(vcol),
                  pl.BlockSpec((1, 3, NSA_HPG * tq, tq), lambda bi, g, i: (g, 0, 0, 0)),
                  pl.BlockSpec((1, 1, tq, nslc), lambda bi, g, i: (bi, g, i, 0))],
        out_specs=pl.BlockSpec((1, tq, gw), lambda bi, g, i: (bi, i, g)),
        out_shape=jax.ShapeDtypeStruct((b, t, NSA_WIDTH), F32),
        scratch_shapes=[pltpu.VMEM((NSA_HPG * tq, HEAD_DIM), BF16),
                        pltpu.VMEM((NSA_HPG * tq, 1), F32),
                        pltpu.VMEM((NSA_HPG * tq, 1), F32),
                        pltpu.VMEM((NSA_HPG * tq, HEAD_DIM), F32)],
        compiler_params=_cparams(("parallel", "parallel", "arbitrary")),
        name="nsa_window" if window else "nsa_selected",
    )(proj3, proj3, proj3, bpat, sel)


def _gate_mix_kernel(g_ref, oc_ref, os_ref, ow_ref, o_ref):
    gates = jax.nn.sigmoid(g_ref[...])
    for h in range(NSA_HEADS):
        hs = slice(h * HEAD_DIM, (h + 1) * HEAD_DIM)
        mix = (gates[:, 3 * h:3 * h + 1] * oc_ref[:, hs]
               + gates[:, 3 * h + 1:3 * h + 2] * os_ref[:, hs]
               + gates[:, 3 * h + 2:3 * h + 3] * ow_ref[:, hs])
        o_ref[:, hs] = mix.astype(o_ref.dtype)


def _gate_mix(gate_logits, o_c, o_s, o_w, *, tm=256):
    n = gate_logits.shape[0]
    row = pl.BlockSpec((tm, NSA_WIDTH), lambda i: (i, 0))
    return pl.pallas_call(
        _gate_mix_kernel,
        grid=(n // tm,),
        in_specs=[pl.BlockSpec((tm, 3 * NSA_HEADS), lambda i: (i, 0)), row, row, row],
        out_specs=row,
        out_shape=jax.ShapeDtypeStruct((n, NSA_WIDTH), BF16),
        compiler_params=_cparams(("parallel",)),
        name="nsa_gate_mix",
    )(gate_logits, o_c, o_s, o_w)


def _nsa_bias_tables(rel_bias, t):
    tq = NSA_TQ
    ncp = t // CMP_STRIDE
    tab = rel_bias.astype(F32)
    dist_c = np.arange(t)[:, None] - (np.arange(ncp)[None, :] * CMP_STRIDE + CMP_LEN - 1)
    bias_c = jnp.transpose(tab[_rel_bucket_np(dist_c)], (2, 0, 1))
    iq = np.arange(tq)[:, None]
    jk = np.arange(tq)[None, :]
    far = _rel_bucket_np(np.arange(2 * tq - (tq - 1), t + tq))
    assert (far == REL_BUCKETS - 1).all()
    idx = np.stack([_rel_bucket_np(iq - jk), _rel_bucket_np(tq + iq - jk),
                    np.full((tq, tq), REL_BUCKETS - 1, np.int32)])
    pat = jnp.transpose(tab[idx], (3, 0, 1, 2))
    pat = pat.reshape(NSA_KV_GROUPS, NSA_HPG, 3, tq, tq).transpose(0, 2, 1, 3, 4)
    return bias_c, pat.reshape(NSA_KV_GROUPS, 3, NSA_HPG * tq, tq)


def _router_kernel(h_ref, w_ref, b_ref, idx_ref, wt_ref, *, tm):
    logits = _dot(h_ref[...], w_ref[...], precision=HI) + b_ref[...]
    lane = lax.broadcasted_iota(I32, (tm, N_EXPERTS), 1).astype(F32)
    out_lane = lax.broadcasted_iota(I32, (tm, LANES), 1)
    work = logits
    idx_out = jnp.zeros((tm, LANES), F32)
    val_out = jnp.zeros((tm, LANES), F32)
    top = None
    den = jnp.zeros((tm, 1), F32)
    for k in range(TOP_K):
        mx = jnp.max(work, axis=-1, keepdims=True)
        idx = jnp.min(jnp.where(work == mx, lane, float(N_EXPERTS)), axis=-1, keepdims=True)
        if top is None:
            top = mx
        e = jnp.exp(mx - top)
        den = den + e
        idx_out = jnp.where(out_lane == k, idx, idx_out)
        val_out = jnp.where(out_lane == k, e, val_out)
        work = jnp.where(lane == idx, -jnp.inf, work)
    idx_ref[...] = idx_out.astype(I32)
    wt_ref[...] = val_out / den


def _router(h, w, bias, *, tm=512):
    n, d = h.shape
    out = pl.BlockSpec((tm, LANES), lambda i: (i, 0))
    return pl.pallas_call(
        functools.partial(_router_kernel, tm=tm),
        grid=(n // tm,),
        in_specs=[pl.BlockSpec((tm, d), lambda i: (i, 0)),
                  pl.BlockSpec((d, N_EXPERTS), lambda i: (0, 0)),
                  pl.BlockSpec((1, N_EXPERTS), lambda i: (0, 0))],
        out_specs=[out, out],
        out_shape=[jax.ShapeDtypeStruct((n, LANES), I32), jax.ShapeDtypeStruct((n, LANES), F32)],
        compiler_params=_cparams(("parallel",)),
        name="moe_router",
    )(h, w, bias.reshape(1, N_EXPERTS))


MOE_TM = 256


def _expert_kernel(te_ref, x_ref, wg_ref, wl_ref, bg_ref, bl_ref, wd_ref, bd_ref, rw_ref, o_ref):
    x = x_ref[...]
    glu = jnp.minimum(_dot(x, wg_ref[0]) + bg_ref[0], SWIGLU_LIMIT)
    lin = jnp.clip(_dot(x, wl_ref[0]) + bl_ref[0], -SWIGLU_LIMIT, SWIGLU_LIMIT)
    act = glu * jax.nn.sigmoid(SWIGLU_ALPHA * glu) * (lin + 1.0)
    y = _dot(act.astype(BF16), wd_ref[0]) + bd_ref[0]
    o_ref[...] = rw_ref[...] * y


def _experts(tile_expert, xs, w_glu, w_lin, b_glu, b_lin, w_down, b_down, row_w):
    p, d = xs.shape
    tm = MOE_TM
    ff = w_glu.shape[-1]
    grid_spec = pltpu.PrefetchScalarGridSpec(
        num_scalar_prefetch=1,
        grid=(p // tm,),
        in_specs=[pl.BlockSpec((tm, d), lambda i, te: (i, 0)),
                  pl.BlockSpec((1, d, ff), lambda i, te: (te[i], 0, 0)),
                  pl.BlockSpec((1, d, ff), lambda i, te: (te[i], 0, 0)),
                  pl.BlockSpec((1, 1, ff), lambda i, te: (te[i], 0, 0)),
                  pl.BlockSpec((1, 1, ff), lambda i, te: (te[i], 0, 0)),
                  pl.BlockSpec((1, ff, d), lambda i, te: (te[i], 0, 0)),
                  pl.BlockSpec((1, 1, d), lambda i, te: (te[i], 0, 0)),
                  pl.BlockSpec((tm, 1), lambda i, te: (i, 0))],
        out_specs=pl.BlockSpec((tm, d), lambda i, te: (i, 0)),
    )
    return pl.pallas_call(
        _expert_kernel,
        grid_spec=grid_spec,
        out_shape=jax.ShapeDtypeStruct((p, d), F32),
        compiler_params=_cparams(("arbitrary",)),
        name="moe_experts",
    )(tile_expert, xs, w_glu, w_lin, b_glu, b_lin, w_down, b_down, row_w)


def _moe(h_f32, h_bf16, w_router, b_router, w_glu, w_lin, b_glu, b_lin, w_down, b_down):
    n, d = h_f32.shape
    tm = MOE_TM
    idx128, wt128 = _router(h_f32, w_router, b_router)
    e_flat = idx128[:, :TOP_K].reshape(-1)
    w_flat = wt128[:, :TOP_K].reshape(-1)
    na = n * TOP_K
    p = na + N_EXPERTS * tm
    order = jnp.argsort(e_flat, stable=True)
    counts = jnp.sum(jax.nn.one_hot(e_flat, N_EXPERTS, dtype=I32), axis=0)
    padded = ((counts + tm - 1) // tm) * tm
    pend = jnp.cumsum(padded)
    pstart = pend - padded
    start = jnp.cumsum(counts) - counts
    sorted_e = e_flat[order]
    dest = pstart[sorted_e] + (jnp.arange(na, dtype=I32) - start[sorted_e])
    row_token = jnp.zeros((p,), I32).at[dest].set((order // TOP_K).astype(I32))
    row_w = jnp.zeros((p,), F32).at[dest].set(w_flat[order])
    slot = jnp.zeros((na,), I32).at[order].set(dest)
    tile_expert = jnp.minimum(
        jnp.searchsorted(pend, jnp.arange(p // tm, dtype=I32) * tm, side="right"), N_EXPERTS - 1).astype(I32)
    xs = jnp.take(h_bf16, row_token, axis=0)
    ys = _experts(tile_expert, xs, w_glu, w_lin, b_glu, b_lin, w_down, b_down, row_w.reshape(p, 1))
    return jnp.sum(jnp.take(ys, slot.reshape(n, TOP_K), axis=0), axis=1)


def kernel(x, ln1_g, ln1_b, ln2_g, ln2_b, ev_w_in, ev_w_out, hg_lb_raw, hg_norm_g, od_w_in, od_w_out,
           cmp_k_w1, cmp_k_w2, cmp_k_pos, cmp_v_w1, cmp_v_w2, cmp_v_pos, rel_bias, router_w, router_b,
           exp_w_gu, exp_b_gu, exp_w_down, exp_b_down):
    b, t, d = x.shape
    n = b * t
    depth = ln1_g.shape[0]
    alpha = (2 * depth) ** 0.25
    lb_soft = jax.nn.softmax(hg_lb_raw.astype(F32), axis=0)
    lower_bounds = jnp.cumsum(lb_soft, axis=0) - lb_soft[0]
    bias_c, bias_pat = _nsa_bias_tables(rel_bias, t)

    h = x.reshape(n, d)
    hb = h.astype(BF16)
    for layer in range(depth):
        if layer % 2 == 0:
            e = layer // 2
            proj = _matmul(hb, ev_w_in[e].astype(BF16))
            o_a = _sb_attention(proj, b, t)
            o_b = _hgrn2(proj, lower_bounds[layer], hg_norm_g[e], b, t)
            mix = _matmul2(o_a, o_b, ev_w_out[e].astype(BF16))
        else:
            o = layer // 2
            w_in = od_w_in[o]
            proj = _matmul(hb, w_in[:, :NSA_WIDTH + 6 * NSA_KV_WIDTH].astype(BF16))
            gate_logits = _matmul(hb, w_in[:, NSA_WIDTH + 6 * NSA_KV_WIDTH:].astype(BF16))
            proj3 = proj.reshape(b, t, -1)
            c0 = NSA_WIDTH // HEAD_DIM
            k_cmp = _compress(proj3, c0, cmp_k_w1[o], cmp_k_w2[o], cmp_k_pos[o])
            v_cmp = _compress(proj3, c0 + NSA_KV_GROUPS, cmp_v_w1[o], cmp_v_w2[o], cmp_v_pos[o])
            o_c, sel_t = _cmp_select(proj3, k_cmp, v_cmp, bias_c, b, t)
            sel = jnp.swapaxes(sel_t, 2, 3)
            o_s = _nsa_flash(proj3, c0 + 2 * NSA_KV_GROUPS, c0 + 3 * NSA_KV_GROUPS, bias_pat, sel, b, t, 0)
            o_w = _nsa_flash(proj3, c0 + 4 * NSA_KV_GROUPS, c0 + 5 * NSA_KV_GROUPS, bias_pat, sel, b, t, WINDOW)
            mixed = _gate_mix(gate_logits, o_c.reshape(n, -1), o_s.reshape(n, -1), o_w.reshape(n, -1))
            mix = _matmul(mixed, od_w_out[o].astype(BF16))
        h, hb = _add_ln(h, mix, ln1_g[layer], ln1_b[layer], alpha)
        w_gu = exp_w_gu[layer]
        ffn = _moe(h, hb, router_w[layer], router_b[layer],
                   w_gu[..., 0::2].astype(BF16), w_gu[..., 1::2].astype(BF16),
                   exp_b_gu[layer][:, None, 0::2], exp_b_gu[layer][:, None, 1::2],
                   exp_w_down[layer].astype(BF16), exp_b_down[layer][:, None, :])
        h, hb = _add_ln(h, ffn, ln2_g[layer], ln2_b[layer], alpha)
    return h.reshape(b, t, d)
```

```python
import functools
import math

import jax
import jax.numpy as jnp
import numpy as np
from jax import lax
from jax.experimental import pallas as pl
from jax.experimental.pallas import tpu as pltpu

F32 = jnp.float32
BF16 = jnp.bfloat16
I32 = jnp.int32

HEAD_DIM = 128
SB_HEADS = 16
HG_HEADS = 16
SB_WIDTH = SB_HEADS * HEAD_DIM
HG_WIDTH = HG_HEADS * HEAD_DIM
F_MIN = 1e-6
NSA_HEADS = 32
NSA_KV_GROUPS = 4
NSA_HPG = NSA_HEADS // NSA_KV_GROUPS
NSA_WIDTH = NSA_HEADS * HEAD_DIM
NSA_KV_WIDTH = NSA_KV_GROUPS * HEAD_DIM
CMP_LEN = 32
CMP_STRIDE = 16
SLC_BLOCK = 64
N_SELECT = 16
WINDOW = 512
REL_BUCKETS = 32
REL_MAX_DIST = 128
N_EXPERTS = 32
TOP_K = 4
EXPERT_FF = 384
SWIGLU_LIMIT = 7.0
SWIGLU_ALPHA = 1.702
LN_EPS = 1e-5
NEG_INF = -1e30
FORCED = 1e9

LANES = 128
VMEM_LIMIT = 56 * 1024 * 1024

HI = lax.Precision.HIGHEST
LOG2E = 1.4426950408889634


def _cparams(sem):
    return pltpu.CompilerParams(dimension_semantics=sem, vmem_limit_bytes=VMEM_LIMIT)


def _dot_nt(a, b, **kw):
    return lax.dot_general(a, b, (((1,), (1,)), ((), ())), preferred_element_type=F32, **kw)


def _dot(a, b, **kw):
    return jnp.dot(a, b, preferred_element_type=F32, **kw)


def _mm_kernel(a_ref, w_ref, o_ref):
    o_ref[...] = _dot(a_ref[...], w_ref[...]).astype(o_ref.dtype)


def _matmul(a, w, *, tm=1024, tn=512, out_dtype=F32):
    m, k = a.shape
    n = w.shape[1]
    tn = min(tn, n)
    assert m % tm == 0 and n % tn == 0
    return pl.pallas_call(
        _mm_kernel,
        grid=(m // tm, n // tn),
        in_specs=[pl.BlockSpec((tm, k), lambda i, j: (i, 0)),
                  pl.BlockSpec((k, tn), lambda i, j: (0, j))],
        out_specs=pl.BlockSpec((tm, tn), lambda i, j: (i, j)),
        out_shape=jax.ShapeDtypeStruct((m, n), out_dtype),
        compiler_params=_cparams(("parallel", "arbitrary")),
        name="matmul",
    )(a, w)


def _mm2_kernel(a1_ref, a2_ref, w1_ref, w2_ref, o_ref):
    o_ref[...] = _dot(a1_ref[...], w1_ref[...]) + _dot(a2_ref[...], w2_ref[...])


def _matmul2(a1, a2, w, *, tm=1024, tn=512):
    m, k1 = a1.shape
    k2 = a2.shape[1]
    n = w.shape[1]
    assert k1 == k2 and w.shape[0] == k1 + k2
    return pl.pallas_call(
        _mm2_kernel,
        grid=(m // tm, n // tn),
        in_specs=[pl.BlockSpec((tm, k1), lambda i, j: (i, 0)),
                  pl.BlockSpec((tm, k2), lambda i, j: (i, 0)),
                  pl.BlockSpec((k1, tn), lambda i, j: (0, j)),
                  pl.BlockSpec((k2, tn), lambda i, j: (1, j))],
        out_specs=pl.BlockSpec((tm, tn), lambda i, j: (i, j)),
        out_shape=jax.ShapeDtypeStruct((m, n), F32),
        compiler_params=_cparams(("parallel", "arbitrary")),
        name="matmul2",
    )(a1, a2, w, w)


def _add_ln_kernel(h_ref, y_ref, g_ref, b_ref, of_ref, ob_ref, *, alpha):
    x = alpha * h_ref[...] + y_ref[...]
    mu = jnp.mean(x, axis=-1, keepdims=True)
    xc = x - mu
    var = jnp.mean(xc * xc, axis=-1, keepdims=True)
    out = xc * lax.rsqrt(var + LN_EPS) * g_ref[...] + b_ref[...]
    of_ref[...] = out
    ob_ref[...] = out.astype(BF16)


def _add_ln(h, y, g, b, alpha, *, tm=256):
    n, d = h.shape
    row = pl.BlockSpec((tm, d), lambda i: (i, 0))
    vec = pl.BlockSpec((1, d), lambda i: (0, 0))
    return pl.pallas_call(
        functools.partial(_add_ln_kernel, alpha=alpha),
        grid=(n // tm,),
        in_specs=[row, row, vec, vec],
        out_specs=[row, row],
        out_shape=[jax.ShapeDtypeStruct((n, d), F32), jax.ShapeDtypeStruct((n, d), BF16)],
        compiler_params=_cparams(("parallel",)),
        name="add_ln",
    )(h, y, g.reshape(1, d), b.reshape(1, d))


SB_TQ = 512
SB_KB = LANES
SB_DEAD = -104.0


def _sb_kernel(q_ref, k_ref, v_ref, u_ref, o_ref, kb_ref, vb_ref, acc_ref, carry_ref, *, tq, scale):
    i = pl.program_id(2)
    kb = SB_KB
    nb = tq // kb

    @pl.when(i == 0)
    def _():
        kb_ref[...] = k_ref[...].astype(BF16)
        vb_ref[...] = v_ref[...].astype(BF16)

    q = q_ref[...].astype(BF16)
    u = u_ref[...]
    acc_ref[...] = jnp.zeros_like(acc_ref)
    carry_ref[...] = jnp.zeros_like(carry_ref)

    def block_terms(z, causal=None):
        sp = jnp.log(1.0 + jnp.exp(-jnp.abs(z)))
        ls = jnp.minimum(z, 0.0) - sp
        lk = ls - z
        if causal is not None:
            lk = jnp.where(causal, lk, 0.0)
        hi = lk.astype(BF16)
        lo = (lk - hi.astype(F32)).astype(BF16)
        rt = _dot(jnp.concatenate([hi, lo], axis=1), u)
        return ls, rt[:, :kb] - lk, rt[:, kb:]

    base = pl.multiple_of(i * tq, tq)
    for j in reversed(range(nb)):
        r0 = j * kb
        rows = tq - r0
        kj = kb_ref[pl.ds(base + r0, kb), :]
        vj = vb_ref[pl.ds(base + r0, kb), :]
        z = _dot_nt(q[r0:], kj) * scale
        causal = lax.broadcasted_iota(I32, (rows, kb), 1) < lax.broadcasted_iota(I32, (rows, kb), 0)
        ls, between, tot = block_terms(z, causal)
        carry = carry_ref[r0:, :]
        w = jnp.where(causal, jnp.exp(ls + between + carry), 0.0)
        acc_ref[r0:, :] += _dot(w.astype(BF16), vj)
        carry_ref[r0:, :] = carry + tot

    def far(state):
        c, _ = state
        start = pl.multiple_of(c * tq, tq)
        z = _dot_nt(q, kb_ref[pl.ds(start, tq), :]) * scale
        terms = [block_terms(z[:, j * kb:(j + 1) * kb]) for j in range(nb)]
        carry = carry_ref[...]
        ws = [None] * nb
        for j in reversed(range(nb)):
            ls, between, tot = terms[j]
            ws[j] = jnp.exp(ls + between + carry).astype(BF16)
            carry = carry + tot
        acc_ref[...] += _dot(jnp.concatenate(ws, axis=1), vb_ref[pl.ds(start, tq), :])
        carry_ref[...] = carry
        return c - 1, jnp.max(carry)

    lax.while_loop(lambda st: (st[0] >= 0) & (st[1] > SB_DEAD), far, (i - 1, jnp.max(carry_ref[...])))
    o_ref[...] = acc_ref[...].astype(o_ref.dtype)


def _sb_attention(proj, b, t):
    tq, kb = min(SB_TQ, t), SB_KB
    nq = t // tq
    r = np.arange(2 * kb)[:, None] % kb
    c = np.arange(2 * kb)[None, :]
    u = jnp.asarray(np.where(c < kb, r >= c, True), BF16)
    kv_spec = lambda off: pl.BlockSpec((t, HEAD_DIM), lambda bi, h, i: (bi, off + h))
    return pl.pallas_call(
        functools.partial(_sb_kernel, tq=tq, scale=HEAD_DIM ** -0.5),
        grid=(b, SB_HEADS, nq),
        in_specs=[pl.BlockSpec((tq, HEAD_DIM), lambda bi, h, i: (bi * nq + i, h)),
                  kv_spec(SB_HEADS), kv_spec(2 * SB_HEADS),
                  pl.BlockSpec((2 * kb, 2 * kb), lambda bi, h, i: (0, 0))],
        out_specs=pl.BlockSpec((tq, HEAD_DIM), lambda bi, h, i: (bi * nq + i, h)),
        out_shape=jax.ShapeDtypeStruct((b * t, SB_WIDTH), BF16),
        scratch_shapes=[pltpu.VMEM((t, HEAD_DIM), BF16), pltpu.VMEM((t, HEAD_DIM), BF16),
                        pltpu.VMEM((tq, HEAD_DIM), F32), pltpu.VMEM((tq, kb), F32)],
        compiler_params=_cparams(("parallel", "parallel", "arbitrary")),
        name="sb_attention",
    )(proj, proj, proj, u)


HG_C = 64
HG_TB = 512


def _hg_tables(c):
    levels = []
    m = c // 2
    while m >= 1:
        levels.append(m)
        m //= 2
    nl = len(levels)
    e = np.zeros((nl + 2, c, c), np.float32)
    mask = np.zeros((nl + 1, c, c), np.float32)
    p = np.arange(c)
    for li, m in enumerate(levels):
        blk = p // (2 * m)
        half = (p // m) % 2
        mid = blk * 2 * m + m - 1
        for r in range(c):
            if half[r] == 1:
                e[li, r, mid[r] + 1:r + 1] = 1.0
            else:
                e[li, r, r + 1:mid[r] + 1] = 1.0
        mask[li] = ((half[:, None] == 1) & (half[None, :] == 0) & (blk[:, None] == blk[None, :]))
    mask[nl] = np.eye(c)
    e[nl] = np.tril(np.ones((c, c)))
    e[nl + 1] = np.triu(np.ones((c, c)), 1)
    return e.reshape((nl + 2) * c, c), mask, nl


def _hg_kernel(qh_ref, fh_ref, ih_ref, gh_ref, lb_ref, ng_ref, e_ref, mask_ref, o_ref, st_ref,
               *, c, tb, nl):
    @pl.when(pl.program_id(2) == 0)
    def _():
        st_ref[...] = jnp.zeros_like(st_ref)

    lb = lb_ref[...]
    ng = ng_ref[...]
    emat = e_ref[...]

    def chunk(ci, carry):
        r0 = pl.multiple_of(ci * c, c)
        rows = pl.ds(r0, c)
        qh = qh_ref[rows, :]
        f = lb + (1.0 - lb) * jax.nn.sigmoid(fh_ref[rows, :])
        g = jnp.log(jnp.maximum(f, F_MIN))
        kk = 1.0 - f
        q = qh * jax.nn.sigmoid(qh)
        v = ih_ref[rows, :]
        vb = v.astype(BF16)
        ex = jnp.exp(_dot(emat, g, precision=HI))
        scores = mask_ref[nl] * _dot_nt(q.astype(BF16), kk.astype(BF16))
        for li in range(nl):
            a = ex[li * c:(li + 1) * c]
            scores = scores + mask_ref[li] * _dot_nt((q * a).astype(BF16), (kk * a).astype(BF16))
        ecum = ex[nl * c:(nl + 1) * c]
        erest = ex[(nl + 1) * c:(nl + 2) * c]
        st = st_ref[...]
        o = _dot(scores.astype(BF16), vb) + _dot_nt((q * ecum).astype(BF16), st.astype(BF16))
        st_ref[...] = ecum[c - 1:c, :] * st + _dot(v.T.astype(BF16), (kk * erest).astype(BF16))
        gh = gh_ref[rows, :]
        o = o * lax.rsqrt(jnp.mean(o * o, axis=-1, keepdims=True) + LN_EPS) * ng
        o_ref[rows, :] = (o * (gh * jax.nn.sigmoid(gh))).astype(o_ref.dtype)
        return carry

    lax.fori_loop(0, tb // c, chunk, 0)


def _hgrn2(proj, lb, norm_g, b, t):
    c, tb = HG_C, min(HG_TB, t)
    e, mask, nl = _hg_tables(c)
    nt = t // tb
    base = 3 * SB_HEADS
    col = lambda k: pl.BlockSpec((tb, HEAD_DIM), lambda bi, h, i: (bi * nt + i, base + k * HG_HEADS + h))
    const2 = lambda a: pl.BlockSpec(a.shape, lambda bi, h, i: (0,) * a.ndim)
    return pl.pallas_call(
        functools.partial(_hg_kernel, c=c, tb=tb, nl=nl),
        grid=(b, HG_HEADS, nt),
        in_specs=[col(0), col(1), col(2), col(3),
                  pl.BlockSpec((1, HEAD_DIM), lambda bi, h, i: (0, h)),
                  pl.BlockSpec((1, HEAD_DIM), lambda bi, h, i: (0, 0)),
                  const2(e), const2(mask)],
        out_specs=pl.BlockSpec((tb, HEAD_DIM), lambda bi, h, i: (bi * nt + i, h)),
        out_shape=jax.ShapeDtypeStruct((b * t, HG_WIDTH), BF16),
        scratch_shapes=[pltpu.VMEM((HEAD_DIM, HEAD_DIM), F32)],
        compiler_params=_cparams(("parallel", "parallel", "arbitrary")),
        name="hgrn2",
    )(proj, proj, proj, proj, lb.reshape(1, HG_WIDTH), norm_g.reshape(1, HEAD_DIM),
      jnp.asarray(e), jnp.asarray(mask))


def _cmp_kernel(a_ref, w1_ref, w2_ref, pos_ref, o_ref, *, ncp):
    half = CMP_LEN // 2
    p = jnp.zeros((ncp, HEAD_DIM), F32)
    q = jnp.zeros((ncp, HEAD_DIM), F32)
    for j in range(half):
        s = a_ref[pl.ds(j, ncp, stride=CMP_STRIDE), :]
        p = p + _dot((s + pos_ref[j:j + 1, :]).astype(BF16), w1_ref[j].astype(BF16))
        q = q + _dot((s + pos_ref[half + j:half + j + 1, :]).astype(BF16), w1_ref[half + j].astype(BF16))
    hid = p + pltpu.roll(q, ncp - 1, 0)
    hid = hid * jax.nn.sigmoid(hid)
    out = _dot(hid.astype(BF16), w2_ref[...].astype(BF16))
    row = lax.broadcasted_iota(I32, (ncp, HEAD_DIM), 0)
    o_ref[0, 0] = jnp.where(row < ncp - 1, out, 0.0)


def _compress(proj3, col0, w1, w2, pos):
    b, t, _ = proj3.shape
    ncp = t // CMP_STRIDE
    assert CMP_LEN == 2 * CMP_STRIDE
    return pl.pallas_call(
        functools.partial(_cmp_kernel, ncp=ncp),
        grid=(b, NSA_KV_GROUPS),
        in_specs=[pl.BlockSpec((None, t, HEAD_DIM), lambda bi, g: (bi, 0, col0 + g)),
                  pl.BlockSpec((CMP_LEN, HEAD_DIM, HEAD_DIM), lambda bi, g: (0, 0, 0)),
                  pl.BlockSpec((HEAD_DIM, HEAD_DIM), lambda bi, g: (0, 0)),
                  pl.BlockSpec((CMP_LEN, HEAD_DIM), lambda bi, g: (0, 0))],
        out_specs=pl.BlockSpec((1, 1, ncp, HEAD_DIM), lambda bi, g: (bi, g, 0, 0)),
        out_shape=jax.ShapeDtypeStruct((b, NSA_KV_GROUPS, ncp, HEAD_DIM), F32),
        compiler_params=_cparams(("parallel", "parallel")),
        name="nsa_compress",
    )(proj3, w1.reshape(CMP_LEN, HEAD_DIM, HEAD_DIM), w2, pos)


NSA_TQ = 128
CMP_NEAR = 16
CMP_NEAR_LO = 9


def _rel_bucket_np(dist):
    dist = np.maximum(dist, 0)
    max_exact = REL_BUCKETS // 2
    ratio = (np.log(np.maximum(dist, max_exact).astype(np.float32) / np.float32(max_exact))
             / np.float32(math.log(REL_MAX_DIST / max_exact)))
    large = np.minimum(max_exact + (ratio * np.float32(REL_BUCKETS - max_exact)).astype(np.int32),
                       REL_BUCKETS - 1)
    return np.where(dist < max_exact, dist, large).astype(np.int32)


def _cmpsel_kernel(q_ref, kc_ref, vc_ref, pb_ref, ovl_ref, oc_ref, sel_ref,
                   *, tq, ncp, nslc, nsel, scale):
    i = pl.program_id(2)
    kc = kc_ref[0, 0].astype(BF16)
    vc = vc_ref[0, 0].astype(BF16)
    qpos = i * tq + lax.broadcasted_iota(I32, (tq, ncp), 0)
    ncol = lax.broadcasted_iota(I32, (tq, ncp), 1)
    valid = (ncol * CMP_STRIDE + (CMP_LEN - 1) <= qpos) & (ncol < ncp - 1)
    sr = lax.broadcasted_iota(I32, (LANES, ncp), 0)
    sc = lax.broadcasted_iota(I32, (LANES, ncp), 1)
    first = i * (tq // CMP_STRIDE) - CMP_NEAR_LO
    place = jnp.where(sr < CMP_NEAR, jnp.where(sc == first + sr, 1.0, 0.0),
                      jnp.where(sr == CMP_NEAR, jnp.where(sc < first, 1.0, 0.0), 0.0))
    bias = _dot(pb_ref[0], place, precision=HI)
    psum = jnp.zeros((tq, ncp), F32)
    for r in range(NSA_HPG):
        hs = slice(r * HEAD_DIM, (r + 1) * HEAD_DIM)
        s = _dot_nt(q_ref[0, :, hs].astype(BF16), kc) * scale + bias[r * tq:(r + 1) * tq]
        s = jnp.where(valid, s, NEG_INF)
        mx = jnp.max(s, axis=-1, keepdims=True)
        e = jnp.where(valid, jnp.exp(s - mx), 0.0)
        den = jnp.sum(e, axis=-1, keepdims=True)
        p = e / jnp.maximum(den, 1e-30)
        oc_ref[0, :, hs] = _dot(p.astype(BF16), vc)
        psum = psum + p
    imp = _dot_nt(ovl_ref[...], psum, precision=HI)
    jrow = lax.broadcasted_iota(I32, (nslc, tq), 0)
    qblk = (i * tq + lax.broadcasted_iota(I32, (nslc, tq), 1)) // SLC_BLOCK
    ok = jrow <= qblk
    forced = (jrow == 0) | (jrow == qblk) | (jrow == qblk - 1)
    imp = jnp.where(ok, jnp.where(forced, FORCED, imp), NEG_INF)
    rank = jnp.zeros((nslc, tq), F32)
    for j2 in range(nslc):
        row = imp[j2:j2 + 1, :]
        tie = jnp.where(jrow > j2, 1.0, 0.0)
        rank = rank + jnp.where(row > imp, 1.0, jnp.where(row == imp, tie, 0.0))
    sel_ref[0, 0] = jnp.where((rank < nsel) & ok, 1.0, 0.0)


def _cmp_select(proj3, k_cmp, v_cmp, pb, b, t):
    tq = NSA_TQ
    ncp = t // CMP_STRIDE
    nslc = t // SLC_BLOCK
    nsel = min(N_SELECT, nslc)
    n_idx = np.arange(ncp)
    slc_start = np.arange(nslc) * SLC_BLOCK
    cs = n_idx * CMP_STRIDE
    ovl = ((cs[None, :] < slc_start[:, None] + SLC_BLOCK)
           & (cs[None, :] + CMP_LEN - 1 >= slc_start[:, None])
           & (n_idx[None, :] < ncp - 1)).astype(np.float32)
    gw = NSA_HPG * HEAD_DIM
    return pl.pallas_call(
        functools.partial(_cmpsel_kernel, tq=tq, ncp=ncp, nslc=nslc, nsel=nsel, scale=HEAD_DIM ** -0.5),
        grid=(b, NSA_KV_GROUPS, t // tq),
        in_specs=[pl.BlockSpec((1, tq, gw), lambda bi, g, i: (bi, i, g)),
                  pl.BlockSpec((1, 1, ncp, HEAD_DIM), lambda bi, g, i: (bi, g, 0, 0)),
                  pl.BlockSpec((1, 1, ncp, HEAD_DIM), lambda bi, g, i: (bi, g, 0, 0)),
                  pl.BlockSpec((1, NSA_HPG * tq, LANES), lambda bi, g, i: (g, 0, 0)),
                  pl.BlockSpec((nslc, ncp), lambda bi, g, i: (0, 0))],
        out_specs=[pl.BlockSpec((1, tq, gw), lambda bi, g, i: (bi, i, g)),
                   pl.BlockSpec((1, 1, nslc, tq), lambda bi, g, i: (bi, g, 0, i))],
        out_shape=[jax.ShapeDtypeStruct((b, t, NSA_WIDTH), F32),
                   jax.ShapeDtypeStruct((b, NSA_KV_GROUPS, nslc, t), F32)],
        compiler_params=_cparams(("parallel", "parallel", "parallel")),
        name="nsa_cmp_select",
    )(proj3, k_cmp, v_cmp, pb, jnp.asarray(ovl))


SEL_CHUNK = 512
MASKED = -1e30
FAR_BIAS_COL = SLC_BLOCK


def _softmax_step(s, vaug, m_ref, acc_ref, c1):
    m_old = m_ref[...]
    m_new = jnp.maximum(m_old, jnp.max(s, axis=-1, keepdims=True))
    p = jnp.exp2((s - m_new) * c1)
    acc_ref[...] = jnp.exp2((m_old - m_new) * c1) * acc_ref[...] + _dot(p.astype(BF16), vaug)
    m_ref[...] = m_new


def _sel_kernel(q_ref, k_ref, v_ref, kext_ref, vext_ref, bnear_ref, bfar_ref, sel_ref, o_ref,
                qa_ref, ka_ref, va_ref, m_ref, acc_ref, *, tq, c1):
    i = pl.program_id(2)
    hpg = NSA_HPG
    rows = hpg * tq
    d = HEAD_DIM

    @pl.when(i == 0)
    def _():
        ka_ref[:, :d] = k_ref[...].astype(BF16)
        ka_ref[:, d:] = kext_ref[...]
        va_ref[:, :d] = v_ref[...].astype(BF16)
        va_ref[:, d:] = vext_ref[...]

    sel = sel_ref[0, 0]
    nslc = sel.shape[1]
    blk = lax.broadcasted_iota(I32, (tq, nslc), 1)
    far_blocks = (i - 1) * (tq // SLC_BLOCK)
    maskq = jnp.where(blk < far_blocks, jnp.where(sel > 0.5, 0.0, MASKED), MASKED)
    ext = jnp.concatenate([maskq, jnp.zeros((tq, LANES - nslc), F32)], axis=1)
    for r in range(hpg):
        qa_ref[r * tq:(r + 1) * tq, :d] = q_ref[0, :, r * d:(r + 1) * d].astype(BF16)
        qa_ref[r * tq:(r + 1) * tq, d:] = (ext + bfar_ref[0, r:r + 1, :]).astype(BF16)
    m_ref[...] = jnp.full_like(m_ref, MASKED)
    acc_ref[...] = jnp.zeros_like(acc_ref)

    def far(c, carry):
        start = pl.multiple_of(c * SEL_CHUNK, SEL_CHUNK)
        s = _dot_nt(qa_ref[...], ka_ref[pl.ds(start, SEL_CHUNK), :])
        _softmax_step(s, va_ref[pl.ds(start, SEL_CHUNK), :], m_ref, acc_ref, c1)
        return carry

    n_far = (jnp.maximum(i - 1, 0) * tq + SEL_CHUNK - 1) // SEL_CHUNK
    lax.fori_loop(0, n_far, far, 0)

    p0 = pl.multiple_of(jnp.maximum(i - 1, 0) * tq, tq)
    p1 = pl.multiple_of(i * tq, tq)
    kn = jnp.concatenate([ka_ref[pl.ds(p0, tq), :d], ka_ref[pl.ds(p1, tq), :d]], axis=0)
    vn = jnp.concatenate([va_ref[pl.ds(p0, tq), :], va_ref[pl.ds(p1, tq), :]], axis=0)
    er = lax.broadcasted_iota(I32, (nslc, 2 * tq), 0)
    ec = lax.broadcasted_iota(I32, (nslc, 2 * tq), 1) // SLC_BLOCK
    expand = jnp.where(er == far_blocks + ec, 1.0, 0.0).astype(BF16)
    picked = _dot(sel.astype(BF16), expand)
    qi = lax.broadcasted_iota(I32, (tq, 2 * tq), 0)
    kj = lax.broadcasted_iota(I32, (tq, 2 * tq), 1)
    keep = jnp.where(kj <= qi + tq, picked, 0.0) > 0.5
    s = _dot_nt(qa_ref[:, :d], kn) + bnear_ref[0]
    s = jnp.where(keep[None], s.reshape(hpg, tq, 2 * tq), MASKED).reshape(rows, 2 * tq)
    _softmax_step(s, vn, m_ref, acc_ref, c1)

    acc = acc_ref[...]
    out = acc[:, :d] / acc[:, d:d + 1]
    for r in range(hpg):
        o_ref[0, :, r * d:(r + 1) * d] = out[r * tq:(r + 1) * tq, :]


def _nsa_selected(proj3, kcol, vcol, bnear, bfar, sel, b, t):
    tq = NSA_TQ
    d = HEAD_DIM
    gw = NSA_HPG * d
    nslc = t // SLC_BLOCK
    assert nslc <= FAR_BIAS_COL and t % SEL_CHUNK == 0
    rows = NSA_HPG * tq
    pos = np.arange(t)
    kext = np.zeros((t, LANES), np.float32)
    kext[pos, pos // SLC_BLOCK] = 1.0
    kext[:, FAR_BIAS_COL:FAR_BIAS_COL + 2] = 1.0
    vext = np.zeros((t, LANES), np.float32)
    vext[:, 0] = 1.0
    kv = lambda col: pl.BlockSpec((None, t, d), lambda bi, g, i: (bi, 0, col + g))
    const = lambda shape: pl.BlockSpec(shape, lambda bi, g, i: (0,) * len(shape))
    return pl.pallas_call(
        functools.partial(_sel_kernel, tq=tq, c1=d ** -0.5 * LOG2E),
        grid=(b, NSA_KV_GROUPS, t // tq),
        in_specs=[pl.BlockSpec((1, tq, gw), lambda bi, g, i: (bi, i, g)),
                  kv(kcol), kv(vcol), const((t, LANES)), const((t, LANES)),
                  pl.BlockSpec((1, rows, 2 * tq), lambda bi, g, i: (g, 0, 0)),
                  pl.BlockSpec((1, NSA_HPG, LANES), lambda bi, g, i: (g, 0, 0)),
                  pl.BlockSpec((1, 1, tq, nslc), lambda bi, g, i: (bi, g, i, 0))],
        out_specs=pl.BlockSpec((1, tq, gw), lambda bi, g, i: (bi, i, g)),
        out_shape=jax.ShapeDtypeStruct((b, t, NSA_WIDTH), F32),
        scratch_shapes=[pltpu.VMEM((rows, 2 * d), BF16),
                        pltpu.VMEM((t, 2 * d), BF16),
                        pltpu.VMEM((t, 2 * d), BF16),
                        pltpu.VMEM((rows, 1), F32),
                        pltpu.VMEM((rows, 2 * d), F32)],
        compiler_params=_cparams(("parallel", "parallel", "arbitrary")),
        name="nsa_selected",
    )(proj3, proj3, proj3, jnp.asarray(kext, BF16), jnp.asarray(vext, BF16), bnear, bfar, sel)


def _win_kernel(q_ref, k_ref, v_ref, bpat_ref, o_ref, kb_ref, vb_ref, *, tq, window, c1):
    i = pl.program_id(2)
    hpg = NSA_HPG
    d = HEAD_DIM
    nt = window // tq + 1

    @pl.when(i == 0)
    def _():
        kb_ref[...] = k_ref[...].astype(BF16)
        vb_ref[...] = v_ref[...].astype(BF16)

    qs = jnp.concatenate([q_ref[0, :, r * d:(r + 1) * d] for r in range(hpg)], axis=0).astype(BF16)
    t0 = jnp.maximum(i - (nt - 1), 0)
    qi = lax.broadcasted_iota(I32, (tq, tq), 0)
    kj = lax.broadcasted_iota(I32, (tq, tq), 1)
    parts = []
    for kk in range(nt):
        dt = i - (t0 + kk)
        start = pl.multiple_of((t0 + kk) * tq, tq)
        z = _dot_nt(qs, kb_ref[pl.ds(start, tq), :]) + bpat_ref[0, jnp.clip(dt, 0, 2)]
        dist = dt * tq + qi - kj
        keep = jnp.where(dist >= 0, jnp.where(dist < window, 1.0, 0.0), 0.0) > 0.5
        parts.append(jnp.where(keep[None], z.reshape(hpg, tq, tq), MASKED))
    s = jnp.concatenate(parts, axis=-1)
    m = jnp.max(s, axis=-1, keepdims=True)
    p = jnp.exp2((s - m) * c1)
    den = jnp.sum(p, axis=-1, keepdims=True)
    vwin = vb_ref[pl.ds(pl.multiple_of(t0 * tq, tq), nt * tq), :]
    out = (_dot(p.reshape(hpg * tq, nt * tq).astype(BF16), vwin).reshape(hpg, tq, d) / den)
    for r in range(hpg):
        o_ref[0, :, r * d:(r + 1) * d] = out[r]


def _nsa_window(proj3, kcol, vcol, bpat, b, t):
    tq = NSA_TQ
    d = HEAD_DIM
    gw = NSA_HPG * d
    assert WINDOW % tq == 0 and t >= WINDOW + tq
    kv = lambda col: pl.BlockSpec((None, t, d), lambda bi, g, i: (bi, 0, col + g))
    return pl.pallas_call(
        functools.partial(_win_kernel, tq=tq, window=WINDOW, c1=d ** -0.5 * LOG2E),
        grid=(b, NSA_KV_GROUPS, t // tq),
        in_specs=[pl.BlockSpec((1, tq, gw), lambda bi, g, i: (bi, i, g)),
                  kv(kcol), kv(vcol),
                  pl.BlockSpec((1, 3, NSA_HPG * tq, tq), lambda bi, g, i: (g, 0, 0, 0))],
        out_specs=pl.BlockSpec((1, tq, gw), lambda bi, g, i: (bi, i, g)),
        out_shape=jax.ShapeDtypeStruct((b, t, NSA_WIDTH), F32),
        scratch_shapes=[pltpu.VMEM((t, d), BF16), pltpu.VMEM((t, d), BF16)],
        compiler_params=_cparams(("parallel", "parallel", "arbitrary")),
        name="nsa_window",
    )(proj3, proj3, proj3, bpat)


def _gate_mix_kernel(g_ref, oc_ref, os_ref, ow_ref, o_ref):
    gates = jax.nn.sigmoid(g_ref[...])
    for h in range(NSA_HEADS):
        hs = slice(h * HEAD_DIM, (h + 1) * HEAD_DIM)
        mix = (gates[:, 3 * h:3 * h + 1] * oc_ref[:, hs]
               + gates[:, 3 * h + 1:3 * h + 2] * os_ref[:, hs]
               + gates[:, 3 * h + 2:3 * h + 3] * ow_ref[:, hs])
        o_ref[:, hs] = mix.astype(o_ref.dtype)


def _gate_mix(gate_logits, o_c, o_s, o_w, *, tm=256):
    n = gate_logits.shape[0]
    row = pl.BlockSpec((tm, NSA_WIDTH), lambda i: (i, 0))
    return pl.pallas_call(
        _gate_mix_kernel,
        grid=(n // tm,),
        in_specs=[pl.BlockSpec((tm, 3 * NSA_HEADS), lambda i: (i, 0)), row, row, row],
        out_specs=row,
        out_shape=jax.ShapeDtypeStruct((n, NSA_WIDTH), BF16),
        compiler_params=_cparams(("parallel",)),
        name="nsa_gate_mix",
    )(gate_logits, o_c, o_s, o_w)


def _nsa_bias_tables(rel_bias, t):
    tq = NSA_TQ
    g, hpg = NSA_KV_GROUPS, NSA_HPG
    inv_scale = HEAD_DIM ** 0.5
    tab = rel_bias.astype(F32)
    last = REL_BUCKETS - 1
    far_from = int(np.nonzero(_rel_bucket_np(np.arange(4 * REL_MAX_DIST)) < last)[0].max()) + 1
    assert (_rel_bucket_np(np.arange(far_from, t + tq)) == last).all()

    def by_group(a):
        a = jnp.moveaxis(a, -1, 0)
        return a.reshape((g, hpg) + a.shape[1:])

    iq = np.arange(tq)[:, None]
    m = np.arange(CMP_NEAR)[None, :]
    dist_c = iq - CMP_STRIDE * (m - CMP_NEAR_LO) - (CMP_LEN - 1)
    assert dist_c[:, 0].min() >= far_from
    assert (iq - CMP_STRIDE * (CMP_NEAR - CMP_NEAR_LO) - (CMP_LEN - 1)).max() < 0
    pb = jnp.zeros((g, hpg, tq, LANES), F32)
    pb = pb.at[..., :CMP_NEAR].set(by_group(tab[_rel_bucket_np(dist_c)]))
    pb = pb.at[..., CMP_NEAR].set(by_group(tab[last])[..., None])
    pb = pb.reshape(g, hpg * tq, LANES)
    assert 2 * tq - (tq - 1) >= far_from
    jk = np.arange(tq)[None, :]
    idx = np.stack([_rel_bucket_np(iq - jk), _rel_bucket_np(tq + iq - jk), np.full((tq, tq), last, np.int32)])
    pat = by_group(tab[idx]) * inv_scale
    bpat = jnp.transpose(pat, (0, 2, 1, 3, 4)).reshape(g, 3, hpg * tq, tq)
    bnear = jnp.concatenate([pat[:, :, 1], pat[:, :, 0]], axis=-1).reshape(g, hpg * tq, 2 * tq)
    far = by_group(tab[last]) * inv_scale
    far_hi = far.astype(BF16).astype(F32)
    bfar = jnp.zeros((g, hpg, LANES), F32)
    bfar = bfar.at[..., FAR_BIAS_COL].set(far_hi).at[..., FAR_BIAS_COL + 1].set(far - far_hi)
    return pb, bnear, bfar, bpat


def _router_kernel(h_ref, w_ref, b_ref, idx_ref, wt_ref, *, tm):
    logits = _dot(h_ref[...], w_ref[...], precision=HI) + b_ref[...]
    lane = lax.broadcasted_iota(I32, (tm, N_EXPERTS), 1).astype(F32)
    out_lane = lax.broadcasted_iota(I32, (tm, LANES), 1)
    work = logits
    idx_out = jnp.zeros((tm, LANES), F32)
    val_out = jnp.zeros((tm, LANES), F32)
    top = None
    den = jnp.zeros((tm, 1), F32)
    for k in range(TOP_K):
        mx = jnp.max(work, axis=-1, keepdims=True)
        idx = jnp.min(jnp.where(work == mx, lane, float(N_EXPERTS)), axis=-1, keepdims=True)
        if top is None:
            top = mx
        e = jnp.exp(mx - top)
        den = den + e
        idx_out = jnp.where(out_lane == k, idx, idx_out)
        val_out = jnp.where(out_lane == k, e, val_out)
        work = jnp.where(lane == idx, -jnp.inf, work)
    idx_ref[...] = idx_out.astype(I32)
    wt_ref[...] = val_out / den


def _router(h, w, bias, *, tm=512):
    n, d = h.shape
    out = pl.BlockSpec((tm, LANES), lambda i: (i, 0))
    return pl.pallas_call(
        functools.partial(_router_kernel, tm=tm),
        grid=(n // tm,),
        in_specs=[pl.BlockSpec((tm, d), lambda i: (i, 0)),
                  pl.BlockSpec((d, N_EXPERTS), lambda i: (0, 0)),
                  pl.BlockSpec((1, N_EXPERTS), lambda i: (0, 0))],
        out_specs=[out, out],
        out_shape=[jax.ShapeDtypeStruct((n, LANES), I32), jax.ShapeDtypeStruct((n, LANES), F32)],
        compiler_params=_cparams(("parallel",)),
        name="moe_router",
    )(h, w, bias.reshape(1, N_EXPERTS))


def _gu_prep_kernel(w_ref, p_ref, o_ref):
    o_ref[0] = _dot(w_ref[0].astype(BF16), p_ref[...]).astype(BF16)


def _gu_prep(w_gu, *, tk=1024):
    e, d, f2 = w_gu.shape
    perm = np.zeros((f2, f2), np.float32)
    perm[np.arange(f2), (np.arange(f2) % 2) * (f2 // 2) + np.arange(f2) // 2] = 1.0
    return pl.pallas_call(
        _gu_prep_kernel,
        grid=(e, d // tk),
        in_specs=[pl.BlockSpec((1, tk, f2), lambda ei, k: (ei, k, 0)),
                  pl.BlockSpec((f2, f2), lambda ei, k: (0, 0))],
        out_specs=pl.BlockSpec((1, tk, f2), lambda ei, k: (ei, k, 0)),
        out_shape=jax.ShapeDtypeStruct((e, d, f2), BF16),
        compiler_params=_cparams(("parallel", "parallel")),
        name="moe_gu_prep",
    )(w_gu, jnp.asarray(perm, BF16))


MOE_TM = 256


def _expert_kernel(te_ref, x_ref, wgu_ref, bg_ref, bl_ref, wd_ref, bd_ref, rw_ref, o_ref):
    ff = bg_ref.shape[-1]
    hgu = _dot(x_ref[...], wgu_ref[0])
    glu = jnp.minimum(hgu[:, :ff] + bg_ref[0], SWIGLU_LIMIT)
    lin = jnp.clip(hgu[:, ff:] + bl_ref[0], -SWIGLU_LIMIT, SWIGLU_LIMIT)
    act = glu * jax.nn.sigmoid(SWIGLU_ALPHA * glu) * (lin + 1.0)
    y = _dot(act.astype(BF16), wd_ref[0]) + bd_ref[0]
    o_ref[...] = rw_ref[...] * y


def _experts(tile_expert, xs, w_gu, b_glu, b_lin, w_down, b_down, row_w):
    p, d = xs.shape
    tm = MOE_TM
    ff = w_down.shape[1]
    by_expert = lambda shape: pl.BlockSpec((1,) + shape, lambda i, te: (te[i], 0, 0))
    grid_spec = pltpu.PrefetchScalarGridSpec(
        num_scalar_prefetch=1,
        grid=(p // tm,),
        in_specs=[pl.BlockSpec((tm, d), lambda i, te: (i, 0)),
                  by_expert((d, 2 * ff)), by_expert((1, ff)), by_expert((1, ff)),
                  by_expert((ff, d)), by_expert((1, d)),
                  pl.BlockSpec((tm, 1), lambda i, te: (i, 0))],
        out_specs=pl.BlockSpec((tm, d), lambda i, te: (i, 0)),
    )
    return pl.pallas_call(
        _expert_kernel,
        grid_spec=grid_spec,
        out_shape=jax.ShapeDtypeStruct((p, d), F32),
        compiler_params=_cparams(("arbitrary",)),
        name="moe_experts",
    )(tile_expert, xs, w_gu, b_glu, b_lin, w_down, b_down, row_w)


def _moe(h_f32, h_bf16, w_router, b_router, w_gu, b_glu, b_lin, w_down, b_down):
    n, d = h_f32.shape
    tm = MOE_TM
    idx128, wt128 = _router(h_f32, w_router, b_router)
    e_flat = idx128[:, :TOP_K].reshape(-1)
    w_flat = wt128[:, :TOP_K].reshape(-1)
    na = n * TOP_K
    p = na + N_EXPERTS * tm
    order = jnp.argsort(e_flat, stable=True)
    counts = jnp.sum(jax.nn.one_hot(e_flat, N_EXPERTS, dtype=I32), axis=0)
    padded = ((counts + tm - 1) // tm) * tm
    pend = jnp.cumsum(padded)
    pstart = pend - padded
    start = jnp.cumsum(counts) - counts
    sorted_e = e_flat[order]
    dest = pstart[sorted_e] + (jnp.arange(na, dtype=I32) - start[sorted_e])
    row_token = jnp.zeros((p,), I32).at[dest].set((order // TOP_K).astype(I32))
    row_w = jnp.zeros((p,), F32).at[dest].set(w_flat[order])
    slot = jnp.zeros((na,), I32).at[order].set(dest)
    tile_expert = jnp.minimum(
        jnp.searchsorted(pend, jnp.arange(p // tm, dtype=I32) * tm, side="right"), N_EXPERTS - 1).astype(I32)
    xs = jnp.take(h_bf16, row_token, axis=0)
    ys = _experts(tile_expert, xs, w_gu, b_glu, b_lin, w_down, b_down, row_w.reshape(p, 1))
    return jnp.sum(jnp.take(ys, slot.reshape(n, TOP_K), axis=0), axis=1)


def kernel(x, ln1_g, ln1_b, ln2_g, ln2_b, ev_w_in, ev_w_out, hg_lb_raw, hg_norm_g, od_w_in, od_w_out,
           cmp_k_w1, cmp_k_w2, cmp_k_pos, cmp_v_w1, cmp_v_w2, cmp_v_pos, rel_bias, router_w, router_b,
           exp_w_gu, exp_b_gu, exp_w_down, exp_b_down):
    b, t, d = x.shape
    n = b * t
    depth = ln1_g.shape[0]
    alpha = (2 * depth) ** 0.25
    lb_soft = jax.nn.softmax(hg_lb_raw.astype(F32), axis=0)
    lower_bounds = jnp.cumsum(lb_soft, axis=0) - lb_soft[0]
    pb, bnear, bfar, bpat = _nsa_bias_tables(rel_bias, t)

    h = x.reshape(n, d)
    hb = h.astype(BF16)
    for layer in range(depth):
        if layer % 2 == 0:
            e = layer // 2
            proj = _matmul(hb, ev_w_in[e].astype(BF16))
            o_a = _sb_attention(proj, b, t)
            o_b = _hgrn2(proj, lower_bounds[layer], hg_norm_g[e], b, t)
            mix = _matmul2(o_a, o_b, ev_w_out[e].astype(BF16))
        else:
            o = layer // 2
            w_in = od_w_in[o]
            proj = _matmul(hb, w_in[:, :NSA_WIDTH + 6 * NSA_KV_WIDTH].astype(BF16))
            gate_logits = _matmul(hb, w_in[:, NSA_WIDTH + 6 * NSA_KV_WIDTH:].astype(BF16))
            proj3 = proj.reshape(b, t, -1)
            c0 = NSA_WIDTH // HEAD_DIM
            k_cmp = _compress(proj3, c0, cmp_k_w1[o], cmp_k_w2[o], cmp_k_pos[o])
            v_cmp = _compress(proj3, c0 + NSA_KV_GROUPS, cmp_v_w1[o], cmp_v_w2[o], cmp_v_pos[o])
            o_c, sel_t = _cmp_select(proj3, k_cmp, v_cmp, pb, b, t)
            sel = jnp.swapaxes(sel_t, 2, 3)
            o_s = _nsa_selected(proj3, c0 + 2 * NSA_KV_GROUPS, c0 + 3 * NSA_KV_GROUPS, bnear, bfar, sel, b, t)
            o_w = _nsa_window(proj3, c0 + 4 * NSA_KV_GROUPS, c0 + 5 * NSA_KV_GROUPS, bpat, b, t)
            mixed = _gate_mix(gate_logits, o_c.reshape(n, -1), o_s.reshape(n, -1), o_w.reshape(n, -1))
            mix = _matmul(mixed, od_w_out[o].astype(BF16))
        h, hb = _add_ln(h, mix, ln1_g[layer], ln1_b[layer], alpha)
        ffn = _moe(h, hb, router_w[layer], router_b[layer], _gu_prep(exp_w_gu[layer]),
                   exp_b_gu[layer][:, None, 0::2], exp_b_gu[layer][:, None, 1::2],
                   exp_w_down[layer].astype(BF16), exp_b_down[layer][:, None, :])
        h, hb = _add_ln(h, ffn, ln2_g[layer], ln2_b[layer], alpha)
    return h.reshape(b, t, d)
```

```python
import functools
import math

import jax
import jax.numpy as jnp
import numpy as np
from jax import lax
from jax.experimental import pallas as pl
from jax.experimental.pallas import tpu as pltpu

F32 = jnp.float32
BF16 = jnp.bfloat16
I32 = jnp.int32

HEAD_DIM = 128
SB_HEADS = 16
HG_HEADS = 16
SB_WIDTH = SB_HEADS * HEAD_DIM
HG_WIDTH = HG_HEADS * HEAD_DIM
F_MIN = 1e-6
NSA_HEADS = 32
NSA_KV_GROUPS = 4
NSA_HPG = NSA_HEADS // NSA_KV_GROUPS
NSA_WIDTH = NSA_HEADS * HEAD_DIM
NSA_KV_WIDTH = NSA_KV_GROUPS * HEAD_DIM
CMP_LEN = 32
CMP_STRIDE = 16
SLC_BLOCK = 64
N_SELECT = 16
WINDOW = 512
REL_BUCKETS = 32
REL_MAX_DIST = 128
N_EXPERTS = 32
TOP_K = 4
EXPERT_FF = 384
SWIGLU_LIMIT = 7.0
SWIGLU_ALPHA = 1.702
LN_EPS = 1e-5
NEG_INF = -1e30
FORCED = 1e9

LANES = 128
VMEM_LIMIT = 56 * 1024 * 1024

HI = lax.Precision.HIGHEST
LOG2E = 1.4426950408889634


def _cparams(sem):
    return pltpu.CompilerParams(dimension_semantics=sem, vmem_limit_bytes=VMEM_LIMIT)


def _dot_nt(a, b, **kw):
    return lax.dot_general(a, b, (((1,), (1,)), ((), ())), preferred_element_type=F32, **kw)


def _dot(a, b, **kw):
    return jnp.dot(a, b, preferred_element_type=F32, **kw)


def _mm_kernel(a_ref, w_ref, o_ref):
    o_ref[...] = _dot(a_ref[...], w_ref[...].astype(BF16)).astype(o_ref.dtype)


def _matmul(a, w, *, n=None, tm=1024, tn=512, out_dtype=F32):
    m, k = a.shape
    n = w.shape[1] if n is None else n
    tn = min(tn, n)
    assert m % tm == 0 and n % tn == 0
    return pl.pallas_call(
        _mm_kernel,
        grid=(m // tm, n // tn),
        in_specs=[pl.BlockSpec((tm, k), lambda i, j: (i, 0)),
                  pl.BlockSpec((k, tn), lambda i, j: (0, j))],
        out_specs=pl.BlockSpec((tm, tn), lambda i, j: (i, j)),
        out_shape=jax.ShapeDtypeStruct((m, n), out_dtype),
        compiler_params=_cparams(("parallel", "arbitrary")),
        name="matmul",
    )(a, w)


def _mm2_kernel(a1_ref, a2_ref, w1_ref, w2_ref, o_ref):
    o_ref[...] = (_dot(a1_ref[...], w1_ref[...].astype(BF16))
                  + _dot(a2_ref[...], w2_ref[...].astype(BF16)))


def _matmul2(a1, a2, w, *, tm=1024, tn=512):
    m, k1 = a1.shape
    k2 = a2.shape[1]
    n = w.shape[1]
    assert k1 == k2 and w.shape[0] == k1 + k2
    return pl.pallas_call(
        _mm2_kernel,
        grid=(m // tm, n // tn),
        in_specs=[pl.BlockSpec((tm, k1), lambda i, j: (i, 0)),
                  pl.BlockSpec((tm, k2), lambda i, j: (i, 0)),
                  pl.BlockSpec((k1, tn), lambda i, j: (0, j)),
                  pl.BlockSpec((k2, tn), lambda i, j: (1, j))],
        out_specs=pl.BlockSpec((tm, tn), lambda i, j: (i, j)),
        out_shape=jax.ShapeDtypeStruct((m, n), F32),
        compiler_params=_cparams(("parallel", "arbitrary")),
        name="matmul2",
    )(a1, a2, w, w)


def _ln_store(x, g_ref, b_ref, of_ref, ob_ref):
    mu = jnp.mean(x, axis=-1, keepdims=True)
    xc = x - mu
    var = jnp.mean(xc * xc, axis=-1, keepdims=True)
    out = xc * lax.rsqrt(var + LN_EPS) * g_ref[...] + b_ref[...]
    of_ref[...] = out
    ob_ref[...] = out.astype(BF16)


def _add_ln_kernel(h_ref, y_ref, g_ref, b_ref, of_ref, ob_ref, *, alpha):
    _ln_store(alpha * h_ref[...] + y_ref[...], g_ref, b_ref, of_ref, ob_ref)


def _combine_ln_kernel(h_ref, *refs, alpha):
    ys, (w_ref, g_ref, b_ref, of_ref, ob_ref) = refs[:TOP_K], refs[TOP_K:]
    w = w_ref[...]
    x = alpha * h_ref[...]
    for k in range(TOP_K):
        x = x + w[:, k:k + 1] * ys[k][...]
    _ln_store(x, g_ref, b_ref, of_ref, ob_ref)


def _combine_ln(h, ys, w128, g, b, alpha, *, tm=128):
    n, d = h.shape
    row = pl.BlockSpec((tm, d), lambda i: (i, 0))
    vec = pl.BlockSpec((1, d), lambda i: (0, 0))
    return pl.pallas_call(
        functools.partial(_combine_ln_kernel, alpha=alpha),
        grid=(n // tm,),
        in_specs=[row] * (1 + TOP_K) + [pl.BlockSpec((tm, LANES), lambda i: (i, 0)), vec, vec],
        out_specs=[row, row],
        out_shape=[jax.ShapeDtypeStruct((n, d), F32), jax.ShapeDtypeStruct((n, d), BF16)],
        compiler_params=_cparams(("parallel",)),
        name="moe_combine_ln",
    )(h, *ys, w128, g.reshape(1, d), b.reshape(1, d))


def _add_ln(h, y, g, b, alpha, *, tm=256):
    n, d = h.shape
    row = pl.BlockSpec((tm, d), lambda i: (i, 0))
    vec = pl.BlockSpec((1, d), lambda i: (0, 0))
    return pl.pallas_call(
        functools.partial(_add_ln_kernel, alpha=alpha),
        grid=(n // tm,),
        in_specs=[row, row, vec, vec],
        out_specs=[row, row],
        out_shape=[jax.ShapeDtypeStruct((n, d), F32), jax.ShapeDtypeStruct((n, d), BF16)],
        compiler_params=_cparams(("parallel",)),
        name="add_ln",
    )(h, y, g.reshape(1, d), b.reshape(1, d))


SB_TQ = 512
SB_KB = LANES
SB_DEAD = -104.0


def _sb_kernel(q_ref, k_ref, v_ref, u_ref, o_ref, kb_ref, vb_ref, acc_ref, carry_ref, *, tq, scale):
    i = pl.program_id(2)
    kb = SB_KB
    nb = tq // kb

    @pl.when(i == 0)
    def _():
        kb_ref[...] = k_ref[...].astype(BF16)
        vb_ref[...] = v_ref[...].astype(BF16)

    q = q_ref[...].astype(BF16)
    u = u_ref[...]
    acc_ref[...] = jnp.zeros_like(acc_ref)
    carry_ref[...] = jnp.zeros_like(carry_ref)

    def block_terms(z, causal=None):
        sp = jnp.log(1.0 + jnp.exp(-jnp.abs(z)))
        ls = jnp.minimum(z, 0.0) - sp
        lk = ls - z
        if causal is not None:
            lk = jnp.where(causal, lk, 0.0)
        hi = lk.astype(BF16)
        lo = (lk - hi.astype(F32)).astype(BF16)
        rt = _dot(jnp.concatenate([hi, lo], axis=1), u)
        return ls, rt[:, :kb] - lk, rt[:, kb:]

    base = pl.multiple_of(i * tq, tq)
    for j in reversed(range(nb)):
        r0 = j * kb
        rows = tq - r0
        kj = kb_ref[pl.ds(base + r0, kb), :]
        vj = vb_ref[pl.ds(base + r0, kb), :]
        z = _dot_nt(q[r0:], kj) * scale
        causal = lax.broadcasted_iota(I32, (rows, kb), 1) < lax.broadcasted_iota(I32, (rows, kb), 0)
        ls, between, tot = block_terms(z, causal)
        carry = carry_ref[r0:, :]
        w = jnp.where(causal, jnp.exp(ls + between + carry), 0.0)
        acc_ref[r0:, :] += _dot(w.astype(BF16), vj)
        carry_ref[r0:, :] = carry + tot

    def far(state):
        c, _ = state
        start = pl.multiple_of(c * tq, tq)
        z = _dot_nt(q, kb_ref[pl.ds(start, tq), :]) * scale
        terms = [block_terms(z[:, j * kb:(j + 1) * kb]) for j in range(nb)]
        carry = carry_ref[...]
        ws = [None] * nb
        for j in reversed(range(nb)):
            ls, between, tot = terms[j]
            ws[j] = jnp.exp(ls + between + carry).astype(BF16)
            carry = carry + tot
        acc_ref[...] += _dot(jnp.concatenate(ws, axis=1), vb_ref[pl.ds(start, tq), :])
        carry_ref[...] = carry
        return c - 1, jnp.max(carry)

    lax.while_loop(lambda st: (st[0] >= 0) & (st[1] > SB_DEAD), far, (i - 1, jnp.max(carry_ref[...])))
    o_ref[...] = acc_ref[...].astype(o_ref.dtype)


def _sb_attention(proj, b, t):
    tq, kb = min(SB_TQ, t), SB_KB
    nq = t // tq
    r = np.arange(2 * kb)[:, None] % kb
    c = np.arange(2 * kb)[None, :]
    u = jnp.asarray(np.where(c < kb, r >= c, True), BF16)
    kv_spec = lambda off: pl.BlockSpec((t, HEAD_DIM), lambda bi, h, i: (bi, off + h))
    return pl.pallas_call(
        functools.partial(_sb_kernel, tq=tq, scale=HEAD_DIM ** -0.5),
        grid=(b, SB_HEADS, nq),
        in_specs=[pl.BlockSpec((tq, HEAD_DIM), lambda bi, h, i: (bi * nq + i, h)),
                  kv_spec(SB_HEADS), kv_spec(2 * SB_HEADS),
                  pl.BlockSpec((2 * kb, 2 * kb), lambda bi, h, i: (0, 0))],
        out_specs=pl.BlockSpec((tq, HEAD_DIM), lambda bi, h, i: (bi * nq + i, h)),
        out_shape=jax.ShapeDtypeStruct((b * t, SB_WIDTH), BF16),
        scratch_shapes=[pltpu.VMEM((t, HEAD_DIM), BF16), pltpu.VMEM((t, HEAD_DIM), BF16),
                        pltpu.VMEM((tq, HEAD_DIM), F32), pltpu.VMEM((tq, kb), F32)],
        compiler_params=_cparams(("parallel", "parallel", "arbitrary")),
        name="sb_attention",
    )(proj, proj, proj, u)


HG_C = 64
HG_TB = 512
HG_NH = 4


def _hg_tables(c):
    levels = []
    m = c // 2
    while m >= 1:
        levels.append(m)
        m //= 2
    nl = len(levels)
    e = np.zeros((nl + 2, c, c), np.float32)
    mask = np.zeros((nl + 1, c, c), np.float32)
    p = np.arange(c)
    for li, m in enumerate(levels):
        blk = p // (2 * m)
        half = (p // m) % 2
        mid = blk * 2 * m + m - 1
        for r in range(c):
            if half[r] == 1:
                e[li, r, mid[r] + 1:r + 1] = 1.0
            else:
                e[li, r, r + 1:mid[r] + 1] = 1.0
        mask[li] = ((half[:, None] == 1) & (half[None, :] == 0) & (blk[:, None] == blk[None, :]))
    mask[nl] = np.eye(c)
    e[nl] = np.tril(np.ones((c, c)))
    e[nl + 1] = np.triu(np.ones((c, c)), 1)
    return e.reshape((nl + 2) * c, c), mask, nl


def _hg_kernel(qh_ref, fh_ref, ih_ref, gh_ref, lb_ref, ng_ref, e_ref, mask_ref, o_ref, st_ref,
               *, c, tb, nl, nh):
    @pl.when(pl.program_id(2) == 0)
    def _():
        st_ref[...] = jnp.zeros_like(st_ref)

    ng = ng_ref[...]
    emat = e_ref[...]

    def one_head(rows, hh):
        cols = slice(hh * HEAD_DIM, (hh + 1) * HEAD_DIM)
        lb = lb_ref[:, cols]
        qh = qh_ref[rows, cols]
        f = lb + (1.0 - lb) * jax.nn.sigmoid(fh_ref[rows, cols])
        g = jnp.log(jnp.maximum(f, F_MIN))
        kk = 1.0 - f
        q = qh * jax.nn.sigmoid(qh)
        v = ih_ref[rows, cols]
        vb = v.astype(BF16)
        g1 = g.astype(BF16)
        r1 = g - g1.astype(F32)
        g2 = r1.astype(BF16)
        g3 = (r1 - g2.astype(F32)).astype(BF16)
        ex = jnp.exp(_dot(emat, jnp.concatenate([g1, g2, g3], axis=0)))
        scores = mask_ref[nl] * _dot_nt(q.astype(BF16), kk.astype(BF16))
        for li in range(nl):
            a = ex[li * c:(li + 1) * c]
            scores = scores + mask_ref[li] * _dot_nt((q * a).astype(BF16), (kk * a).astype(BF16))
        ecum = ex[nl * c:(nl + 1) * c]
        erest = ex[(nl + 1) * c:(nl + 2) * c]
        st = st_ref[hh]
        o = _dot(scores.astype(BF16), vb) + _dot_nt((q * ecum).astype(BF16), st.astype(BF16))
        st_ref[hh] = ecum[c - 1:c, :] * st + _dot(v.T.astype(BF16), (kk * erest).astype(BF16))
        gh = gh_ref[rows, cols]
        o = o * lax.rsqrt(jnp.mean(o * o, axis=-1, keepdims=True) + LN_EPS) * ng
        o_ref[rows, cols] = (o * (gh * jax.nn.sigmoid(gh))).astype(o_ref.dtype)

    def chunk(ci, carry):
        rows = pl.ds(pl.multiple_of(ci * c, c), c)
        for hh in range(nh):
            one_head(rows, hh)
        return carry

    lax.fori_loop(0, tb // c, chunk, 0)


def _hgrn2(proj, lb, norm_g, b, t):
    c, tb, nh = HG_C, min(HG_TB, t), HG_NH
    e, mask, nl = _hg_tables(c)
    e3 = np.concatenate([e, e, e], axis=1)
    nt = t // tb
    base = 3 * SB_HEADS
    assert base % nh == 0 and HG_HEADS % nh == 0
    wide = nh * HEAD_DIM
    col = lambda k: pl.BlockSpec((tb, wide), lambda bi, h, i: (bi * nt + i, (base + k * HG_HEADS) // nh + h))
    const2 = lambda a: pl.BlockSpec(a.shape, lambda bi, h, i: (0,) * a.ndim)
    return pl.pallas_call(
        functools.partial(_hg_kernel, c=c, tb=tb, nl=nl, nh=nh),
        grid=(b, HG_HEADS // nh, nt),
        in_specs=[col(0), col(1), col(2), col(3),
                  pl.BlockSpec((1, wide), lambda bi, h, i: (0, h)),
                  pl.BlockSpec((1, HEAD_DIM), lambda bi, h, i: (0, 0)),
                  const2(e3), const2(mask)],
        out_specs=pl.BlockSpec((tb, wide), lambda bi, h, i: (bi * nt + i, h)),
        out_shape=jax.ShapeDtypeStruct((b * t, HG_WIDTH), BF16),
        scratch_shapes=[pltpu.VMEM((nh, HEAD_DIM, HEAD_DIM), F32)],
        compiler_params=_cparams(("parallel", "parallel", "arbitrary")),
        name="hgrn2",
    )(proj, proj, proj, proj, lb.reshape(1, HG_WIDTH), norm_g.reshape(1, HEAD_DIM),
      jnp.asarray(e3, BF16), jnp.asarray(mask))


def _cmp_kernel(a_ref, w1_ref, w2_ref, pos_ref, o_ref, *, ncp):
    half = CMP_LEN // 2
    p = jnp.zeros((ncp, HEAD_DIM), F32)
    q = jnp.zeros((ncp, HEAD_DIM), F32)
    for j in range(half):
        s = a_ref[pl.ds(j, ncp, stride=CMP_STRIDE), :]
        p = p + _dot((s + pos_ref[j:j + 1, :]).astype(BF16), w1_ref[j].astype(BF16))
        q = q + _dot((s + pos_ref[half + j:half + j + 1, :]).astype(BF16), w1_ref[half + j].astype(BF16))
    hid = p + pltpu.roll(q, ncp - 1, 0)
    hid = hid * jax.nn.sigmoid(hid)
    out = _dot(hid.astype(BF16), w2_ref[...].astype(BF16))
    row = lax.broadcasted_iota(I32, (ncp, HEAD_DIM), 0)
    o_ref[0, 0] = jnp.where(row < ncp - 1, out, 0.0)


def _compress(proj3, col0, w1, w2, pos):
    b, t, _ = proj3.shape
    ncp = t // CMP_STRIDE
    assert CMP_LEN == 2 * CMP_STRIDE
    return pl.pallas_call(
        functools.partial(_cmp_kernel, ncp=ncp),
        grid=(b, NSA_KV_GROUPS),
        in_specs=[pl.BlockSpec((None, t, HEAD_DIM), lambda bi, g: (bi, 0, col0 + g)),
                  pl.BlockSpec((CMP_LEN, HEAD_DIM, HEAD_DIM), lambda bi, g: (0, 0, 0)),
                  pl.BlockSpec((HEAD_DIM, HEAD_DIM), lambda bi, g: (0, 0)),
                  pl.BlockSpec((CMP_LEN, HEAD_DIM), lambda bi, g: (0, 0))],
        out_specs=pl.BlockSpec((1, 1, ncp, HEAD_DIM), lambda bi, g: (bi, g, 0, 0)),
        out_shape=jax.ShapeDtypeStruct((b, NSA_KV_GROUPS, ncp, HEAD_DIM), F32),
        compiler_params=_cparams(("parallel", "parallel")),
        name="nsa_compress",
    )(proj3, w1.reshape(CMP_LEN, HEAD_DIM, HEAD_DIM), w2, pos)


NSA_TQ = 128
CMP_NEAR = 16
CMP_NEAR_LO = 9


def _rel_bucket_np(dist):
    dist = np.maximum(dist, 0)
    max_exact = REL_BUCKETS // 2
    ratio = (np.log(np.maximum(dist, max_exact).astype(np.float32) / np.float32(max_exact))
             / np.float32(math.log(REL_MAX_DIST / max_exact)))
    large = np.minimum(max_exact + (ratio * np.float32(REL_BUCKETS - max_exact)).astype(np.int32),
                       REL_BUCKETS - 1)
    return np.where(dist < max_exact, dist, large).astype(np.int32)


def _cmpsel_kernel(q_ref, kc_ref, vc_ref, pb_ref, ovl_ref, oc_ref, sel_ref,
                   *, tq, ncp, nslc, nsel, scale):
    i = pl.program_id(2)
    kc = kc_ref[0, 0].astype(BF16)
    vc = vc_ref[0, 0].astype(BF16)
    qpos = i * tq + lax.broadcasted_iota(I32, (tq, ncp), 0)
    ncol = lax.broadcasted_iota(I32, (tq, ncp), 1)
    valid = (ncol * CMP_STRIDE + (CMP_LEN - 1) <= qpos) & (ncol < ncp - 1)
    sr = lax.broadcasted_iota(I32, (LANES, ncp), 0)
    sc = lax.broadcasted_iota(I32, (LANES, ncp), 1)
    first = i * (tq // CMP_STRIDE) - CMP_NEAR_LO
    place = jnp.where(sr < CMP_NEAR, jnp.where(sc == first + sr, 1.0, 0.0),
                      jnp.where(sr == CMP_NEAR, jnp.where(sc < first, 1.0, 0.0), 0.0))
    place = place.astype(BF16)
    bias = _dot(pb_ref[0], jnp.concatenate([place, place], axis=0))
    psum = jnp.zeros((tq, ncp), F32)
    for r in range(NSA_HPG):
        hs = slice(r * HEAD_DIM, (r + 1) * HEAD_DIM)
        s = _dot_nt(q_ref[0, :, hs].astype(BF16), kc) * scale + bias[r * tq:(r + 1) * tq]
        s = jnp.where(valid, s, NEG_INF)
        mx = jnp.max(s, axis=-1, keepdims=True)
        e = jnp.where(valid, jnp.exp(s - mx), 0.0)
        den = jnp.sum(e, axis=-1, keepdims=True)
        p = e / jnp.maximum(den, 1e-30)
        oc_ref[0, :, hs] = _dot(p.astype(BF16), vc)
        psum = psum + p
    imp = _dot_nt(ovl_ref[...], psum, precision=HI)
    jrow = lax.broadcasted_iota(I32, (nslc, tq), 0)
    qblk = (i * tq + lax.broadcasted_iota(I32, (nslc, tq), 1)) // SLC_BLOCK
    ok = jrow <= qblk
    forced = (jrow == 0) | (jrow == qblk) | (jrow == qblk - 1)
    imp = jnp.where(ok, jnp.where(forced, FORCED, imp), NEG_INF)
    rank = jnp.zeros((nslc, tq), F32)
    for j2 in range(nslc):
        row = imp[j2:j2 + 1, :]
        tie = jnp.where(jrow > j2, 1.0, 0.0)
        rank = rank + jnp.where(row > imp, 1.0, jnp.where(row == imp, tie, 0.0))
    sel_ref[0, 0] = jnp.where((rank < nsel) & ok, 1.0, 0.0)


def _cmp_select(proj3, k_cmp, v_cmp, pb, b, t):
    tq = NSA_TQ
    ncp = t // CMP_STRIDE
    nslc = t // SLC_BLOCK
    nsel = min(N_SELECT, nslc)
    n_idx = np.arange(ncp)
    slc_start = np.arange(nslc) * SLC_BLOCK
    cs = n_idx * CMP_STRIDE
    ovl = ((cs[None, :] < slc_start[:, None] + SLC_BLOCK)
           & (cs[None, :] + CMP_LEN - 1 >= slc_start[:, None])
           & (n_idx[None, :] < ncp - 1)).astype(np.float32)
    gw = NSA_HPG * HEAD_DIM
    return pl.pallas_call(
        functools.partial(_cmpsel_kernel, tq=tq, ncp=ncp, nslc=nslc, nsel=nsel, scale=HEAD_DIM ** -0.5),
        grid=(b, NSA_KV_GROUPS, t // tq),
        in_specs=[pl.BlockSpec((1, tq, gw), lambda bi, g, i: (bi, i, g)),
                  pl.BlockSpec((1, 1, ncp, HEAD_DIM), lambda bi, g, i: (bi, g, 0, 0)),
                  pl.BlockSpec((1, 1, ncp, HEAD_DIM), lambda bi, g, i: (bi, g, 0, 0)),
                  pl.BlockSpec((1, NSA_HPG * tq, 2 * LANES), lambda bi, g, i: (g, 0, 0)),
                  pl.BlockSpec((nslc, ncp), lambda bi, g, i: (0, 0))],
        out_specs=[pl.BlockSpec((1, tq, gw), lambda bi, g, i: (bi, i, g)),
                   pl.BlockSpec((1, 1, nslc, tq), lambda bi, g, i: (bi, g, 0, i))],
        out_shape=[jax.ShapeDtypeStruct((b, t, NSA_WIDTH), F32),
                   jax.ShapeDtypeStruct((b, NSA_KV_GROUPS, nslc, t), F32)],
        compiler_params=_cparams(("parallel", "parallel", "parallel")),
        name="nsa_cmp_select",
    )(proj3, k_cmp, v_cmp, pb, jnp.asarray(ovl))


SEL_CHUNK = 512
MASKED = -1e30
FAR_BIAS_COL = SLC_BLOCK


def _softmax_step(st, vt, m_ref, acc_ref, c1):
    m_old = m_ref[...]
    m_new = jnp.maximum(m_old, jnp.max(st, axis=0, keepdims=True))
    p = jnp.exp2((st - m_new) * c1)
    acc_ref[...] = jnp.exp2((m_old - m_new) * c1) * acc_ref[...] + _dot(vt, p.astype(BF16))
    m_ref[...] = m_new


def _sel_kernel(q_ref, k_ref, v_ref, kext_ref, vext_ref, bnear_ref, bfar_ref, sel_ref, o_ref,
                qa_ref, ka_ref, vt_ref, m_ref, acc_ref, *, tq, c1):
    i = pl.program_id(2)
    hpg = NSA_HPG
    d = HEAD_DIM

    @pl.when(i == 0)
    def _():
        ka_ref[:, :d] = k_ref[...].astype(BF16)
        ka_ref[:, d:] = kext_ref[...]
        for c in range(vt_ref.shape[0]):
            vt_ref[c, :d, :] = v_ref[c * SEL_CHUNK:(c + 1) * SEL_CHUNK, :].T.astype(BF16)
            vt_ref[c, d:, :] = vext_ref[...]

    sel = sel_ref[0, 0]
    nslc = sel.shape[1]
    blk = lax.broadcasted_iota(I32, (tq, nslc), 1)
    far_blocks = (i - 1) * (tq // SLC_BLOCK)
    maskq = jnp.where(blk < far_blocks, jnp.where(sel > 0.5, 0.0, MASKED), MASKED)
    ext = jnp.concatenate([maskq, jnp.zeros((tq, LANES - nslc), F32)], axis=1)
    for r in range(hpg):
        qa_ref[r * tq:(r + 1) * tq, :d] = q_ref[0, :, r * d:(r + 1) * d].astype(BF16)
        qa_ref[r * tq:(r + 1) * tq, d:] = (ext + bfar_ref[0, r:r + 1, :]).astype(BF16)
    m_ref[...] = jnp.full_like(m_ref, MASKED)
    acc_ref[...] = jnp.zeros_like(acc_ref)

    def far(c, carry):
        start = pl.multiple_of(c * SEL_CHUNK, SEL_CHUNK)
        st = _dot_nt(ka_ref[pl.ds(start, SEL_CHUNK), :], qa_ref[...])
        _softmax_step(st, vt_ref[c], m_ref, acc_ref, c1)
        return carry

    n_far = (jnp.maximum(i - 1, 0) * tq + SEL_CHUNK - 1) // SEL_CHUNK
    lax.fori_loop(0, n_far, far, 0)

    p0 = pl.multiple_of(jnp.maximum(i - 1, 0) * tq, tq)
    p1 = pl.multiple_of(i * tq, tq)
    kn = jnp.concatenate([ka_ref[pl.ds(p0, tq), :d], ka_ref[pl.ds(p1, tq), :d]], axis=0)
    vn = jnp.concatenate([v_ref[pl.ds(p0, tq), :], v_ref[pl.ds(p1, tq), :]], axis=0)
    vnt = jnp.concatenate([vn.T.astype(BF16), vext_ref[:, :2 * tq]], axis=0)
    er = lax.broadcasted_iota(I32, (2 * tq, nslc), 1)
    ec = lax.broadcasted_iota(I32, (2 * tq, nslc), 0) // SLC_BLOCK
    expand = jnp.where(er == far_blocks + ec, 1.0, 0.0).astype(BF16)
    picked = _dot_nt(expand, sel.astype(BF16))
    kj = lax.broadcasted_iota(I32, (2 * tq, tq), 0)
    qi = lax.broadcasted_iota(I32, (2 * tq, tq), 1)
    keep = jnp.where(kj <= qi + tq, picked, 0.0)
    keep = jnp.concatenate([keep] * hpg, axis=1) > 0.5
    st = _dot_nt(kn, qa_ref[:, :d]) + bnear_ref[0]
    _softmax_step(jnp.where(keep, st, MASKED), vnt, m_ref, acc_ref, c1)

    acc = acc_ref[...]
    out = acc[:d, :] / acc[d:d + 1, :]
    for r in range(hpg):
        o_ref[0, :, r * d:(r + 1) * d] = out[:, r * tq:(r + 1) * tq].T


def _nsa_selected(proj3, kcol, vcol, bnear, bfar, sel, b, t):
    tq = NSA_TQ
    d = HEAD_DIM
    gw = NSA_HPG * d
    nslc = t // SLC_BLOCK
    assert nslc <= FAR_BIAS_COL and t % SEL_CHUNK == 0
    rows = NSA_HPG * tq
    pos = np.arange(t)
    kext = np.zeros((t, LANES), np.float32)
    kext[pos, pos // SLC_BLOCK] = 1.0
    kext[:, FAR_BIAS_COL:FAR_BIAS_COL + 2] = 1.0
    vext = np.zeros((LANES, SEL_CHUNK), np.float32)
    vext[0, :] = 1.0
    kv = lambda col: pl.BlockSpec((None, t, d), lambda bi, g, i: (bi, 0, col + g))
    const = lambda shape: pl.BlockSpec(shape, lambda bi, g, i: (0,) * len(shape))
    return pl.pallas_call(
        functools.partial(_sel_kernel, tq=tq, c1=d ** -0.5 * LOG2E),
        grid=(b, NSA_KV_GROUPS, t // tq),
        in_specs=[pl.BlockSpec((1, tq, gw), lambda bi, g, i: (bi, i, g)),
                  kv(kcol), kv(vcol), const((t, LANES)), const((LANES, SEL_CHUNK)),
                  pl.BlockSpec((1, 2 * tq, rows), lambda bi, g, i: (g, 0, 0)),
                  pl.BlockSpec((1, NSA_HPG, LANES), lambda bi, g, i: (g, 0, 0)),
                  pl.BlockSpec((1, 1, tq, nslc), lambda bi, g, i: (bi, g, i, 0))],
        out_specs=pl.BlockSpec((1, tq, gw), lambda bi, g, i: (bi, i, g)),
        out_shape=jax.ShapeDtypeStruct((b, t, NSA_WIDTH), F32),
        scratch_shapes=[pltpu.VMEM((rows, 2 * d), BF16),
                        pltpu.VMEM((t, 2 * d), BF16),
                        pltpu.VMEM((t // SEL_CHUNK, 2 * d, SEL_CHUNK), BF16),
                        pltpu.VMEM((1, rows), F32),
                        pltpu.VMEM((2 * d, rows), F32)],
        compiler_params=_cparams(("parallel", "parallel", "arbitrary")),
        name="nsa_selected",
    )(proj3, proj3, proj3, jnp.asarray(kext, BF16), jnp.asarray(vext, BF16),
      jnp.swapaxes(bnear, 1, 2), bfar, sel)


def _win_kernel(q_ref, k_ref, v_ref, bpat_ref, o_ref, kb_ref, vb_ref, *, tq, window, c1):
    i = pl.program_id(2)
    hpg = NSA_HPG
    d = HEAD_DIM
    nt = window // tq + 1

    @pl.when(i == 0)
    def _():
        kb_ref[...] = k_ref[...].astype(BF16)
        vb_ref[...] = v_ref[...].astype(BF16)

    qs = jnp.concatenate([q_ref[0, :, r * d:(r + 1) * d] for r in range(hpg)], axis=0).astype(BF16)
    t0 = jnp.maximum(i - (nt - 1), 0)
    qi = lax.broadcasted_iota(I32, (tq, tq), 0)
    kj = lax.broadcasted_iota(I32, (tq, tq), 1)
    parts = []
    for kk in range(nt):
        dt = i - (t0 + kk)
        start = pl.multiple_of((t0 + kk) * tq, tq)
        z = _dot_nt(qs, kb_ref[pl.ds(start, tq), :]) + bpat_ref[0, jnp.clip(dt, 0, 2)]
        dist = dt * tq + qi - kj
        keep = jnp.where(dist >= 0, jnp.where(dist < window, 1.0, 0.0), 0.0) > 0.5
        parts.append(jnp.where(keep[None], z.reshape(hpg, tq, tq), MASKED))
    s = jnp.concatenate(parts, axis=-1)
    m = jnp.max(s, axis=-1, keepdims=True)
    p = jnp.exp2((s - m) * c1)
    den = jnp.sum(p, axis=-1, keepdims=True)
    vwin = vb_ref[pl.ds(pl.multiple_of(t0 * tq, tq), nt * tq), :]
    out = (_dot(p.reshape(hpg * tq, nt * tq).astype(BF16), vwin).reshape(hpg, tq, d) / den)
    for r in range(hpg):
        o_ref[0, :, r * d:(r + 1) * d] = out[r]


def _nsa_window(proj3, kcol, vcol, bpat, b, t):
    tq = NSA_TQ
    d = HEAD_DIM
    gw = NSA_HPG * d
    assert WINDOW % tq == 0 and t >= WINDOW + tq
    kv = lambda col: pl.BlockSpec((None, t, d), lambda bi, g, i: (bi, 0, col + g))
    return pl.pallas_call(
        functools.partial(_win_kernel, tq=tq, window=WINDOW, c1=d ** -0.5 * LOG2E),
        grid=(b, NSA_KV_GROUPS, t // tq),
        in_specs=[pl.BlockSpec((1, tq, gw), lambda bi, g, i: (bi, i, g)),
                  kv(kcol), kv(vcol),
                  pl.BlockSpec((1, 3, NSA_HPG * tq, tq), lambda bi, g, i: (g, 0, 0, 0))],
        out_specs=pl.BlockSpec((1, tq, gw), lambda bi, g, i: (bi, i, g)),
        out_shape=jax.ShapeDtypeStruct((b, t, NSA_WIDTH), F32),
        scratch_shapes=[pltpu.VMEM((t, d), BF16), pltpu.VMEM((t, d), BF16)],
        compiler_params=_cparams(("parallel", "parallel", "arbitrary")),
        name="nsa_window",
    )(proj3, proj3, proj3, bpat)


def _gate_mix_kernel(g_ref, oc_ref, os_ref, ow_ref, o_ref):
    gates = jax.nn.sigmoid(g_ref[...])
    for h in range(NSA_HEADS):
        hs = slice(h * HEAD_DIM, (h + 1) * HEAD_DIM)
        mix = (gates[:, 3 * h:3 * h + 1] * oc_ref[:, hs]
               + gates[:, 3 * h + 1:3 * h + 2] * os_ref[:, hs]
               + gates[:, 3 * h + 2:3 * h + 3] * ow_ref[:, hs])
        o_ref[:, hs] = mix.astype(o_ref.dtype)


def _gate_mix(gate_logits, o_c, o_s, o_w, *, tm=256):
    n = gate_logits.shape[0]
    row = pl.BlockSpec((tm, NSA_WIDTH), lambda i: (i, 0))
    return pl.pallas_call(
        _gate_mix_kernel,
        grid=(n // tm,),
        in_specs=[pl.BlockSpec((tm, 3 * NSA_HEADS), lambda i: (i, 0)), row, row, row],
        out_specs=row,
        out_shape=jax.ShapeDtypeStruct((n, NSA_WIDTH), BF16),
        compiler_params=_cparams(("parallel",)),
        name="nsa_gate_mix",
    )(gate_logits, o_c, o_s, o_w)


def _nsa_bias_tables(rel_bias, t):
    tq = NSA_TQ
    g, hpg = NSA_KV_GROUPS, NSA_HPG
    inv_scale = HEAD_DIM ** 0.5
    tab = rel_bias.astype(F32)
    last = REL_BUCKETS - 1
    far_from = int(np.nonzero(_rel_bucket_np(np.arange(4 * REL_MAX_DIST)) < last)[0].max()) + 1
    assert (_rel_bucket_np(np.arange(far_from, t + tq)) == last).all()

    def by_group(a):
        a = jnp.moveaxis(a, -1, 0)
        return a.reshape((g, hpg) + a.shape[1:])

    iq = np.arange(tq)[:, None]
    m = np.arange(CMP_NEAR)[None, :]
    dist_c = iq - CMP_STRIDE * (m - CMP_NEAR_LO) - (CMP_LEN - 1)
    assert dist_c[:, 0].min() >= far_from
    assert (iq - CMP_STRIDE * (CMP_NEAR - CMP_NEAR_LO) - (CMP_LEN - 1)).max() < 0
    pb = jnp.zeros((g, hpg, tq, LANES), F32)
    pb = pb.at[..., :CMP_NEAR].set(by_group(tab[_rel_bucket_np(dist_c)]))
    pb = pb.at[..., CMP_NEAR].set(by_group(tab[last])[..., None])
    pb = pb.reshape(g, hpg * tq, LANES)
    pb_hi = pb.astype(BF16)
    pb = jnp.concatenate([pb_hi, (pb - pb_hi.astype(F32)).astype(BF16)], axis=-1)
    assert 2 * tq - (tq - 1) >= far_from
    jk = np.arange(tq)[None, :]
    idx = np.stack([_rel_bucket_np(iq - jk), _rel_bucket_np(tq + iq - jk), np.full((tq, tq), last, np.int32)])
    pat = by_group(tab[idx]) * inv_scale
    bpat = jnp.transpose(pat, (0, 2, 1, 3, 4)).reshape(g, 3, hpg * tq, tq)
    bnear = jnp.concatenate([pat[:, :, 1], pat[:, :, 0]], axis=-1).reshape(g, hpg * tq, 2 * tq)
    far = by_group(tab[last]) * inv_scale
    far_hi = far.astype(BF16).astype(F32)
    bfar = jnp.zeros((g, hpg, LANES), F32)
    bfar = bfar.at[..., FAR_BIAS_COL].set(far_hi).at[..., FAR_BIAS_COL + 1].set(far - far_hi)
    return pb, bnear, bfar, bpat


def _router_kernel(h_ref, w_ref, b_ref, idx_ref, wt_ref, *, tm):
    logits = _dot(h_ref[...], w_ref[...], precision=HI) + b_ref[...]
    lane = lax.broadcasted_iota(I32, (tm, N_EXPERTS), 1).astype(F32)
    out_lane = lax.broadcasted_iota(I32, (tm, LANES), 1)
    work = logits
    idx_out = jnp.zeros((tm, LANES), F32)
    val_out = jnp.zeros((tm, LANES), F32)
    top = None
    den = jnp.zeros((tm, 1), F32)
    for k in range(TOP_K):
        mx = jnp.max(work, axis=-1, keepdims=True)
        idx = jnp.min(jnp.where(work == mx, lane, float(N_EXPERTS)), axis=-1, keepdims=True)
        if top is None:
            top = mx
        e = jnp.exp(mx - top)
        den = den + e
        idx_out = jnp.where(out_lane == k, idx, idx_out)
        val_out = jnp.where(out_lane == k, e, val_out)
        work = jnp.where(lane == idx, -jnp.inf, work)
    idx_ref[...] = idx_out.astype(I32)
    wt_ref[...] = val_out / den


def _router(h, w, bias, *, tm=512):
    n, d = h.shape
    out = pl.BlockSpec((tm, LANES), lambda i: (i, 0))
    return pl.pallas_call(
        functools.partial(_router_kernel, tm=tm),
        grid=(n // tm,),
        in_specs=[pl.BlockSpec((tm, d), lambda i: (i, 0)),
                  pl.BlockSpec((d, N_EXPERTS), lambda i: (0, 0)),
                  pl.BlockSpec((1, N_EXPERTS), lambda i: (0, 0))],
        out_specs=[out, out],
        out_shape=[jax.ShapeDtypeStruct((n, LANES), I32), jax.ShapeDtypeStruct((n, LANES), F32)],
        compiler_params=_cparams(("parallel",)),
        name="moe_router",
    )(h, w, bias.reshape(1, N_EXPERTS))


def _gu_prep_kernel(w_ref, p_ref, o_ref):
    o_ref[0] = _dot(w_ref[0].astype(BF16), p_ref[...]).astype(BF16)


def _gu_prep(w_gu, *, tk=1024):
    e, d, f2 = w_gu.shape
    perm = np.zeros((f2, f2), np.float32)
    perm[np.arange(f2), (np.arange(f2) % 2) * (f2 // 2) + np.arange(f2) // 2] = 1.0
    return pl.pallas_call(
        _gu_prep_kernel,
        grid=(e, d // tk),
        in_specs=[pl.BlockSpec((1, tk, f2), lambda ei, k: (ei, k, 0)),
                  pl.BlockSpec((f2, f2), lambda ei, k: (0, 0))],
        out_specs=pl.BlockSpec((1, tk, f2), lambda ei, k: (ei, k, 0)),
        out_shape=jax.ShapeDtypeStruct((e, d, f2), BF16),
        compiler_params=_cparams(("parallel", "parallel")),
        name="moe_gu_prep",
    )(w_gu, jnp.asarray(perm, BF16))


MOE_TM = 256


def _expert_kernel(te_ref, x_ref, wgu_ref, bg_ref, bl_ref, wd_ref, bd_ref, o_ref):
    ff = bg_ref.shape[-1]
    hgu = _dot(x_ref[...], wgu_ref[0])
    glu = jnp.minimum(hgu[:, :ff] + bg_ref[0], SWIGLU_LIMIT)
    lin = jnp.clip(hgu[:, ff:] + bl_ref[0], -SWIGLU_LIMIT, SWIGLU_LIMIT)
    act = glu * jax.nn.sigmoid(SWIGLU_ALPHA * glu) * (lin + 1.0)
    o_ref[...] = _dot(act.astype(BF16), wd_ref[0].astype(BF16)) + bd_ref[0]


def _experts(tile_expert, xs, w_gu, b_glu, b_lin, w_down, b_down):
    p, d = xs.shape
    tm = MOE_TM
    ff = w_down.shape[1]
    by_expert = lambda shape: pl.BlockSpec((1,) + shape, lambda i, te: (te[i], 0, 0))
    grid_spec = pltpu.PrefetchScalarGridSpec(
        num_scalar_prefetch=1,
        grid=(p // tm,),
        in_specs=[pl.BlockSpec((tm, d), lambda i, te: (i, 0)),
                  by_expert((d, 2 * ff)), by_expert((1, ff)), by_expert((1, ff)),
                  by_expert((ff, d)), by_expert((1, d))],
        out_specs=pl.BlockSpec((tm, d), lambda i, te: (i, 0)),
    )
    return pl.pallas_call(
        _expert_kernel,
        grid_spec=grid_spec,
        out_shape=jax.ShapeDtypeStruct((p, d), F32),
        compiler_params=_cparams(("arbitrary",)),
        name="moe_experts",
    )(tile_expert, xs, w_gu, b_glu, b_lin, w_down, b_down)


def _moe(h_f32, h_bf16, w_router, b_router, w_gu, b_glu, b_lin, w_down, b_down):
    n, d = h_f32.shape
    tm = MOE_TM
    idx128, wt128 = _router(h_f32, w_router, b_router)
    e_flat = idx128[:, :TOP_K].reshape(-1)
    na = n * TOP_K
    p = na + N_EXPERTS * tm
    order = jnp.argsort(e_flat, stable=True).astype(I32)
    sorted_e = e_flat[order]
    start = jnp.searchsorted(sorted_e, jnp.arange(N_EXPERTS + 1, dtype=I32), side="left").astype(I32)
    counts = start[1:] - start[:-1]
    padded = ((counts + tm - 1) // tm) * tm
    pend = jnp.cumsum(padded)
    pstart = pend - padded
    dest = (pstart - start[:-1])[sorted_e] + jnp.arange(na, dtype=I32)
    row_token = jnp.zeros((p,), I32).at[dest].set(order // TOP_K)
    slot = jnp.zeros((na,), I32).at[order].set(dest).reshape(n, TOP_K)
    tile_expert = jnp.minimum(
        jnp.searchsorted(pend, jnp.arange(p // tm, dtype=I32) * tm, side="right"), N_EXPERTS - 1).astype(I32)
    xs = jnp.take(h_bf16, row_token, axis=0)
    ys = _experts(tile_expert, xs, w_gu, b_glu, b_lin, w_down, b_down)
    return [jnp.take(ys, slot[:, k], axis=0) for k in range(TOP_K)], wt128


def kernel(x, ln1_g, ln1_b, ln2_g, ln2_b, ev_w_in, ev_w_out, hg_lb_raw, hg_norm_g, od_w_in, od_w_out,
           cmp_k_w1, cmp_k_w2, cmp_k_pos, cmp_v_w1, cmp_v_w2, cmp_v_pos, rel_bias, router_w, router_b,
           exp_w_gu, exp_b_gu, exp_w_down, exp_b_down):
    b, t, d = x.shape
    n = b * t
    depth = ln1_g.shape[0]
    alpha = (2 * depth) ** 0.25
    lb_soft = jax.nn.softmax(hg_lb_raw.astype(F32), axis=0)
    lower_bounds = jnp.cumsum(lb_soft, axis=0) - lb_soft[0]
    pb, bnear, bfar, bpat = _nsa_bias_tables(rel_bias, t)

    h = x.reshape(n, d)
    hb = h.astype(BF16)
    for layer in range(depth):
        if layer % 2 == 0:
            e = layer // 2
            proj = _matmul(hb, ev_w_in[e])
            o_a = _sb_attention(proj, b, t)
            o_b = _hgrn2(proj, lower_bounds[layer], hg_norm_g[e], b, t)
            mix = _matmul2(o_a, o_b, ev_w_out[e])
        else:
            o = layer // 2
            w_in = od_w_in[o]
            proj = _matmul(hb, w_in, n=NSA_WIDTH + 6 * NSA_KV_WIDTH)
            gate_logits = _matmul(hb, w_in[:, NSA_WIDTH + 6 * NSA_KV_WIDTH:])
            proj3 = proj.reshape(b, t, -1)
            c0 = NSA_WIDTH // HEAD_DIM
            k_cmp = _compress(proj3, c0, cmp_k_w1[o], cmp_k_w2[o], cmp_k_pos[o])
            v_cmp = _compress(proj3, c0 + NSA_KV_GROUPS, cmp_v_w1[o], cmp_v_w2[o], cmp_v_pos[o])
            o_c, sel_t = _cmp_select(proj3, k_cmp, v_cmp, pb, b, t)
            sel = jnp.swapaxes(sel_t, 2, 3)
            o_s = _nsa_selected(proj3, c0 + 2 * NSA_KV_GROUPS, c0 + 3 * NSA_KV_GROUPS, bnear, bfar, sel, b, t)
            o_w = _nsa_window(proj3, c0 + 4 * NSA_KV_GROUPS, c0 + 5 * NSA_KV_GROUPS, bpat, b, t)
            mixed = _gate_mix(gate_logits, o_c.reshape(n, -1), o_s.reshape(n, -1), o_w.reshape(n, -1))
            mix = _matmul(mixed, od_w_out[o])
        h, hb = _add_ln(h, mix, ln1_g[layer], ln1_b[layer], alpha)
        ys, gate_w = _moe(h, hb, router_w[layer], router_b[layer], _gu_prep(exp_w_gu[layer]),
                          exp_b_gu[layer][:, None, 0::2], exp_b_gu[layer][:, None, 1::2],
                          exp_w_down[layer], exp_b_down[layer][:, None, :])
        h, hb = _combine_ln(h, ys, gate_w, ln2_g[layer], ln2_b[layer], alpha)
    return h.reshape(b, t, d)
```

```python
import functools
import math

import jax
import jax.numpy as jnp
import numpy as np
from jax import lax
from jax.experimental import pallas as pl
from jax.experimental.pallas import tpu as pltpu

F32 = jnp.float32
BF16 = jnp.bfloat16
I32 = jnp.int32

HEAD_DIM = 128
SB_HEADS = 16
HG_HEADS = 16
SB_WIDTH = SB_HEADS * HEAD_DIM
HG_WIDTH = HG_HEADS * HEAD_DIM
F_MIN = 1e-6
NSA_HEADS = 32
NSA_KV_GROUPS = 4
NSA_HPG = NSA_HEADS // NSA_KV_GROUPS
NSA_WIDTH = NSA_HEADS * HEAD_DIM
NSA_KV_WIDTH = NSA_KV_GROUPS * HEAD_DIM
CMP_LEN = 32
CMP_STRIDE = 16
SLC_BLOCK = 64
N_SELECT = 16
WINDOW = 512
REL_BUCKETS = 32
REL_MAX_DIST = 128
N_EXPERTS = 32
TOP_K = 4
EXPERT_FF = 384
SWIGLU_LIMIT = 7.0
SWIGLU_ALPHA = 1.702
LN_EPS = 1e-5
NEG_INF = -1e30
FORCED = 1e9

LANES = 128
VMEM_LIMIT = 56 * 1024 * 1024

HI = lax.Precision.HIGHEST
LOG2E = 1.4426950408889634


def _cparams(sem):
    return pltpu.CompilerParams(dimension_semantics=sem, vmem_limit_bytes=VMEM_LIMIT)


def _dot_nt(a, b, **kw):
    return lax.dot_general(a, b, (((1,), (1,)), ((), ())), preferred_element_type=F32, **kw)


def _dot(a, b, **kw):
    return jnp.dot(a, b, preferred_element_type=F32, **kw)


def _mm_kernel(a_ref, w_ref, o_ref):
    o_ref[...] = _dot(a_ref[...], w_ref[...].astype(BF16)).astype(o_ref.dtype)


def _matmul(a, w, layer=None, *, n=None, tm=1024, tn=512, out_dtype=F32):
    m, k = a.shape
    n = w.shape[-1] if n is None else n
    tn = min(tn, n)
    assert m % tm == 0 and n % tn == 0
    if w.ndim == 3:
        w_spec = pl.BlockSpec((None, k, tn), lambda i, j: (layer, 0, j))
    else:
        w_spec = pl.BlockSpec((k, tn), lambda i, j: (0, j))
    return pl.pallas_call(
        _mm_kernel,
        grid=(m // tm, n // tn),
        in_specs=[pl.BlockSpec((tm, k), lambda i, j: (i, 0)), w_spec],
        out_specs=pl.BlockSpec((tm, tn), lambda i, j: (i, j)),
        out_shape=jax.ShapeDtypeStruct((m, n), out_dtype),
        compiler_params=_cparams(("parallel", "arbitrary")),
        name="matmul",
    )(a, w)


def _mm2_kernel(a1_ref, a2_ref, w1_ref, w2_ref, o_ref):
    o_ref[...] = (_dot(a1_ref[...], w1_ref[...].astype(BF16))
                  + _dot(a2_ref[...], w2_ref[...].astype(BF16)))


def _matmul2(a1, a2, w, layer, *, tm=1024, tn=512):
    m, k1 = a1.shape
    k2 = a2.shape[1]
    n = w.shape[-1]
    assert k1 == k2 and w.shape[1] == k1 + k2
    return pl.pallas_call(
        _mm2_kernel,
        grid=(m // tm, n // tn),
        in_specs=[pl.BlockSpec((tm, k1), lambda i, j: (i, 0)),
                  pl.BlockSpec((tm, k2), lambda i, j: (i, 0)),
                  pl.BlockSpec((None, k1, tn), lambda i, j: (layer, 0, j)),
                  pl.BlockSpec((None, k2, tn), lambda i, j: (layer, 1, j))],
        out_specs=pl.BlockSpec((tm, tn), lambda i, j: (i, j)),
        out_shape=jax.ShapeDtypeStruct((m, n), F32),
        compiler_params=_cparams(("parallel", "arbitrary")),
        name="matmul2",
    )(a1, a2, w, w)


def _ln_store(x, g_ref, b_ref, of_ref, ob_ref):
    mu = jnp.mean(x, axis=-1, keepdims=True)
    xc = x - mu
    var = jnp.mean(xc * xc, axis=-1, keepdims=True)
    out = xc * lax.rsqrt(var + LN_EPS) * g_ref[...] + b_ref[...]
    of_ref[...] = out
    ob_ref[...] = out.astype(BF16)


def _add_ln_kernel(h_ref, y_ref, g_ref, b_ref, of_ref, ob_ref, *, alpha):
    _ln_store(alpha * h_ref[...] + y_ref[...], g_ref, b_ref, of_ref, ob_ref)


def _combine_ln_kernel(h_ref, *refs, alpha):
    ys, (w_ref, g_ref, b_ref, of_ref, ob_ref) = refs[:TOP_K], refs[TOP_K:]
    w = w_ref[...]
    x = alpha * h_ref[...]
    for k in range(TOP_K):
        x = x + w[:, k:k + 1] * ys[k][...]
    _ln_store(x, g_ref, b_ref, of_ref, ob_ref)


def _combine_ln(h, ys, w128, g, b, alpha, *, tm=128):
    n, d = h.shape
    row = pl.BlockSpec((tm, d), lambda i: (i, 0))
    vec = pl.BlockSpec((1, d), lambda i: (0, 0))
    return pl.pallas_call(
        functools.partial(_combine_ln_kernel, alpha=alpha),
        grid=(n // tm,),
        in_specs=[row] * (1 + TOP_K) + [pl.BlockSpec((tm, LANES), lambda i: (i, 0)), vec, vec],
        out_specs=[row, row],
        out_shape=[jax.ShapeDtypeStruct((n, d), F32), jax.ShapeDtypeStruct((n, d), BF16)],
        compiler_params=_cparams(("parallel",)),
        name="moe_combine_ln",
    )(h, *ys, w128, g.reshape(1, d), b.reshape(1, d))


def _add_ln(h, y, g, b, alpha, *, tm=256):
    n, d = h.shape
    row = pl.BlockSpec((tm, d), lambda i: (i, 0))
    vec = pl.BlockSpec((1, d), lambda i: (0, 0))
    return pl.pallas_call(
        functools.partial(_add_ln_kernel, alpha=alpha),
        grid=(n // tm,),
        in_specs=[row, row, vec, vec],
        out_specs=[row, row],
        out_shape=[jax.ShapeDtypeStruct((n, d), F32), jax.ShapeDtypeStruct((n, d), BF16)],
        compiler_params=_cparams(("parallel",)),
        name="add_ln",
    )(h, y, g.reshape(1, d), b.reshape(1, d))


SB_TQ = 512
SB_KB = LANES
SB_DEAD = -104.0


def _sb_kernel(q_ref, k_ref, v_ref, u_ref, o_ref, kb_ref, vb_ref, acc_ref, carry_ref, *, tq, scale):
    i = pl.program_id(2)
    kb = SB_KB
    nb = tq // kb

    @pl.when(i == 0)
    def _():
        kb_ref[...] = k_ref[...].astype(BF16)
        vb_ref[...] = v_ref[...].astype(BF16)

    q = q_ref[...].astype(BF16)
    u = u_ref[...]
    acc_ref[...] = jnp.zeros_like(acc_ref)
    carry_ref[...] = jnp.zeros_like(carry_ref)

    def block_terms(z, causal=None):
        sp = jnp.log(1.0 + jnp.exp(-jnp.abs(z)))
        ls = jnp.minimum(z, 0.0) - sp
        lk = ls - z
        if causal is not None:
            lk = jnp.where(causal, lk, 0.0)
        hi = lk.astype(BF16)
        lo = (lk - hi.astype(F32)).astype(BF16)
        rt = _dot(jnp.concatenate([hi, lo], axis=1), u)
        return ls, rt[:, :kb] - lk, rt[:, kb:]

    base = pl.multiple_of(i * tq, tq)
    for j in reversed(range(nb)):
        r0 = j * kb
        rows = tq - r0
        kj = kb_ref[pl.ds(base + r0, kb), :]
        vj = vb_ref[pl.ds(base + r0, kb), :]
        z = _dot_nt(q[r0:], kj) * scale
        causal = lax.broadcasted_iota(I32, (rows, kb), 1) < lax.broadcasted_iota(I32, (rows, kb), 0)
        ls, between, tot = block_terms(z, causal)
        carry = carry_ref[r0:, :]
        w = jnp.where(causal, jnp.exp(ls + between + carry), 0.0)
        acc_ref[r0:, :] += _dot(w.astype(BF16), vj)
        carry_ref[r0:, :] = carry + tot

    def far(state):
        c, _ = state
        start = pl.multiple_of(c * tq, tq)
        z = _dot_nt(q, kb_ref[pl.ds(start, tq), :]) * scale
        terms = [block_terms(z[:, j * kb:(j + 1) * kb]) for j in range(nb)]
        carry = carry_ref[...]
        ws = [None] * nb
        for j in reversed(range(nb)):
            ls, between, tot = terms[j]
            ws[j] = jnp.exp(ls + between + carry).astype(BF16)
            carry = carry + tot
        acc_ref[...] += _dot(jnp.concatenate(ws, axis=1), vb_ref[pl.ds(start, tq), :])
        carry_ref[...] = carry
        return c - 1, jnp.max(carry)

    lax.while_loop(lambda st: (st[0] >= 0) & (st[1] > SB_DEAD), far, (i - 1, jnp.max(carry_ref[...])))
    o_ref[...] = acc_ref[...].astype(o_ref.dtype)


def _sb_attention(proj, b, t):
    tq, kb = min(SB_TQ, t), SB_KB
    nq = t // tq
    r = np.arange(2 * kb)[:, None] % kb
    c = np.arange(2 * kb)[None, :]
    u = jnp.asarray(np.where(c < kb, r >= c, True), BF16)
    kv_spec = lambda off: pl.BlockSpec((t, HEAD_DIM), lambda bi, h, i: (bi, off + h))
    return pl.pallas_call(
        functools.partial(_sb_kernel, tq=tq, scale=HEAD_DIM ** -0.5),
        grid=(b, SB_HEADS, nq),
        in_specs=[pl.BlockSpec((tq, HEAD_DIM), lambda bi, h, i: (bi * nq + i, h)),
                  kv_spec(SB_HEADS), kv_spec(2 * SB_HEADS),
                  pl.BlockSpec((2 * kb, 2 * kb), lambda bi, h, i: (0, 0))],
        out_specs=pl.BlockSpec((tq, HEAD_DIM), lambda bi, h, i: (bi * nq + i, h)),
        out_shape=jax.ShapeDtypeStruct((b * t, SB_WIDTH), BF16),
        scratch_shapes=[pltpu.VMEM((t, HEAD_DIM), BF16), pltpu.VMEM((t, HEAD_DIM), BF16),
                        pltpu.VMEM((tq, HEAD_DIM), F32), pltpu.VMEM((tq, kb), F32)],
        compiler_params=_cparams(("parallel", "parallel", "arbitrary")),
        name="sb_attention",
    )(proj, proj, proj, u)


HG_C = 64
HG_TB = 512
HG_NH = 8


def _hg_tables(c):
    levels = []
    m = c // 2
    while m >= 1:
        levels.append(m)
        m //= 2
    nl = len(levels)
    e = np.zeros((nl + 2, c, c), np.float32)
    mask = np.zeros((nl + 1, c, c), np.float32)
    p = np.arange(c)
    for li, m in enumerate(levels):
        blk = p // (2 * m)
        half = (p // m) % 2
        mid = blk * 2 * m + m - 1
        for r in range(c):
            if half[r] == 1:
                e[li, r, mid[r] + 1:r + 1] = 1.0
            else:
                e[li, r, r + 1:mid[r] + 1] = 1.0
        mask[li] = ((half[:, None] == 1) & (half[None, :] == 0) & (blk[:, None] == blk[None, :]))
    mask[nl] = np.eye(c)
    e[nl] = np.tril(np.ones((c, c)))
    e[nl + 1] = np.triu(np.ones((c, c)), 1)
    return e.reshape((nl + 2) * c, c), mask, nl


def _hg_kernel(qh_ref, fh_ref, ih_ref, gh_ref, lb_ref, ng_ref, e_ref, mask_ref, o_ref, st_ref,
               *, c, tb, nl, nh):
    @pl.when(pl.program_id(2) == 0)
    def _():
        st_ref[...] = jnp.zeros_like(st_ref)

    ng = ng_ref[...]
    emat = e_ref[...]

    def one_head(rows, hh):
        cols = slice(hh * HEAD_DIM, (hh + 1) * HEAD_DIM)
        lb = lb_ref[:, cols]
        qh = qh_ref[rows, cols]
        f = lb + (1.0 - lb) * jax.nn.sigmoid(fh_ref[rows, cols])
        g = jnp.log(jnp.maximum(f, F_MIN))
        kk = 1.0 - f
        q = qh * jax.nn.sigmoid(qh)
        v = ih_ref[rows, cols]
        vb = v.astype(BF16)
        g1 = g.astype(BF16)
        r1 = g - g1.astype(F32)
        g2 = r1.astype(BF16)
        g3 = (r1 - g2.astype(F32)).astype(BF16)
        ex = jnp.exp(_dot(emat, jnp.concatenate([g1, g2, g3], axis=0)))
        scores = mask_ref[nl] * _dot_nt(q.astype(BF16), kk.astype(BF16))
        for li in range(nl):
            a = ex[li * c:(li + 1) * c]
            scores = scores + mask_ref[li] * _dot_nt((q * a).astype(BF16), (kk * a).astype(BF16))
        ecum = ex[nl * c:(nl + 1) * c]
        erest = ex[(nl + 1) * c:(nl + 2) * c]
        st = st_ref[hh]
        o = _dot(scores.astype(BF16), vb) + _dot_nt((q * ecum).astype(BF16), st.astype(BF16))
        st_ref[hh] = ecum[c - 1:c, :] * st + _dot(v.T.astype(BF16), (kk * erest).astype(BF16))
        gh = gh_ref[rows, cols]
        o = o * lax.rsqrt(jnp.mean(o * o, axis=-1, keepdims=True) + LN_EPS) * ng
        o_ref[rows, cols] = (o * (gh * jax.nn.sigmoid(gh))).astype(o_ref.dtype)

    def chunk(ci, carry):
        rows = pl.ds(pl.multiple_of(ci * c, c), c)
        for hh in range(nh):
            one_head(rows, hh)
        return carry

    lax.fori_loop(0, tb // c, chunk, 0)


def _hgrn2(proj, lb, norm_g, b, t):
    c, tb, nh = HG_C, min(HG_TB, t), HG_NH
    e, mask, nl = _hg_tables(c)
    e3 = np.concatenate([e, e, e], axis=1)
    nt = t // tb
    base = 3 * SB_HEADS
    assert base % nh == 0 and HG_HEADS % nh == 0
    wide = nh * HEAD_DIM
    col = lambda k: pl.BlockSpec((tb, wide), lambda bi, h, i: (bi * nt + i, (base + k * HG_HEADS) // nh + h))
    const2 = lambda a: pl.BlockSpec(a.shape, lambda bi, h, i: (0,) * a.ndim)
    return pl.pallas_call(
        functools.partial(_hg_kernel, c=c, tb=tb, nl=nl, nh=nh),
        grid=(b, HG_HEADS // nh, nt),
        in_specs=[col(0), col(1), col(2), col(3),
                  pl.BlockSpec((1, wide), lambda bi, h, i: (0, h)),
                  pl.BlockSpec((1, HEAD_DIM), lambda bi, h, i: (0, 0)),
                  const2(e3), const2(mask)],
        out_specs=pl.BlockSpec((tb, wide), lambda bi, h, i: (bi * nt + i, h)),
        out_shape=jax.ShapeDtypeStruct((b * t, HG_WIDTH), BF16),
        scratch_shapes=[pltpu.VMEM((nh, HEAD_DIM, HEAD_DIM), F32)],
        compiler_params=_cparams(("parallel", "parallel", "arbitrary")),
        name="hgrn2",
    )(proj, proj, proj, proj, lb.reshape(1, HG_WIDTH), norm_g.reshape(1, HEAD_DIM),
      jnp.asarray(e3, BF16), jnp.asarray(mask))


def _cmp_kernel(a_ref, w1_ref, w2_ref, pos_ref, o_ref, *, ncp):
    half = CMP_LEN // 2
    p = jnp.zeros((ncp, HEAD_DIM), F32)
    q = jnp.zeros((ncp, HEAD_DIM), F32)
    for j in range(half):
        s = a_ref[pl.ds(j, ncp, stride=CMP_STRIDE), :]
        p = p + _dot((s + pos_ref[j:j + 1, :]).astype(BF16), w1_ref[j].astype(BF16))
        q = q + _dot((s + pos_ref[half + j:half + j + 1, :]).astype(BF16), w1_ref[half + j].astype(BF16))
    hid = p + pltpu.roll(q, ncp - 1, 0)
    hid = hid * jax.nn.sigmoid(hid)
    out = _dot(hid.astype(BF16), w2_ref[...].astype(BF16))
    row = lax.broadcasted_iota(I32, (ncp, HEAD_DIM), 0)
    o_ref[0, 0] = jnp.where(row < ncp - 1, out, 0.0)


def _compress(proj3, col0, w1, w2, pos):
    b, t, _ = proj3.shape
    ncp = t // CMP_STRIDE
    assert CMP_LEN == 2 * CMP_STRIDE
    return pl.pallas_call(
        functools.partial(_cmp_kernel, ncp=ncp),
        grid=(b, NSA_KV_GROUPS),
        in_specs=[pl.BlockSpec((None, t, HEAD_DIM), lambda bi, g: (bi, 0, col0 + g)),
                  pl.BlockSpec((CMP_LEN, HEAD_DIM, HEAD_DIM), lambda bi, g: (0, 0, 0)),
                  pl.BlockSpec((HEAD_DIM, HEAD_DIM), lambda bi, g: (0, 0)),
                  pl.BlockSpec((CMP_LEN, HEAD_DIM), lambda bi, g: (0, 0))],
        out_specs=pl.BlockSpec((1, 1, ncp, HEAD_DIM), lambda bi, g: (bi, g, 0, 0)),
        out_shape=jax.ShapeDtypeStruct((b, NSA_KV_GROUPS, ncp, HEAD_DIM), F32),
        compiler_params=_cparams(("parallel", "parallel")),
        name="nsa_compress",
    )(proj3, w1.reshape(CMP_LEN, HEAD_DIM, HEAD_DIM), w2, pos)


NSA_TQ = 128
CMP_NEAR = 16
CMP_NEAR_LO = 9


def _rel_bucket_np(dist):
    dist = np.maximum(dist, 0)
    max_exact = REL_BUCKETS // 2
    ratio = (np.log(np.maximum(dist, max_exact).astype(np.float32) / np.float32(max_exact))
             / np.float32(math.log(REL_MAX_DIST / max_exact)))
    large = np.minimum(max_exact + (ratio * np.float32(REL_BUCKETS - max_exact)).astype(np.int32),
                       REL_BUCKETS - 1)
    return np.where(dist < max_exact, dist, large).astype(np.int32)


def _cmpsel_kernel(q_ref, kc_ref, vc_ref, pb_ref, ovl_ref, oc_ref, sel_ref,
                   *, tq, ncp, nslc, nsel, scale):
    i = pl.program_id(2)
    kc = kc_ref[0, 0].astype(BF16)
    vc = vc_ref[0, 0].astype(BF16)
    qpos = i * tq + lax.broadcasted_iota(I32, (tq, ncp), 0)
    ncol = lax.broadcasted_iota(I32, (tq, ncp), 1)
    valid = (ncol * CMP_STRIDE + (CMP_LEN - 1) <= qpos) & (ncol < ncp - 1)
    sr = lax.broadcasted_iota(I32, (LANES, ncp), 0)
    sc = lax.broadcasted_iota(I32, (LANES, ncp), 1)
    first = i * (tq // CMP_STRIDE) - CMP_NEAR_LO
    place = jnp.where(sr < CMP_NEAR, jnp.where(sc == first + sr, 1.0, 0.0),
                      jnp.where(sr == CMP_NEAR, jnp.where(sc < first, 1.0, 0.0), 0.0))
    place = place.astype(BF16)
    bias = _dot(pb_ref[0], jnp.concatenate([place, place], axis=0))
    psum = jnp.zeros((tq, ncp), F32)
    for r in range(NSA_HPG):
        hs = slice(r * HEAD_DIM, (r + 1) * HEAD_DIM)
        s = _dot_nt(q_ref[0, :, hs].astype(BF16), kc) * scale + bias[r * tq:(r + 1) * tq]
        s = jnp.where(valid, s, NEG_INF)
        mx = jnp.max(s, axis=-1, keepdims=True)
        e = jnp.where(valid, jnp.exp(s - mx), 0.0)
        den = jnp.sum(e, axis=-1, keepdims=True)
        p = e / jnp.maximum(den, 1e-30)
        oc_ref[0, :, hs] = _dot(p.astype(BF16), vc)
        psum = psum + p
    imp = _dot_nt(ovl_ref[...], psum, precision=HI)
    jrow = lax.broadcasted_iota(I32, (nslc, tq), 0)
    qblk = (i * tq + lax.broadcasted_iota(I32, (nslc, tq), 1)) // SLC_BLOCK
    ok = jrow <= qblk
    forced = (jrow == 0) | (jrow == qblk) | (jrow == qblk - 1)
    imp = jnp.where(ok, jnp.where(forced, FORCED, imp), NEG_INF)
    rank = jnp.zeros((nslc, tq), F32)
    for j2 in range(nslc):
        row = imp[j2:j2 + 1, :]
        tie = jnp.where(jrow > j2, 1.0, 0.0)
        rank = rank + jnp.where(row > imp, 1.0, jnp.where(row == imp, tie, 0.0))
    sel_ref[0, 0] = jnp.where((rank < nsel) & ok, 1.0, 0.0)


def _cmp_select(proj3, k_cmp, v_cmp, pb, b, t):
    tq = NSA_TQ
    ncp = t // CMP_STRIDE
    nslc = t // SLC_BLOCK
    nsel = min(N_SELECT, nslc)
    n_idx = np.arange(ncp)
    slc_start = np.arange(nslc) * SLC_BLOCK
    cs = n_idx * CMP_STRIDE
    ovl = ((cs[None, :] < slc_start[:, None] + SLC_BLOCK)
           & (cs[None, :] + CMP_LEN - 1 >= slc_start[:, None])
           & (n_idx[None, :] < ncp - 1)).astype(np.float32)
    gw = NSA_HPG * HEAD_DIM
    return pl.pallas_call(
        functools.partial(_cmpsel_kernel, tq=tq, ncp=ncp, nslc=nslc, nsel=nsel, scale=HEAD_DIM ** -0.5),
        grid=(b, NSA_KV_GROUPS, t // tq),
        in_specs=[pl.BlockSpec((1, tq, gw), lambda bi, g, i: (bi, i, g)),
                  pl.BlockSpec((1, 1, ncp, HEAD_DIM), lambda bi, g, i: (bi, g, 0, 0)),
                  pl.BlockSpec((1, 1, ncp, HEAD_DIM), lambda bi, g, i: (bi, g, 0, 0)),
                  pl.BlockSpec((1, NSA_HPG * tq, 2 * LANES), lambda bi, g, i: (g, 0, 0)),
                  pl.BlockSpec((nslc, ncp), lambda bi, g, i: (0, 0))],
        out_specs=[pl.BlockSpec((1, tq, gw), lambda bi, g, i: (bi, i, g)),
                   pl.BlockSpec((1, 1, nslc, tq), lambda bi, g, i: (bi, g, 0, i))],
        out_shape=[jax.ShapeDtypeStruct((b, t, NSA_WIDTH), F32),
                   jax.ShapeDtypeStruct((b, NSA_KV_GROUPS, nslc, t), F32)],
        compiler_params=_cparams(("parallel", "parallel", "parallel")),
        name="nsa_cmp_select",
    )(proj3, k_cmp, v_cmp, pb, jnp.asarray(ovl))


SEL_CHUNK = 512
MASKED = -1e30
FAR_BIAS_COL = SLC_BLOCK


def _softmax_step(st, vt, m_ref, acc_ref, c1):
    m_old = m_ref[...]
    m_new = jnp.maximum(m_old, jnp.max(st, axis=0, keepdims=True))
    p = jnp.exp2((st - m_new) * c1)
    acc_ref[...] = jnp.exp2((m_old - m_new) * c1) * acc_ref[...] + _dot(vt, p.astype(BF16))
    m_ref[...] = m_new


def _sel_kernel(q_ref, k_ref, v_ref, kext_ref, vext_ref, bnear_ref, bfar_ref, sel_ref, o_ref,
                qa_ref, ka_ref, vt_ref, m_ref, acc_ref, *, tq, c1):
    i = pl.program_id(2)
    hpg = NSA_HPG
    d = HEAD_DIM

    @pl.when(i == 0)
    def _():
        ka_ref[:, :d] = k_ref[...].astype(BF16)
        ka_ref[:, d:] = kext_ref[...]
        for c in range(vt_ref.shape[0]):
            vt_ref[c, :d, :] = v_ref[c * SEL_CHUNK:(c + 1) * SEL_CHUNK, :].T.astype(BF16)
            vt_ref[c, d:, :] = vext_ref[...]

    sel = sel_ref[0, 0]
    nslc = sel.shape[1]
    blk = lax.broadcasted_iota(I32, (tq, nslc), 1)
    far_blocks = (i - 1) * (tq // SLC_BLOCK)
    maskq = jnp.where(blk < far_blocks, jnp.where(sel > 0.5, 0.0, MASKED), MASKED)
    ext = jnp.concatenate([maskq, jnp.zeros((tq, LANES - nslc), F32)], axis=1)
    for r in range(hpg):
        qa_ref[r * tq:(r + 1) * tq, :d] = q_ref[0, :, r * d:(r + 1) * d].astype(BF16)
        qa_ref[r * tq:(r + 1) * tq, d:] = (ext + bfar_ref[0, r:r + 1, :]).astype(BF16)
    m_ref[...] = jnp.full_like(m_ref, MASKED)
    acc_ref[...] = jnp.zeros_like(acc_ref)

    def far(c, carry):
        start = pl.multiple_of(c * SEL_CHUNK, SEL_CHUNK)
        st = _dot_nt(ka_ref[pl.ds(start, SEL_CHUNK), :], qa_ref[...])
        _softmax_step(st, vt_ref[c], m_ref, acc_ref, c1)
        return carry

    n_far = (jnp.maximum(i - 1, 0) * tq + SEL_CHUNK - 1) // SEL_CHUNK
    lax.fori_loop(0, n_far, far, 0)

    p0 = pl.multiple_of(jnp.maximum(i - 1, 0) * tq, tq)
    p1 = pl.multiple_of(i * tq, tq)
    kn = jnp.concatenate([ka_ref[pl.ds(p0, tq), :d], ka_ref[pl.ds(p1, tq), :d]], axis=0)
    vn = jnp.concatenate([v_ref[pl.ds(p0, tq), :], v_ref[pl.ds(p1, tq), :]], axis=0)
    vnt = jnp.concatenate([vn.T.astype(BF16), vext_ref[:, :2 * tq]], axis=0)
    er = lax.broadcasted_iota(I32, (2 * tq, nslc), 1)
    ec = lax.broadcasted_iota(I32, (2 * tq, nslc), 0) // SLC_BLOCK
    expand = jnp.where(er == far_blocks + ec, 1.0, 0.0).astype(BF16)
    picked = _dot_nt(expand, sel.astype(BF16))
    kj = lax.broadcasted_iota(I32, (2 * tq, tq), 0)
    qi = lax.broadcasted_iota(I32, (2 * tq, tq), 1)
    keep = jnp.where(kj <= qi + tq, picked, 0.0)
    keep = jnp.concatenate([keep] * hpg, axis=1) > 0.5
    st = _dot_nt(kn, qa_ref[:, :d]) + bnear_ref[0]
    _softmax_step(jnp.where(keep, st, MASKED), vnt, m_ref, acc_ref, c1)

    acc = acc_ref[...]
    out = acc[:d, :] / acc[d:d + 1, :]
    for r in range(hpg):
        o_ref[0, :, r * d:(r + 1) * d] = out[:, r * tq:(r + 1) * tq].T


def _nsa_selected(proj3, kcol, vcol, bnear, bfar, sel, b, t):
    tq = NSA_TQ
    d = HEAD_DIM
    gw = NSA_HPG * d
    nslc = t // SLC_BLOCK
    assert nslc <= FAR_BIAS_COL and t % SEL_CHUNK == 0
    rows = NSA_HPG * tq
    pos = np.arange(t)
    kext = np.zeros((t, LANES), np.float32)
    kext[pos, pos // SLC_BLOCK] = 1.0
    kext[:, FAR_BIAS_COL:FAR_BIAS_COL + 2] = 1.0
    vext = np.zeros((LANES, SEL_CHUNK), np.float32)
    vext[0, :] = 1.0
    kv = lambda col: pl.BlockSpec((None, t, d), lambda bi, g, i: (bi, 0, col + g))
    const = lambda shape: pl.BlockSpec(shape, lambda bi, g, i: (0,) * len(shape))
    return pl.pallas_call(
        functools.partial(_sel_kernel, tq=tq, c1=d ** -0.5 * LOG2E),
        grid=(b, NSA_KV_GROUPS, t // tq),
        in_specs=[pl.BlockSpec((1, tq, gw), lambda bi, g, i: (bi, i, g)),
                  kv(kcol), kv(vcol), const((t, LANES)), const((LANES, SEL_CHUNK)),
                  pl.BlockSpec((1, 2 * tq, rows), lambda bi, g, i: (g, 0, 0)),
                  pl.BlockSpec((1, NSA_HPG, LANES), lambda bi, g, i: (g, 0, 0)),
                  pl.BlockSpec((1, 1, tq, nslc), lambda bi, g, i: (bi, g, i, 0))],
        out_specs=pl.BlockSpec((1, tq, gw), lambda bi, g, i: (bi, i, g)),
        out_shape=jax.ShapeDtypeStruct((b, t, NSA_WIDTH), F32),
        scratch_shapes=[pltpu.VMEM((rows, 2 * d), BF16),
                        pltpu.VMEM((t, 2 * d), BF16),
                        pltpu.VMEM((t // SEL_CHUNK, 2 * d, SEL_CHUNK), BF16),
                        pltpu.VMEM((1, rows), F32),
                        pltpu.VMEM((2 * d, rows), F32)],
        compiler_params=_cparams(("parallel", "parallel", "arbitrary")),
        name="nsa_selected",
    )(proj3, proj3, proj3, jnp.asarray(kext, BF16), jnp.asarray(vext, BF16),
      jnp.swapaxes(bnear, 1, 2), bfar, sel)


def _win_kernel(q_ref, k_ref, v_ref, bpat_ref, o_ref, kb_ref, vb_ref, *, tq, window, c1):
    i = pl.program_id(2)
    hpg = NSA_HPG
    d = HEAD_DIM
    nt = window // tq + 1

    @pl.when(i == 0)
    def _():
        kb_ref[...] = k_ref[...].astype(BF16)
        vb_ref[...] = v_ref[...].astype(BF16)

    qs = jnp.concatenate([q_ref[0, :, r * d:(r + 1) * d] for r in range(hpg)], axis=0).astype(BF16)
    t0 = jnp.maximum(i - (nt - 1), 0)
    qi = lax.broadcasted_iota(I32, (tq, tq), 0)
    kj = lax.broadcasted_iota(I32, (tq, tq), 1)
    parts = []
    for kk in range(nt):
        dt = i - (t0 + kk)
        start = pl.multiple_of((t0 + kk) * tq, tq)
        z = _dot_nt(qs, kb_ref[pl.ds(start, tq), :]) + bpat_ref[0, jnp.clip(dt, 0, 2)]
        dist = dt * tq + qi - kj
        keep = jnp.where(dist >= 0, jnp.where(dist < window, 1.0, 0.0), 0.0) > 0.5
        parts.append(jnp.where(keep[None], z.reshape(hpg, tq, tq), MASKED))
    s = jnp.concatenate(parts, axis=-1)
    m = jnp.max(s, axis=-1, keepdims=True)
    p = jnp.exp2((s - m) * c1)
    den = jnp.sum(p, axis=-1, keepdims=True)
    vwin = vb_ref[pl.ds(pl.multiple_of(t0 * tq, tq), nt * tq), :]
    out = (_dot(p.reshape(hpg * tq, nt * tq).astype(BF16), vwin).reshape(hpg, tq, d) / den)
    for r in range(hpg):
        o_ref[0, :, r * d:(r + 1) * d] = out[r]


def _nsa_window(proj3, kcol, vcol, bpat, b, t):
    tq = NSA_TQ
    d = HEAD_DIM
    gw = NSA_HPG * d
    assert WINDOW % tq == 0 and t >= WINDOW + tq
    kv = lambda col: pl.BlockSpec((None, t, d), lambda bi, g, i: (bi, 0, col + g))
    return pl.pallas_call(
        functools.partial(_win_kernel, tq=tq, window=WINDOW, c1=d ** -0.5 * LOG2E),
        grid=(b, NSA_KV_GROUPS, t // tq),
        in_specs=[pl.BlockSpec((1, tq, gw), lambda bi, g, i: (bi, i, g)),
                  kv(kcol), kv(vcol),
                  pl.BlockSpec((1, 3, NSA_HPG * tq, tq), lambda bi, g, i: (g, 0, 0, 0))],
        out_specs=pl.BlockSpec((1, tq, gw), lambda bi, g, i: (bi, i, g)),
        out_shape=jax.ShapeDtypeStruct((b, t, NSA_WIDTH), F32),
        scratch_shapes=[pltpu.VMEM((t, d), BF16), pltpu.VMEM((t, d), BF16)],
        compiler_params=_cparams(("parallel", "parallel", "arbitrary")),
        name="nsa_window",
    )(proj3, proj3, proj3, bpat)


def _gate_mix_kernel(g_ref, oc_ref, os_ref, ow_ref, o_ref):
    gates = jax.nn.sigmoid(g_ref[...])
    for h in range(NSA_HEADS):
        hs = slice(h * HEAD_DIM, (h + 1) * HEAD_DIM)
        mix = (gates[:, 3 * h:3 * h + 1] * oc_ref[:, hs]
               + gates[:, 3 * h + 1:3 * h + 2] * os_ref[:, hs]
               + gates[:, 3 * h + 2:3 * h + 3] * ow_ref[:, hs])
        o_ref[:, hs] = mix.astype(o_ref.dtype)


def _gate_mix(gate_logits, o_c, o_s, o_w, *, tm=256):
    n = gate_logits.shape[0]
    row = pl.BlockSpec((tm, NSA_WIDTH), lambda i: (i, 0))
    return pl.pallas_call(
        _gate_mix_kernel,
        grid=(n // tm,),
        in_specs=[pl.BlockSpec((tm, 3 * NSA_HEADS), lambda i: (i, 0)), row, row, row],
        out_specs=row,
        out_shape=jax.ShapeDtypeStruct((n, NSA_WIDTH), BF16),
        compiler_params=_cparams(("parallel",)),
        name="nsa_gate_mix",
    )(gate_logits, o_c, o_s, o_w)


def _nsa_bias_tables(rel_bias, t):
    tq = NSA_TQ
    g, hpg = NSA_KV_GROUPS, NSA_HPG
    inv_scale = HEAD_DIM ** 0.5
    tab = rel_bias.astype(F32)
    last = REL_BUCKETS - 1
    far_from = int(np.nonzero(_rel_bucket_np(np.arange(4 * REL_MAX_DIST)) < last)[0].max()) + 1
    assert (_rel_bucket_np(np.arange(far_from, t + tq)) == last).all()

    def by_group(a):
        a = jnp.moveaxis(a, -1, 0)
        return a.reshape((g, hpg) + a.shape[1:])

    iq = np.arange(tq)[:, None]
    m = np.arange(CMP_NEAR)[None, :]
    dist_c = iq - CMP_STRIDE * (m - CMP_NEAR_LO) - (CMP_LEN - 1)
    assert dist_c[:, 0].min() >= far_from
    assert (iq - CMP_STRIDE * (CMP_NEAR - CMP_NEAR_LO) - (CMP_LEN - 1)).max() < 0
    pb = jnp.zeros((g, hpg, tq, LANES), F32)
    pb = pb.at[..., :CMP_NEAR].set(by_group(tab[_rel_bucket_np(dist_c)]))
    pb = pb.at[..., CMP_NEAR].set(by_group(tab[last])[..., None])
    pb = pb.reshape(g, hpg * tq, LANES)
    pb_hi = pb.astype(BF16)
    pb = jnp.concatenate([pb_hi, (pb - pb_hi.astype(F32)).astype(BF16)], axis=-1)
    assert 2 * tq - (tq - 1) >= far_from
    jk = np.arange(tq)[None, :]
    idx = np.stack([_rel_bucket_np(iq - jk), _rel_bucket_np(tq + iq - jk), np.full((tq, tq), last, np.int32)])
    pat = by_group(tab[idx]) * inv_scale
    bpat = jnp.transpose(pat, (0, 2, 1, 3, 4)).reshape(g, 3, hpg * tq, tq)
    bnear = jnp.concatenate([pat[:, :, 1], pat[:, :, 0]], axis=-1).reshape(g, hpg * tq, 2 * tq)
    far = by_group(tab[last]) * inv_scale
    far_hi = far.astype(BF16).astype(F32)
    bfar = jnp.zeros((g, hpg, LANES), F32)
    bfar = bfar.at[..., FAR_BIAS_COL].set(far_hi).at[..., FAR_BIAS_COL + 1].set(far - far_hi)
    return pb, bnear, bfar, bpat


def _router_kernel(h_ref, w_ref, b_ref, tri_ref, idx_ref, wt_ref, rank_ref, cnt_ref, *, tm):
    @pl.when(pl.program_id(0) == 0)
    def _():
        cnt_ref[...] = jnp.zeros_like(cnt_ref)

    logits = _dot(h_ref[...], w_ref[...], precision=HI) + b_ref[...]
    lane = lax.broadcasted_iota(I32, (tm, N_EXPERTS), 1).astype(F32)
    out_lane = lax.broadcasted_iota(I32, (tm, LANES), 1)
    work = logits
    idx_out = jnp.zeros((tm, LANES), F32)
    val_out = jnp.zeros((tm, LANES), F32)
    top = None
    den = jnp.zeros((tm, 1), F32)
    hots = []
    for k in range(TOP_K):
        mx = jnp.max(work, axis=-1, keepdims=True)
        idx = jnp.min(jnp.where(work == mx, lane, float(N_EXPERTS)), axis=-1, keepdims=True)
        if top is None:
            top = mx
        e = jnp.exp(mx - top)
        den = den + e
        idx_out = jnp.where(out_lane == k, idx, idx_out)
        val_out = jnp.where(out_lane == k, e, val_out)
        hots.append(jnp.where(lane == idx, 1.0, 0.0))
        work = jnp.where(lane == idx, -jnp.inf, work)
    idx_ref[...] = idx_out.astype(I32)
    wt_ref[...] = val_out / den
    hot = hots[0] + hots[1] + hots[2] + hots[3]
    before = cnt_ref[...] + _dot(tri_ref[...], hot.astype(BF16))
    rank_out = jnp.zeros((tm, LANES), F32)
    for k in range(TOP_K):
        rank_out = jnp.where(out_lane == k, jnp.sum(hots[k] * before, axis=-1, keepdims=True), rank_out)
    rank_ref[...] = rank_out.astype(I32)
    cnt_ref[...] += jnp.sum(hot, axis=0, keepdims=True)


def _router(h, w, bias, *, tm=512):
    n, d = h.shape
    out = pl.BlockSpec((tm, LANES), lambda i: (i, 0))
    tri = jnp.asarray(np.tril(np.ones((tm, tm), np.float32), -1), BF16)
    return pl.pallas_call(
        functools.partial(_router_kernel, tm=tm),
        grid=(n // tm,),
        in_specs=[pl.BlockSpec((tm, d), lambda i: (i, 0)),
                  pl.BlockSpec((d, N_EXPERTS), lambda i: (0, 0)),
                  pl.BlockSpec((1, N_EXPERTS), lambda i: (0, 0)),
                  pl.BlockSpec((tm, tm), lambda i: (0, 0))],
        out_specs=[out, out, out, pl.BlockSpec((1, N_EXPERTS), lambda i: (0, 0))],
        out_shape=[jax.ShapeDtypeStruct((n, LANES), I32), jax.ShapeDtypeStruct((n, LANES), F32),
                   jax.ShapeDtypeStruct((n, LANES), I32), jax.ShapeDtypeStruct((1, N_EXPERTS), F32)],
        compiler_params=_cparams(("arbitrary",)),
        name="moe_router",
    )(h, w, bias.reshape(1, N_EXPERTS), tri)


GU_BLK = 2 * LANES


def _gu_prep_kernel(w_ref, p_ref, o_ref):
    for c in range(w_ref.shape[-1] // GU_BLK):
        cols = slice(c * GU_BLK, (c + 1) * GU_BLK)
        o_ref[0, :, cols] = _dot(w_ref[0, :, cols].astype(BF16), p_ref[...]).astype(BF16)


def _gu_prep(w_gu, *, tk=1024):
    e, d, f2 = w_gu.shape
    assert f2 % GU_BLK == 0
    k = np.arange(GU_BLK)
    perm = np.zeros((GU_BLK, GU_BLK), np.float32)
    perm[k, (k % 2) * LANES + k // 2] = 1.0
    return pl.pallas_call(
        _gu_prep_kernel,
        grid=(e, d // tk),
        in_specs=[pl.BlockSpec((1, tk, f2), lambda ei, ki: (ei, ki, 0)),
                  pl.BlockSpec((GU_BLK, GU_BLK), lambda ei, ki: (0, 0))],
        out_specs=pl.BlockSpec((1, tk, f2), lambda ei, ki: (ei, ki, 0)),
        out_shape=jax.ShapeDtypeStruct((e, d, f2), BF16),
        compiler_params=_cparams(("parallel", "parallel")),
        name="moe_gu_prep",
    )(w_gu, jnp.asarray(perm, BF16))


MOE_TM = 512


def _expert_kernel(te_ref, x_ref, wgu_ref, bg_ref, bl_ref, wd_ref, bd_ref, o_ref):
    hgu = _dot(x_ref[...], wgu_ref[0])
    nblk = hgu.shape[1] // GU_BLK
    glu = jnp.concatenate([hgu[:, c * GU_BLK:c * GU_BLK + LANES] for c in range(nblk)], axis=1)
    lin = jnp.concatenate([hgu[:, c * GU_BLK + LANES:(c + 1) * GU_BLK] for c in range(nblk)], axis=1)
    glu = jnp.minimum(glu + bg_ref[0], SWIGLU_LIMIT)
    lin = jnp.clip(lin + bl_ref[0], -SWIGLU_LIMIT, SWIGLU_LIMIT)
    act = glu * jax.nn.sigmoid(SWIGLU_ALPHA * glu) * (lin + 1.0)
    o_ref[...] = _dot(act.astype(BF16), wd_ref[0]) + bd_ref[0]


def _experts(tile_expert, xs, w_gu, b_glu, b_lin, w_down, b_down):
    p, d = xs.shape
    tm = MOE_TM
    ff = w_down.shape[1]
    by_expert = lambda shape: pl.BlockSpec((1,) + shape, lambda i, te: (te[i], 0, 0))
    grid_spec = pltpu.PrefetchScalarGridSpec(
        num_scalar_prefetch=1,
        grid=(p // tm,),
        in_specs=[pl.BlockSpec((tm, d), lambda i, te: (i, 0)),
                  by_expert((d, 2 * ff)), by_expert((1, ff)), by_expert((1, ff)),
                  by_expert((ff, d)), by_expert((1, d))],
        out_specs=pl.BlockSpec((tm, d), lambda i, te: (i, 0)),
    )
    return pl.pallas_call(
        _expert_kernel,
        grid_spec=grid_spec,
        out_shape=jax.ShapeDtypeStruct((p, d), F32),
        compiler_params=_cparams(("arbitrary",)),
        name="moe_experts",
    )(tile_expert, xs, w_gu, b_glu, b_lin, w_down, b_down)


def _moe(h_f32, h_bf16, w_router, b_router, layer, w_gu, b_glu, b_lin, w_down, b_down):
    n, d = h_f32.shape
    tm = MOE_TM
    idx128, wt128, rank128, cnt = _router(h_f32, w_router, b_router)
    expert = idx128[:, :TOP_K]
    na = n * TOP_K
    p = na + N_EXPERTS * tm
    counts = cnt[0].astype(I32)
    padded = ((counts + tm - 1) // tm) * tm
    pend = jnp.cumsum(padded)
    pstart = pend - padded
    first = jnp.sum(jnp.where(expert[..., None] == jnp.arange(N_EXPERTS, dtype=I32), pstart, 0), axis=-1)
    slot = first + rank128[:, :TOP_K]
    token = jnp.arange(na, dtype=I32) // TOP_K
    row_token = jnp.zeros((p,), I32).at[slot.reshape(-1)].set(
        token, mode="promise_in_bounds", unique_indices=True)
    tile_expert = jnp.minimum(
        jnp.searchsorted(pend, jnp.arange(p // tm, dtype=I32) * tm, side="right"), N_EXPERTS - 1).astype(I32)
    xs = h_bf16.at[row_token].get(mode="promise_in_bounds")
    ys = _experts(tile_expert + layer * N_EXPERTS, xs, w_gu, b_glu, b_lin, w_down, b_down)
    return [ys.at[slot[:, k]].get(mode="promise_in_bounds") for k in range(TOP_K)], wt128


def kernel(x, ln1_g, ln1_b, ln2_g, ln2_b, ev_w_in, ev_w_out, hg_lb_raw, hg_norm_g, od_w_in, od_w_out,
           cmp_k_w1, cmp_k_w2, cmp_k_pos, cmp_v_w1, cmp_v_w2, cmp_v_pos, rel_bias, router_w, router_b,
           exp_w_gu, exp_b_gu, exp_w_down, exp_b_down):
    b, t, d = x.shape
    n = b * t
    depth = ln1_g.shape[0]
    alpha = (2 * depth) ** 0.25
    lb_soft = jax.nn.softmax(hg_lb_raw.astype(F32), axis=0)
    lower_bounds = jnp.cumsum(lb_soft, axis=0) - lb_soft[0]
    pb, bnear, bfar, bpat = _nsa_bias_tables(rel_bias, t)

    n_all = depth * N_EXPERTS
    ff = exp_w_down.shape[2]
    wgu_all = _gu_prep(exp_w_gu.reshape(n_all, d, 2 * ff))
    wd_all = exp_w_down.astype(BF16).reshape(n_all, ff, d)
    bg_all = exp_b_gu[..., 0::2].reshape(n_all, 1, ff)
    bl_all = exp_b_gu[..., 1::2].reshape(n_all, 1, ff)
    bd_all = exp_b_down.reshape(n_all, 1, d)

    h = x.reshape(n, d)
    hb = h.astype(BF16)
    for layer in range(depth):
        if layer % 2 == 0:
            e = layer // 2
            proj = _matmul(hb, ev_w_in, e)
            o_a = _sb_attention(proj, b, t)
            o_b = _hgrn2(proj, lower_bounds[layer], hg_norm_g[e], b, t)
            mix = _matmul2(o_a, o_b, ev_w_out, e)
        else:
            o = layer // 2
            proj = _matmul(hb, od_w_in, o, n=NSA_WIDTH + 6 * NSA_KV_WIDTH)
            gate_logits = _matmul(hb, od_w_in[o, :, NSA_WIDTH + 6 * NSA_KV_WIDTH:])
            proj3 = proj.reshape(b, t, -1)
            c0 = NSA_WIDTH // HEAD_DIM
            k_cmp = _compress(proj3, c0, cmp_k_w1[o], cmp_k_w2[o], cmp_k_pos[o])
            v_cmp = _compress(proj3, c0 + NSA_KV_GROUPS, cmp_v_w1[o], cmp_v_w2[o], cmp_v_pos[o])
            o_c, sel_t = _cmp_select(proj3, k_cmp, v_cmp, pb, b, t)
            sel = jnp.swapaxes(sel_t, 2, 3)
            o_s = _nsa_selected(proj3, c0 + 2 * NSA_KV_GROUPS, c0 + 3 * NSA_KV_GROUPS, bnear, bfar, sel, b, t)
            o_w = _nsa_window(proj3, c0 + 4 * NSA_KV_GROUPS, c0 + 5 * NSA_KV_GROUPS, bpat, b, t)
            mixed = _gate_mix(gate_logits, o_c.reshape(n, -1), o_s.reshape(n, -1), o_w.reshape(n, -1))
            mix = _matmul(mixed, od_w_out, o)
        h, hb = _add_ln(h, mix, ln1_g[layer], ln1_b[layer], alpha)
        ys, gate_w = _moe(h, hb, router_w[layer], router_b[layer], layer,
                          wgu_all, bg_all, bl_all, wd_all, bd_all)
        h, hb = _combine_ln(h, ys, gate_w, ln2_g[layer], ln2_b[layer], alpha)
    return h.reshape(b, t, d)
```

```python
import functools
import math

import jax
import jax.numpy as jnp
import numpy as np
from jax import lax
from jax.experimental import pallas as pl
from jax.experimental.pallas import tpu as pltpu

F32 = jnp.float32
BF16 = jnp.bfloat16
I32 = jnp.int32

HEAD_DIM = 128
SB_HEADS = 16
HG_HEADS = 16
SB_WIDTH = SB_HEADS * HEAD_DIM
HG_WIDTH = HG_HEADS * HEAD_DIM
F_MIN = 1e-6
NSA_HEADS = 32
NSA_KV_GROUPS = 4
NSA_HPG = NSA_HEADS // NSA_KV_GROUPS
NSA_WIDTH = NSA_HEADS * HEAD_DIM
NSA_KV_WIDTH = NSA_KV_GROUPS * HEAD_DIM
CMP_LEN = 32
CMP_STRIDE = 16
SLC_BLOCK = 64
N_SELECT = 16
WINDOW = 512
REL_BUCKETS = 32
REL_MAX_DIST = 128
N_EXPERTS = 32
TOP_K = 4
EXPERT_FF = 384
SWIGLU_LIMIT = 7.0
SWIGLU_ALPHA = 1.702
LN_EPS = 1e-5
NEG_INF = -1e30
FORCED = 1e9

LANES = 128
VMEM_LIMIT = 56 * 1024 * 1024

HI = lax.Precision.HIGHEST
LOG2E = 1.4426950408889634


def _cparams(sem):
    return pltpu.CompilerParams(dimension_semantics=sem, vmem_limit_bytes=VMEM_LIMIT)


def _dot_nt(a, b, **kw):
    return lax.dot_general(a, b, (((1,), (1,)), ((), ())), preferred_element_type=F32, **kw)


def _dot(a, b, **kw):
    return jnp.dot(a, b, preferred_element_type=F32, **kw)


def _mm_kernel(a_ref, w_ref, o_ref):
    o_ref[...] = _dot(a_ref[...], w_ref[...].astype(BF16)).astype(o_ref.dtype)


def _matmul(a, w, layer=None, *, n=None, tm=1024, tn=512, out_dtype=F32):
    m, k = a.shape
    n = w.shape[-1] if n is None else n
    tn = min(tn, n)
    assert m % tm == 0 and n % tn == 0
    if w.ndim == 3:
        w_spec = pl.BlockSpec((None, k, tn), lambda i, j: (layer, 0, j))
    else:
        w_spec = pl.BlockSpec((k, tn), lambda i, j: (0, j))
    return pl.pallas_call(
        _mm_kernel,
        grid=(m // tm, n // tn),
        in_specs=[pl.BlockSpec((tm, k), lambda i, j: (i, 0)), w_spec],
        out_specs=pl.BlockSpec((tm, tn), lambda i, j: (i, j)),
        out_shape=jax.ShapeDtypeStruct((m, n), out_dtype),
        compiler_params=_cparams(("parallel", "arbitrary")),
        name="matmul",
    )(a, w)


def _mm2_kernel(a1_ref, a2_ref, w1_ref, w2_ref, o_ref):
    o_ref[...] = (_dot(a1_ref[...], w1_ref[...].astype(BF16))
                  + _dot(a2_ref[...], w2_ref[...].astype(BF16)))


def _matmul2(a1, a2, w, layer, *, tm=1024, tn=512):
    m, k1 = a1.shape
    k2 = a2.shape[1]
    n = w.shape[-1]
    assert k1 == k2 and w.shape[1] == k1 + k2
    return pl.pallas_call(
        _mm2_kernel,
        grid=(m // tm, n // tn),
        in_specs=[pl.BlockSpec((tm, k1), lambda i, j: (i, 0)),
                  pl.BlockSpec((tm, k2), lambda i, j: (i, 0)),
                  pl.BlockSpec((None, k1, tn), lambda i, j: (layer, 0, j)),
                  pl.BlockSpec((None, k2, tn), lambda i, j: (layer, 1, j))],
        out_specs=pl.BlockSpec((tm, tn), lambda i, j: (i, j)),
        out_shape=jax.ShapeDtypeStruct((m, n), F32),
        compiler_params=_cparams(("parallel", "arbitrary")),
        name="matmul2",
    )(a1, a2, w, w)


def _ln_store(x, g_ref, b_ref, of_ref, ob_ref):
    mu = jnp.mean(x, axis=-1, keepdims=True)
    xc = x - mu
    var = jnp.mean(xc * xc, axis=-1, keepdims=True)
    out = xc * lax.rsqrt(var + LN_EPS) * g_ref[...] + b_ref[...]
    of_ref[...] = out
    ob_ref[...] = out.astype(BF16)


def _add_ln_kernel(h_ref, y_ref, g_ref, b_ref, of_ref, ob_ref, *, alpha):
    _ln_store(alpha * h_ref[...] + y_ref[...], g_ref, b_ref, of_ref, ob_ref)


def _combine_ln_kernel(h_ref, *refs, alpha):
    ys, (w_ref, g_ref, b_ref, of_ref, ob_ref) = refs[:TOP_K], refs[TOP_K:]
    w = w_ref[...]
    x = alpha * h_ref[...]
    for k in range(TOP_K):
        x = x + w[:, k:k + 1] * ys[k][...]
    _ln_store(x, g_ref, b_ref, of_ref, ob_ref)


def _combine_ln(h, ys, w128, g, b, alpha, *, tm=128):
    n, d = h.shape
    row = pl.BlockSpec((tm, d), lambda i: (i, 0))
    vec = pl.BlockSpec((1, d), lambda i: (0, 0))
    return pl.pallas_call(
        functools.partial(_combine_ln_kernel, alpha=alpha),
        grid=(n // tm,),
        in_specs=[row] * (1 + TOP_K) + [pl.BlockSpec((tm, LANES), lambda i: (i, 0)), vec, vec],
        out_specs=[row, row],
        out_shape=[jax.ShapeDtypeStruct((n, d), F32), jax.ShapeDtypeStruct((n, d), BF16)],
        compiler_params=_cparams(("parallel",)),
        name="moe_combine_ln",
    )(h, *ys, w128, g.reshape(1, d), b.reshape(1, d))


def _add_ln(h, y, g, b, alpha, *, tm=256):
    n, d = h.shape
    row = pl.BlockSpec((tm, d), lambda i: (i, 0))
    vec = pl.BlockSpec((1, d), lambda i: (0, 0))
    return pl.pallas_call(
        functools.partial(_add_ln_kernel, alpha=alpha),
        grid=(n // tm,),
        in_specs=[row, row, vec, vec],
        out_specs=[row, row],
        out_shape=[jax.ShapeDtypeStruct((n, d), F32), jax.ShapeDtypeStruct((n, d), BF16)],
        compiler_params=_cparams(("parallel",)),
        name="add_ln",
    )(h, y, g.reshape(1, d), b.reshape(1, d))


SB_TQ = 512
SB_KB = LANES
SB_DEAD = -104.0


def _sb_kernel(q_ref, k_ref, v_ref, u_ref, o_ref, kb_ref, vb_ref, acc_ref, carry_ref, *, tq, scale):
    i = pl.program_id(2)
    kb = SB_KB
    nb = tq // kb

    @pl.when(i == 0)
    def _():
        kb_ref[...] = k_ref[...].astype(BF16)
        vb_ref[...] = v_ref[...].astype(BF16)

    q = q_ref[...].astype(BF16)
    u = u_ref[...]
    acc_ref[...] = jnp.zeros_like(acc_ref)
    carry_ref[...] = jnp.zeros_like(carry_ref)

    def block_terms(z, causal=None):
        sp = jnp.log(1.0 + jnp.exp(-jnp.abs(z)))
        ls = jnp.minimum(z, 0.0) - sp
        lk = ls - z
        if causal is not None:
            lk = jnp.where(causal, lk, 0.0)
        hi = lk.astype(BF16)
        lo = (lk - hi.astype(F32)).astype(BF16)
        rt = _dot(jnp.concatenate([hi, lo], axis=1), u)
        return ls, rt[:, :kb] - lk, rt[:, kb:]

    base = pl.multiple_of(i * tq, tq)
    for j in reversed(range(nb)):
        r0 = j * kb
        rows = tq - r0
        kj = kb_ref[pl.ds(base + r0, kb), :]
        vj = vb_ref[pl.ds(base + r0, kb), :]
        z = _dot_nt(q[r0:], kj) * scale
        causal = lax.broadcasted_iota(I32, (rows, kb), 1) < lax.broadcasted_iota(I32, (rows, kb), 0)
        ls, between, tot = block_terms(z, causal)
        carry = carry_ref[r0:, :]
        w = jnp.where(causal, jnp.exp(ls + between + carry), 0.0)
        acc_ref[r0:, :] += _dot(w.astype(BF16), vj)
        carry_ref[r0:, :] = carry + tot

    def far(state):
        c, _ = state
        start = pl.multiple_of(c * tq, tq)
        z = _dot_nt(q, kb_ref[pl.ds(start, tq), :]) * scale
        terms = [block_terms(z[:, j * kb:(j + 1) * kb]) for j in range(nb)]
        carry = carry_ref[...]
        ws = [None] * nb
        for j in reversed(range(nb)):
            ls, between, tot = terms[j]
            ws[j] = jnp.exp(ls + between + carry).astype(BF16)
            carry = carry + tot
        acc_ref[...] += _dot(jnp.concatenate(ws, axis=1), vb_ref[pl.ds(start, tq), :])
        carry_ref[...] = carry
        return c - 1, jnp.max(carry)

    lax.while_loop(lambda st: (st[0] >= 0) & (st[1] > SB_DEAD), far, (i - 1, jnp.max(carry_ref[...])))
    o_ref[...] = acc_ref[...].astype(o_ref.dtype)


def _sb_attention(proj, b, t):
    tq, kb = min(SB_TQ, t), SB_KB
    nq = t // tq
    r = np.arange(2 * kb)[:, None] % kb
    c = np.arange(2 * kb)[None, :]
    u = jnp.asarray(np.where(c < kb, r >= c, True), BF16)
    kv_spec = lambda off: pl.BlockSpec((t, HEAD_DIM), lambda bi, h, i: (bi, off + h))
    return pl.pallas_call(
        functools.partial(_sb_kernel, tq=tq, scale=HEAD_DIM ** -0.5),
        grid=(b, SB_HEADS, nq),
        in_specs=[pl.BlockSpec((tq, HEAD_DIM), lambda bi, h, i: (bi * nq + i, h)),
                  kv_spec(SB_HEADS), kv_spec(2 * SB_HEADS),
                  pl.BlockSpec((2 * kb, 2 * kb), lambda bi, h, i: (0, 0))],
        out_specs=pl.BlockSpec((tq, HEAD_DIM), lambda bi, h, i: (bi * nq + i, h)),
        out_shape=jax.ShapeDtypeStruct((b * t, SB_WIDTH), BF16),
        scratch_shapes=[pltpu.VMEM((t, HEAD_DIM), BF16), pltpu.VMEM((t, HEAD_DIM), BF16),
                        pltpu.VMEM((tq, HEAD_DIM), F32), pltpu.VMEM((tq, kb), F32)],
        compiler_params=_cparams(("parallel", "parallel", "arbitrary")),
        name="sb_attention",
    )(proj, proj, proj, u)


HG_C = 64
HG_TB = 512
HG_NH = 8


def _hg_tables(c):
    levels = []
    m = c // 2
    while m >= 1:
        levels.append(m)
        m //= 2
    nl = len(levels)
    e = np.zeros((nl + 2, c, c), np.float32)
    mask = np.zeros((nl + 1, c, c), np.float32)
    p = np.arange(c)
    for li, m in enumerate(levels):
        blk = p // (2 * m)
        half = (p // m) % 2
        mid = blk * 2 * m + m - 1
        for r in range(c):
            if half[r] == 1:
                e[li, r, mid[r] + 1:r + 1] = 1.0
            else:
                e[li, r, r + 1:mid[r] + 1] = 1.0
        mask[li] = ((half[:, None] == 1) & (half[None, :] == 0) & (blk[:, None] == blk[None, :]))
    mask[nl] = np.eye(c)
    e[nl] = np.tril(np.ones((c, c)))
    e[nl + 1] = np.triu(np.ones((c, c)), 1)
    return e.reshape((nl + 2) * c, c), mask, nl


def _hg_kernel(qh_ref, fh_ref, ih_ref, gh_ref, lb_ref, ng_ref, e_ref, mask_ref, o_ref, st_ref,
               *, c, tb, nl, nh):
    @pl.when(pl.program_id(2) == 0)
    def _():
        st_ref[...] = jnp.zeros_like(st_ref)

    ng = ng_ref[...]
    emat = e_ref[...]

    def one_head(rows, hh):
        cols = slice(hh * HEAD_DIM, (hh + 1) * HEAD_DIM)
        lb = lb_ref[:, cols]
        qh = qh_ref[rows, cols]
        f = lb + (1.0 - lb) * jax.nn.sigmoid(fh_ref[rows, cols])
        g = jnp.log(jnp.maximum(f, F_MIN))
        kk = 1.0 - f
        q = qh * jax.nn.sigmoid(qh)
        v = ih_ref[rows, cols]
        vb = v.astype(BF16)
        g1 = g.astype(BF16)
        r1 = g - g1.astype(F32)
        g2 = r1.astype(BF16)
        g3 = (r1 - g2.astype(F32)).astype(BF16)
        ex = jnp.exp(_dot(emat, jnp.concatenate([g1, g2, g3], axis=0)))
        scores = mask_ref[nl] * _dot_nt(q.astype(BF16), kk.astype(BF16))
        for li in range(nl):
            a = ex[li * c:(li + 1) * c]
            scores = scores + mask_ref[li] * _dot_nt((q * a).astype(BF16), (kk * a).astype(BF16))
        ecum = ex[nl * c:(nl + 1) * c]
        erest = ex[(nl + 1) * c:(nl + 2) * c]
        st = st_ref[hh]
        o = _dot(scores.astype(BF16), vb) + _dot_nt((q * ecum).astype(BF16), st.astype(BF16))
        st_ref[hh] = ecum[c - 1:c, :] * st + _dot(v.T.astype(BF16), (kk * erest).astype(BF16))
        gh = gh_ref[rows, cols]
        o = o * lax.rsqrt(jnp.mean(o * o, axis=-1, keepdims=True) + LN_EPS) * ng
        o_ref[rows, cols] = (o * (gh * jax.nn.sigmoid(gh))).astype(o_ref.dtype)

    def chunk(ci, carry):
        rows = pl.ds(pl.multiple_of(ci * c, c), c)
        for hh in range(nh):
            one_head(rows, hh)
        return carry

    lax.fori_loop(0, tb // c, chunk, 0)


def _hgrn2(proj, lb, norm_g, b, t):
    c, tb, nh = HG_C, min(HG_TB, t), HG_NH
    e, mask, nl = _hg_tables(c)
    e3 = np.concatenate([e, e, e], axis=1)
    nt = t // tb
    base = 3 * SB_HEADS
    assert base % nh == 0 and HG_HEADS % nh == 0
    wide = nh * HEAD_DIM
    col = lambda k: pl.BlockSpec((tb, wide), lambda bi, h, i: (bi * nt + i, (base + k * HG_HEADS) // nh + h))
    const2 = lambda a: pl.BlockSpec(a.shape, lambda bi, h, i: (0,) * a.ndim)
    return pl.pallas_call(
        functools.partial(_hg_kernel, c=c, tb=tb, nl=nl, nh=nh),
        grid=(b, HG_HEADS // nh, nt),
        in_specs=[col(0), col(1), col(2), col(3),
                  pl.BlockSpec((1, wide), lambda bi, h, i: (0, h)),
                  pl.BlockSpec((1, HEAD_DIM), lambda bi, h, i: (0, 0)),
                  const2(e3), const2(mask)],
        out_specs=pl.BlockSpec((tb, wide), lambda bi, h, i: (bi * nt + i, h)),
        out_shape=jax.ShapeDtypeStruct((b * t, HG_WIDTH), BF16),
        scratch_shapes=[pltpu.VMEM((nh, HEAD_DIM, HEAD_DIM), F32)],
        compiler_params=_cparams(("parallel", "parallel", "arbitrary")),
        name="hgrn2",
    )(proj, proj, proj, proj, lb.reshape(1, HG_WIDTH), norm_g.reshape(1, HEAD_DIM),
      jnp.asarray(e3, BF16), jnp.asarray(mask))


def _cmp_kernel(a_ref, w1_ref, w2_ref, pos_ref, o_ref, *, ncp):
    half = CMP_LEN // 2
    p = jnp.zeros((ncp, HEAD_DIM), F32)
    q = jnp.zeros((ncp, HEAD_DIM), F32)
    for j in range(half):
        s = a_ref[pl.ds(j, ncp, stride=CMP_STRIDE), :]
        p = p + _dot((s + pos_ref[j:j + 1, :]).astype(BF16), w1_ref[j].astype(BF16))
        q = q + _dot((s + pos_ref[half + j:half + j + 1, :]).astype(BF16), w1_ref[half + j].astype(BF16))
    hid = p + pltpu.roll(q, ncp - 1, 0)
    hid = hid * jax.nn.sigmoid(hid)
    out = _dot(hid.astype(BF16), w2_ref[...].astype(BF16))
    row = lax.broadcasted_iota(I32, (ncp, HEAD_DIM), 0)
    o_ref[0, 0] = jnp.where(row < ncp - 1, out, 0.0)


def _compress(proj3, col0, w1, w2, pos):
    b, t, _ = proj3.shape
    ncp = t // CMP_STRIDE
    assert CMP_LEN == 2 * CMP_STRIDE
    return pl.pallas_call(
        functools.partial(_cmp_kernel, ncp=ncp),
        grid=(b, NSA_KV_GROUPS),
        in_specs=[pl.BlockSpec((None, t, HEAD_DIM), lambda bi, g: (bi, 0, col0 + g)),
                  pl.BlockSpec((CMP_LEN, HEAD_DIM, HEAD_DIM), lambda bi, g: (0, 0, 0)),
                  pl.BlockSpec((HEAD_DIM, HEAD_DIM), lambda bi, g: (0, 0)),
                  pl.BlockSpec((CMP_LEN, HEAD_DIM), lambda bi, g: (0, 0))],
        out_specs=pl.BlockSpec((1, 1, ncp, HEAD_DIM), lambda bi, g: (bi, g, 0, 0)),
        out_shape=jax.ShapeDtypeStruct((b, NSA_KV_GROUPS, ncp, HEAD_DIM), F32),
        compiler_params=_cparams(("parallel", "parallel")),
        name="nsa_compress",
    )(proj3, w1.reshape(CMP_LEN, HEAD_DIM, HEAD_DIM), w2, pos)


NSA_TQ = 128
CMP_NEAR = 16
CMP_NEAR_LO = 9


def _rel_bucket_np(dist):
    dist = np.maximum(dist, 0)
    max_exact = REL_BUCKETS // 2
    ratio = (np.log(np.maximum(dist, max_exact).astype(np.float32) / np.float32(max_exact))
             / np.float32(math.log(REL_MAX_DIST / max_exact)))
    large = np.minimum(max_exact + (ratio * np.float32(REL_BUCKETS - max_exact)).astype(np.int32),
                       REL_BUCKETS - 1)
    return np.where(dist < max_exact, dist, large).astype(np.int32)


def _cmpsel_kernel(q_ref, kc_ref, vc_ref, pb_ref, ovl_ref, oc_ref, sel_ref,
                   *, tq, ncp, nslc, nsel, scale):
    i = pl.program_id(2)
    kc = kc_ref[0, 0].astype(BF16)
    vc = vc_ref[0, 0].astype(BF16)
    qpos = i * tq + lax.broadcasted_iota(I32, (tq, ncp), 0)
    ncol = lax.broadcasted_iota(I32, (tq, ncp), 1)
    valid = (ncol * CMP_STRIDE + (CMP_LEN - 1) <= qpos) & (ncol < ncp - 1)
    sr = lax.broadcasted_iota(I32, (LANES, ncp), 0)
    sc = lax.broadcasted_iota(I32, (LANES, ncp), 1)
    first = i * (tq // CMP_STRIDE) - CMP_NEAR_LO
    place = jnp.where(sr < CMP_NEAR, jnp.where(sc == first + sr, 1.0, 0.0),
                      jnp.where(sr == CMP_NEAR, jnp.where(sc < first, 1.0, 0.0), 0.0))
    place = place.astype(BF16)
    bias = _dot(pb_ref[0], jnp.concatenate([place, place], axis=0))
    psum = jnp.zeros((tq, ncp), F32)
    for r in range(NSA_HPG):
        hs = slice(r * HEAD_DIM, (r + 1) * HEAD_DIM)
        s = _dot_nt(q_ref[0, :, hs].astype(BF16), kc) * scale + bias[r * tq:(r + 1) * tq]
        s = jnp.where(valid, s, NEG_INF)
        mx = jnp.max(s, axis=-1, keepdims=True)
        e = jnp.where(valid, jnp.exp(s - mx), 0.0)
        den = jnp.sum(e, axis=-1, keepdims=True)
        p = e / jnp.maximum(den, 1e-30)
        oc_ref[0, :, hs] = _dot(p.astype(BF16), vc)
        psum = psum + p
    imp = _dot_nt(ovl_ref[...], psum, precision=HI)
    jrow = lax.broadcasted_iota(I32, (nslc, tq), 0)
    qblk = (i * tq + lax.broadcasted_iota(I32, (nslc, tq), 1)) // SLC_BLOCK
    ok = jrow <= qblk
    forced = (jrow == 0) | (jrow == qblk) | (jrow == qblk - 1)
    imp = jnp.where(ok, jnp.where(forced, FORCED, imp), NEG_INF)
    rank = jnp.zeros((nslc, tq), F32)
    for j2 in range(nslc):
        row = imp[j2:j2 + 1, :]
        tie = jnp.where(jrow > j2, 1.0, 0.0)
        rank = rank + jnp.where(row > imp, 1.0, jnp.where(row == imp, tie, 0.0))
    sel_ref[0, 0] = jnp.where((rank < nsel) & ok, 1.0, 0.0)


def _cmp_select(proj3, k_cmp, v_cmp, pb, b, t):
    tq = NSA_TQ
    ncp = t // CMP_STRIDE
    nslc = t // SLC_BLOCK
    nsel = min(N_SELECT, nslc)
    n_idx = np.arange(ncp)
    slc_start = np.arange(nslc) * SLC_BLOCK
    cs = n_idx * CMP_STRIDE
    ovl = ((cs[None, :] < slc_start[:, None] + SLC_BLOCK)
           & (cs[None, :] + CMP_LEN - 1 >= slc_start[:, None])
           & (n_idx[None, :] < ncp - 1)).astype(np.float32)
    gw = NSA_HPG * HEAD_DIM
    return pl.pallas_call(
        functools.partial(_cmpsel_kernel, tq=tq, ncp=ncp, nslc=nslc, nsel=nsel, scale=HEAD_DIM ** -0.5),
        grid=(b, NSA_KV_GROUPS, t // tq),
        in_specs=[pl.BlockSpec((1, tq, gw), lambda bi, g, i: (bi, i, g)),
                  pl.BlockSpec((1, 1, ncp, HEAD_DIM), lambda bi, g, i: (bi, g, 0, 0)),
                  pl.BlockSpec((1, 1, ncp, HEAD_DIM), lambda bi, g, i: (bi, g, 0, 0)),
                  pl.BlockSpec((1, NSA_HPG * tq, 2 * LANES), lambda bi, g, i: (g, 0, 0)),
                  pl.BlockSpec((nslc, ncp), lambda bi, g, i: (0, 0))],
        out_specs=[pl.BlockSpec((1, tq, gw), lambda bi, g, i: (bi, i, g)),
                   pl.BlockSpec((1, 1, nslc, tq), lambda bi, g, i: (bi, g, 0, i))],
        out_shape=[jax.ShapeDtypeStruct((b, t, NSA_WIDTH), F32),
                   jax.ShapeDtypeStruct((b, NSA_KV_GROUPS, nslc, t), F32)],
        compiler_params=_cparams(("parallel", "parallel", "parallel")),
        name="nsa_cmp_select",
    )(proj3, k_cmp, v_cmp, pb, jnp.asarray(ovl))


SEL_CHUNK = 512
SEL_VEXT = 16
MASKED = -1e30
FAR_BIAS_COL = SLC_BLOCK


def _softmax_step(st, vt, m_ref, acc_ref, c1):
    m_old = m_ref[...]
    m_new = jnp.maximum(m_old, jnp.max(st, axis=0, keepdims=True))
    p = jnp.exp2((st - m_new) * c1)
    acc_ref[...] = jnp.exp2((m_old - m_new) * c1) * acc_ref[...] + _dot(vt, p.astype(BF16))
    m_ref[...] = m_new


def _sel_kernel(q_ref, k_ref, v_ref, kext_ref, vext_ref, bnear_ref, bfar_ref, sel_ref, o_ref,
                qa_ref, ka_ref, vt_ref, m_ref, acc_ref, s0_ref, s1_ref, *, tq, c1):
    i = pl.program_id(2)
    hpg = NSA_HPG
    d = HEAD_DIM

    @pl.when(i == 0)
    def _():
        ka_ref[:, :d] = k_ref[...].astype(BF16)
        ka_ref[:, d:] = kext_ref[...]
        for c in range(vt_ref.shape[0]):
            vt_ref[c, :d, :] = v_ref[c * SEL_CHUNK:(c + 1) * SEL_CHUNK, :].T.astype(BF16)
            vt_ref[c, d:, :] = vext_ref[...]

    sel = sel_ref[0, 0]
    nslc = sel.shape[1]
    blk = lax.broadcasted_iota(I32, (tq, nslc), 1)
    far_blocks = (i - 1) * (tq // SLC_BLOCK)
    maskq = jnp.where(blk < far_blocks, jnp.where(sel > 0.5, 0.0, MASKED), MASKED)
    ext = jnp.concatenate([maskq, jnp.zeros((tq, LANES - nslc), F32)], axis=1)
    for r in range(hpg):
        qa_ref[r * tq:(r + 1) * tq, :d] = q_ref[0, :, r * d:(r + 1) * d].astype(BF16)
        qa_ref[r * tq:(r + 1) * tq, d:] = (ext + bfar_ref[0, r:r + 1, :]).astype(BF16)
    m_ref[...] = jnp.full_like(m_ref, MASKED)
    acc_ref[...] = jnp.zeros_like(acc_ref)

    n_chunks = vt_ref.shape[0]

    def logits(c):
        start = pl.multiple_of(jnp.minimum(c, n_chunks - 1) * SEL_CHUNK, SEL_CHUNK)
        return _dot_nt(ka_ref[pl.ds(start, SEL_CHUNK), :], qa_ref[...])

    n_far = (jnp.maximum(i - 1, 0) * tq + SEL_CHUNK - 1) // SEL_CHUNK
    n_pairs = (n_far + 1) // 2

    @pl.when(n_pairs > 0)
    def _():
        s0_ref[...] = logits(0)

    def pair(j, carry):
        c0 = 2 * j
        s1_ref[...] = logits(c0 + 1)
        _softmax_step(s0_ref[...], vt_ref[c0], m_ref, acc_ref, c1)
        s0_ref[...] = logits(c0 + 2)
        _softmax_step(s1_ref[...], vt_ref[c0 + 1], m_ref, acc_ref, c1)
        return carry

    lax.fori_loop(0, n_pairs, pair, 0)

    p0 = pl.multiple_of(jnp.maximum(i - 1, 0) * tq, tq)
    p1 = pl.multiple_of(i * tq, tq)
    kn = jnp.concatenate([ka_ref[pl.ds(p0, tq), :d], ka_ref[pl.ds(p1, tq), :d]], axis=0)
    vn = jnp.concatenate([v_ref[pl.ds(p0, tq), :], v_ref[pl.ds(p1, tq), :]], axis=0)
    vnt = jnp.concatenate([vn.T.astype(BF16), vext_ref[:, :2 * tq]], axis=0)
    er = lax.broadcasted_iota(I32, (2 * tq, nslc), 1)
    ec = lax.broadcasted_iota(I32, (2 * tq, nslc), 0) // SLC_BLOCK
    expand = jnp.where(er == far_blocks + ec, 1.0, 0.0).astype(BF16)
    picked = _dot_nt(expand, sel.astype(BF16))
    kj = lax.broadcasted_iota(I32, (2 * tq, tq), 0)
    qi = lax.broadcasted_iota(I32, (2 * tq, tq), 1)
    keep = jnp.where(kj <= qi + tq, picked, 0.0)
    keep = jnp.concatenate([keep] * hpg, axis=1) > 0.5
    st = _dot_nt(kn, qa_ref[:, :d]) + bnear_ref[0]
    _softmax_step(jnp.where(keep, st, MASKED), vnt, m_ref, acc_ref, c1)

    acc = acc_ref[...]
    out = acc[:d, :] / acc[d:d + 1, :]
    for r in range(hpg):
        o_ref[0, :, r * d:(r + 1) * d] = out[:, r * tq:(r + 1) * tq].T


def _nsa_selected(proj3, kcol, vcol, bnear, bfar, sel, b, t):
    tq = NSA_TQ
    d = HEAD_DIM
    gw = NSA_HPG * d
    nslc = t // SLC_BLOCK
    assert nslc <= FAR_BIAS_COL and t % SEL_CHUNK == 0
    rows = NSA_HPG * tq
    pos = np.arange(t)
    kext = np.zeros((t, LANES), np.float32)
    kext[pos, pos // SLC_BLOCK] = 1.0
    kext[:, FAR_BIAS_COL:FAR_BIAS_COL + 2] = 1.0
    vext = np.zeros((SEL_VEXT, SEL_CHUNK), np.float32)
    vext[0, :] = 1.0
    kv = lambda col: pl.BlockSpec((None, t, d), lambda bi, g, i: (bi, 0, col + g))
    const = lambda shape: pl.BlockSpec(shape, lambda bi, g, i: (0,) * len(shape))
    return pl.pallas_call(
        functools.partial(_sel_kernel, tq=tq, c1=d ** -0.5 * LOG2E),
        grid=(b, NSA_KV_GROUPS, t // tq),
        in_specs=[pl.BlockSpec((1, tq, gw), lambda bi, g, i: (bi, i, g)),
                  kv(kcol), kv(vcol), const((t, LANES)), const((SEL_VEXT, SEL_CHUNK)),
                  pl.BlockSpec((1, 2 * tq, rows), lambda bi, g, i: (g, 0, 0)),
                  pl.BlockSpec((1, NSA_HPG, LANES), lambda bi, g, i: (g, 0, 0)),
                  pl.BlockSpec((1, 1, tq, nslc), lambda bi, g, i: (bi, g, i, 0))],
        out_specs=pl.BlockSpec((1, tq, gw), lambda bi, g, i: (bi, i, g)),
        out_shape=jax.ShapeDtypeStruct((b, t, NSA_WIDTH), F32),
        scratch_shapes=[pltpu.VMEM((rows, 2 * d), BF16),
                        pltpu.VMEM((t, 2 * d), BF16),
                        pltpu.VMEM((t // SEL_CHUNK, d + SEL_VEXT, SEL_CHUNK), BF16),
                        pltpu.VMEM((1, rows), F32),
                        pltpu.VMEM((d + SEL_VEXT, rows), F32),
                        pltpu.VMEM((SEL_CHUNK, rows), F32),
                        pltpu.VMEM((SEL_CHUNK, rows), F32)],
        compiler_params=_cparams(("parallel", "parallel", "arbitrary")),
        name="nsa_selected",
    )(proj3, proj3, proj3, jnp.asarray(kext, BF16), jnp.asarray(vext, BF16),
      jnp.swapaxes(bnear, 1, 2), bfar, sel)


def _win_kernel(q_ref, k_ref, v_ref, bpat_ref, o_ref, kb_ref, vb_ref, *, tq, window, c1):
    i = pl.program_id(2)
    hpg = NSA_HPG
    d = HEAD_DIM
    nt = window // tq + 1

    @pl.when(i == 0)
    def _():
        kb_ref[...] = k_ref[...].astype(BF16)
        vb_ref[...] = v_ref[...].astype(BF16)

    qs = jnp.concatenate([q_ref[0, :, r * d:(r + 1) * d] for r in range(hpg)], axis=0).astype(BF16)
    t0 = jnp.maximum(i - (nt - 1), 0)
    qi = lax.broadcasted_iota(I32, (tq, tq), 0)
    kj = lax.broadcasted_iota(I32, (tq, tq), 1)
    parts = []
    for kk in range(nt):
        dt = i - (t0 + kk)
        start = pl.multiple_of((t0 + kk) * tq, tq)
        z = _dot_nt(qs, kb_ref[pl.ds(start, tq), :]) + bpat_ref[0, jnp.clip(dt, 0, 2)]
        dist = dt * tq + qi - kj
        keep = jnp.where(dist >= 0, jnp.where(dist < window, 1.0, 0.0), 0.0) > 0.5
        parts.append(jnp.where(keep[None], z.reshape(hpg, tq, tq), MASKED))
    s = jnp.concatenate(parts, axis=-1)
    m = jnp.max(s, axis=-1, keepdims=True)
    p = jnp.exp2((s - m) * c1)
    den = jnp.sum(p, axis=-1, keepdims=True)
    vwin = vb_ref[pl.ds(pl.multiple_of(t0 * tq, tq), nt * tq), :]
    out = (_dot(p.reshape(hpg * tq, nt * tq).astype(BF16), vwin).reshape(hpg, tq, d) / den)
    for r in range(hpg):
        o_ref[0, :, r * d:(r + 1) * d] = out[r]


def _nsa_window(proj3, kcol, vcol, bpat, b, t):
    tq = NSA_TQ
    d = HEAD_DIM
    gw = NSA_HPG * d
    assert WINDOW % tq == 0 and t >= WINDOW + tq
    kv = lambda col: pl.BlockSpec((None, t, d), lambda bi, g, i: (bi, 0, col + g))
    return pl.pallas_call(
        functools.partial(_win_kernel, tq=tq, window=WINDOW, c1=d ** -0.5 * LOG2E),
        grid=(b, NSA_KV_GROUPS, t // tq),
        in_specs=[pl.BlockSpec((1, tq, gw), lambda bi, g, i: (bi, i, g)),
                  kv(kcol), kv(vcol),
                  pl.BlockSpec((1, 3, NSA_HPG * tq, tq), lambda bi, g, i: (g, 0, 0, 0))],
        out_specs=pl.BlockSpec((1, tq, gw), lambda bi, g, i: (bi, i, g)),
        out_shape=jax.ShapeDtypeStruct((b, t, NSA_WIDTH), F32),
        scratch_shapes=[pltpu.VMEM((t, d), BF16), pltpu.VMEM((t, d), BF16)],
        compiler_params=_cparams(("parallel", "parallel", "arbitrary")),
        name="nsa_window",
    )(proj3, proj3, proj3, bpat)


def _gate_mix_kernel(g_ref, oc_ref, os_ref, ow_ref, o_ref):
    gates = jax.nn.sigmoid(g_ref[...])
    for h in range(NSA_HEADS):
        hs = slice(h * HEAD_DIM, (h + 1) * HEAD_DIM)
        mix = (gates[:, 3 * h:3 * h + 1] * oc_ref[:, hs]
               + gates[:, 3 * h + 1:3 * h + 2] * os_ref[:, hs]
               + gates[:, 3 * h + 2:3 * h + 3] * ow_ref[:, hs])
        o_ref[:, hs] = mix.astype(o_ref.dtype)


def _gate_mix(gate_logits, o_c, o_s, o_w, *, tm=256):
    n = gate_logits.shape[0]
    row = pl.BlockSpec((tm, NSA_WIDTH), lambda i: (i, 0))
    return pl.pallas_call(
        _gate_mix_kernel,
        grid=(n // tm,),
        in_specs=[pl.BlockSpec((tm, 3 * NSA_HEADS), lambda i: (i, 0)), row, row, row],
        out_specs=row,
        out_shape=jax.ShapeDtypeStruct((n, NSA_WIDTH), BF16),
        compiler_params=_cparams(("parallel",)),
        name="nsa_gate_mix",
    )(gate_logits, o_c, o_s, o_w)


def _nsa_bias_tables(rel_bias, t):
    tq = NSA_TQ
    g, hpg = NSA_KV_GROUPS, NSA_HPG
    inv_scale = HEAD_DIM ** 0.5
    tab = rel_bias.astype(F32)
    last = REL_BUCKETS - 1
    far_from = int(np.nonzero(_rel_bucket_np(np.arange(4 * REL_MAX_DIST)) < last)[0].max()) + 1
    assert (_rel_bucket_np(np.arange(far_from, t + tq)) == last).all()

    def by_group(a):
        a = jnp.moveaxis(a, -1, 0)
        return a.reshape((g, hpg) + a.shape[1:])

    iq = np.arange(tq)[:, None]
    m = np.arange(CMP_NEAR)[None, :]
    dist_c = iq - CMP_STRIDE * (m - CMP_NEAR_LO) - (CMP_LEN - 1)
    assert dist_c[:, 0].min() >= far_from
    assert (iq - CMP_STRIDE * (CMP_NEAR - CMP_NEAR_LO) - (CMP_LEN - 1)).max() < 0
    pb = jnp.zeros((g, hpg, tq, LANES), F32)
    pb = pb.at[..., :CMP_NEAR].set(by_group(tab[_rel_bucket_np(dist_c)]))
    pb = pb.at[..., CMP_NEAR].set(by_group(tab[last])[..., None])
    pb = pb.reshape(g, hpg * tq, LANES)
    pb_hi = pb.astype(BF16)
    pb = jnp.concatenate([pb_hi, (pb - pb_hi.astype(F32)).astype(BF16)], axis=-1)
    assert 2 * tq - (tq - 1) >= far_from
    jk = np.arange(tq)[None, :]
    idx = np.stack([_rel_bucket_np(iq - jk), _rel_bucket_np(tq + iq - jk), np.full((tq, tq), last, np.int32)])
    pat = by_group(tab[idx]) * inv_scale
    bpat = jnp.transpose(pat, (0, 2, 1, 3, 4)).reshape(g, 3, hpg * tq, tq)
    bnear = jnp.concatenate([pat[:, :, 1], pat[:, :, 0]], axis=-1).reshape(g, hpg * tq, 2 * tq)
    far = by_group(tab[last]) * inv_scale
    far_hi = far.astype(BF16).astype(F32)
    bfar = jnp.zeros((g, hpg, LANES), F32)
    bfar = bfar.at[..., FAR_BIAS_COL].set(far_hi).at[..., FAR_BIAS_COL + 1].set(far - far_hi)
    return pb, bnear, bfar, bpat


def _router_kernel(h_ref, w_ref, b_ref, tri_ref, idx_ref, wt_ref, rank_ref, cnt_ref, *, tm):
    @pl.when(pl.program_id(0) == 0)
    def _():
        cnt_ref[...] = jnp.zeros_like(cnt_ref)

    logits = _dot(h_ref[...], w_ref[...], precision=HI) + b_ref[...]
    lane = lax.broadcasted_iota(I32, (tm, N_EXPERTS), 1).astype(F32)
    out_lane = lax.broadcasted_iota(I32, (tm, LANES), 1)
    work = logits
    idx_out = jnp.zeros((tm, LANES), F32)
    val_out = jnp.zeros((tm, LANES), F32)
    top = None
    den = jnp.zeros((tm, 1), F32)
    hots = []
    for k in range(TOP_K):
        mx = jnp.max(work, axis=-1, keepdims=True)
        idx = jnp.min(jnp.where(work == mx, lane, float(N_EXPERTS)), axis=-1, keepdims=True)
        if top is None:
            top = mx
        e = jnp.exp(mx - top)
        den = den + e
        idx_out = jnp.where(out_lane == k, idx, idx_out)
        val_out = jnp.where(out_lane == k, e, val_out)
        hots.append(jnp.where(lane == idx, 1.0, 0.0))
        work = jnp.where(lane == idx, -jnp.inf, work)
    idx_ref[...] = idx_out.astype(I32)
    wt_ref[...] = val_out / den
    hot = hots[0] + hots[1] + hots[2] + hots[3]
    before = cnt_ref[...] + _dot(tri_ref[...], hot.astype(BF16))
    rank_out = jnp.zeros((tm, LANES), F32)
    for k in range(TOP_K):
        rank_out = jnp.where(out_lane == k, jnp.sum(hots[k] * before, axis=-1, keepdims=True), rank_out)
    rank_ref[...] = rank_out.astype(I32)
    cnt_ref[...] += jnp.sum(hot, axis=0, keepdims=True)


def _router(h, w, bias, *, tm=512):
    n, d = h.shape
    out = pl.BlockSpec((tm, LANES), lambda i: (i, 0))
    tri = jnp.asarray(np.tril(np.ones((tm, tm), np.float32), -1), BF16)
    return pl.pallas_call(
        functools.partial(_router_kernel, tm=tm),
        grid=(n // tm,),
        in_specs=[pl.BlockSpec((tm, d), lambda i: (i, 0)),
                  pl.BlockSpec((d, N_EXPERTS), lambda i: (0, 0)),
                  pl.BlockSpec((1, N_EXPERTS), lambda i: (0, 0)),
                  pl.BlockSpec((tm, tm), lambda i: (0, 0))],
        out_specs=[out, out, out, pl.BlockSpec((1, N_EXPERTS), lambda i: (0, 0))],
        out_shape=[jax.ShapeDtypeStruct((n, LANES), I32), jax.ShapeDtypeStruct((n, LANES), F32),
                   jax.ShapeDtypeStruct((n, LANES), I32), jax.ShapeDtypeStruct((1, N_EXPERTS), F32)],
        compiler_params=_cparams(("arbitrary",)),
        name="moe_router",
    )(h, w, bias.reshape(1, N_EXPERTS), tri)


GU_BLK = 2 * LANES


def _gu_prep_kernel(w_ref, p_ref, o_ref):
    for c in range(w_ref.shape[-1] // GU_BLK):
        cols = slice(c * GU_BLK, (c + 1) * GU_BLK)
        o_ref[0, :, cols] = _dot(w_ref[0, :, cols].astype(BF16), p_ref[...]).astype(BF16)


def _gu_prep(w_gu, *, tk=4096):
    e, d, f2 = w_gu.shape
    assert f2 % GU_BLK == 0
    k = np.arange(GU_BLK)
    perm = np.zeros((GU_BLK, GU_BLK), np.float32)
    perm[k, (k % 2) * LANES + k // 2] = 1.0
    return pl.pallas_call(
        _gu_prep_kernel,
        grid=(e, d // tk),
        in_specs=[pl.BlockSpec((1, tk, f2), lambda ei, ki: (ei, ki, 0)),
                  pl.BlockSpec((GU_BLK, GU_BLK), lambda ei, ki: (0, 0))],
        out_specs=pl.BlockSpec((1, tk, f2), lambda ei, ki: (ei, ki, 0)),
        out_shape=jax.ShapeDtypeStruct((e, d, f2), BF16),
        compiler_params=_cparams(("parallel", "parallel")),
        name="moe_gu_prep",
    )(w_gu, jnp.asarray(perm, BF16))


MOE_TM = 512


def _expert_kernel(te_ref, x_ref, wgu_ref, bg_ref, bl_ref, wd_ref, bd_ref, o_ref):
    hgu = _dot(x_ref[...], wgu_ref[0])
    nblk = hgu.shape[1] // GU_BLK
    glu = jnp.concatenate([hgu[:, c * GU_BLK:c * GU_BLK + LANES] for c in range(nblk)], axis=1)
    lin = jnp.concatenate([hgu[:, c * GU_BLK + LANES:(c + 1) * GU_BLK] for c in range(nblk)], axis=1)
    glu = jnp.minimum(glu + bg_ref[0], SWIGLU_LIMIT)
    lin = jnp.clip(lin + bl_ref[0], -SWIGLU_LIMIT, SWIGLU_LIMIT)
    act = glu * jax.nn.sigmoid(SWIGLU_ALPHA * glu) * (lin + 1.0)
    o_ref[...] = _dot(act.astype(BF16), wd_ref[0]) + bd_ref[0]


def _experts(tile_expert, xs, w_gu, b_glu, b_lin, w_down, b_down):
    p, d = xs.shape
    tm = MOE_TM
    ff = w_down.shape[1]
    by_expert = lambda shape: pl.BlockSpec((1,) + shape, lambda i, te: (te[i], 0, 0))
    grid_spec = pltpu.PrefetchScalarGridSpec(
        num_scalar_prefetch=1,
        grid=(p // tm,),
        in_specs=[pl.BlockSpec((tm, d), lambda i, te: (i, 0)),
                  by_expert((d, 2 * ff)), by_expert((1, ff)), by_expert((1, ff)),
                  by_expert((ff, d)), by_expert((1, d))],
        out_specs=pl.BlockSpec((tm, d), lambda i, te: (i, 0)),
    )
    return pl.pallas_call(
        _expert_kernel,
        grid_spec=grid_spec,
        out_shape=jax.ShapeDtypeStruct((p, d), F32),
        compiler_params=_cparams(("arbitrary",)),
        name="moe_experts",
    )(tile_expert, xs, w_gu, b_glu, b_lin, w_down, b_down)


def _moe(h_f32, h_bf16, w_router, b_router, layer, w_gu, b_glu, b_lin, w_down, b_down):
    n, d = h_f32.shape
    tm = MOE_TM
    idx128, wt128, rank128, cnt = _router(h_f32, w_router, b_router)
    expert = idx128[:, :TOP_K]
    na = n * TOP_K
    p = na + N_EXPERTS * tm
    counts = cnt[0].astype(I32)
    padded = ((counts + tm - 1) // tm) * tm
    pend = jnp.cumsum(padded)
    pstart = pend - padded
    first = jnp.sum(jnp.where(expert[..., None] == jnp.arange(N_EXPERTS, dtype=I32), pstart, 0), axis=-1)
    slot = first + rank128[:, :TOP_K]
    token = jnp.arange(na, dtype=I32) // TOP_K
    row_token = jnp.zeros((p,), I32).at[slot.reshape(-1)].set(
        token, mode="promise_in_bounds", unique_indices=True)
    tile_start = jnp.arange(p // tm, dtype=I32) * tm
    tile_expert = jnp.minimum(jnp.sum((tile_start[:, None] >= pend[None, :]).astype(I32), axis=1), N_EXPERTS - 1)
    xs = jnp.take(h_bf16, row_token, axis=0, mode="clip")
    ys = _experts(tile_expert + layer * N_EXPERTS, xs, w_gu, b_glu, b_lin, w_down, b_down)
    return [ys.at[slot[:, k]].get(mode="promise_in_bounds") for k in range(TOP_K)], wt128


def kernel(x, ln1_g, ln1_b, ln2_g, ln2_b, ev_w_in, ev_w_out, hg_lb_raw, hg_norm_g, od_w_in, od_w_out,
           cmp_k_w1, cmp_k_w2, cmp_k_pos, cmp_v_w1, cmp_v_w2, cmp_v_pos, rel_bias, router_w, router_b,
           exp_w_gu, exp_b_gu, exp_w_down, exp_b_down):
    b, t, d = x.shape
    n = b * t
    depth = ln1_g.shape[0]
    alpha = (2 * depth) ** 0.25
    lb_soft = jax.nn.softmax(hg_lb_raw.astype(F32), axis=0)
    lower_bounds = jnp.cumsum(lb_soft, axis=0) - lb_soft[0]
    pb, bnear, bfar, bpat = _nsa_bias_tables(rel_bias, t)

    n_all = depth * N_EXPERTS
    ff = exp_w_down.shape[2]
    wgu_all = _gu_prep(exp_w_gu.reshape(n_all, d, 2 * ff))
    wd_all = exp_w_down.astype(BF16).reshape(n_all, ff, d)
    bg_all = exp_b_gu[..., 0::2].reshape(n_all, 1, ff)
    bl_all = exp_b_gu[..., 1::2].reshape(n_all, 1, ff)
    bd_all = exp_b_down.reshape(n_all, 1, d)

    h = x.reshape(n, d)
    hb = h.astype(BF16)
    for layer in range(depth):
        if layer % 2 == 0:
            e = layer // 2
            proj = _matmul(hb, ev_w_in, e)
            o_a = _sb_attention(proj, b, t)
            o_b = _hgrn2(proj, lower_bounds[layer], hg_norm_g[e], b, t)
            mix = _matmul2(o_a, o_b, ev_w_out, e)
        else:
            o = layer // 2
            proj = _matmul(hb, od_w_in, o, n=NSA_WIDTH + 6 * NSA_KV_WIDTH)
            gate_logits = _matmul(hb, od_w_in[o, :, NSA_WIDTH + 6 * NSA_KV_WIDTH:])
            proj3 = proj.reshape(b, t, -1)
            c0 = NSA_WIDTH // HEAD_DIM
            k_cmp = _compress(proj3, c0, cmp_k_w1[o], cmp_k_w2[o], cmp_k_pos[o])
            v_cmp = _compress(proj3, c0 + NSA_KV_GROUPS, cmp_v_w1[o], cmp_v_w2[o], cmp_v_pos[o])
            o_c, sel_t = _cmp_select(proj3, k_cmp, v_cmp, pb, b, t)
            sel = jnp.swapaxes(sel_t, 2, 3)
            o_s = _nsa_selected(proj3, c0 + 2 * NSA_KV_GROUPS, c0 + 3 * NSA_KV_GROUPS, bnear, bfar, sel, b, t)
            o_w = _nsa_window(proj3, c0 + 4 * NSA_KV_GROUPS, c0 + 5 * NSA_KV_GROUPS, bpat, b, t)
            mixed = _gate_mix(gate_logits, o_c.reshape(n, -1), o_s.reshape(n, -1), o_w.reshape(n, -1))
            mix = _matmul(mixed, od_w_out, o)
        h, hb = _add_ln(h, mix, ln1_g[layer], ln1_b[layer], alpha)
        ys, gate_w = _moe(h, hb, router_w[layer], router_b[layer], layer,
                          wgu_all, bg_all, bl_all, wd_all, bd_all)
        h, hb = _combine_ln(h, ys, gate_w, ln2_g[layer], ln2_b[layer], alpha)
    return h.reshape(b, t, d)
```

```python
import functools
import math

import jax
import jax.numpy as jnp
import numpy as np
from jax import lax
from jax.experimental import pallas as pl
from jax.experimental.pallas import tpu as pltpu

F32 = jnp.float32
BF16 = jnp.bfloat16
I32 = jnp.int32

HEAD_DIM = 128
SB_HEADS = 16
HG_HEADS = 16
SB_WIDTH = SB_HEADS * HEAD_DIM
HG_WIDTH = HG_HEADS * HEAD_DIM
F_MIN = 1e-6
NSA_HEADS = 32
NSA_KV_GROUPS = 4
NSA_HPG = NSA_HEADS // NSA_KV_GROUPS
NSA_WIDTH = NSA_HEADS * HEAD_DIM
NSA_KV_WIDTH = NSA_KV_GROUPS * HEAD_DIM
CMP_LEN = 32
CMP_STRIDE = 16
SLC_BLOCK = 64
N_SELECT = 16
WINDOW = 512
REL_BUCKETS = 32
REL_MAX_DIST = 128
N_EXPERTS = 32
TOP_K = 4
EXPERT_FF = 384
SWIGLU_LIMIT = 7.0
SWIGLU_ALPHA = 1.702
LN_EPS = 1e-5
NEG_INF = -1e30
FORCED = 1e9

LANES = 128
VMEM_LIMIT = 56 * 1024 * 1024

HI = lax.Precision.HIGHEST
LOG2E = 1.4426950408889634


def _cparams(sem):
    return pltpu.CompilerParams(dimension_semantics=sem, vmem_limit_bytes=VMEM_LIMIT)


def _dot_nt(a, b, **kw):
    return lax.dot_general(a, b, (((1,), (1,)), ((), ())), preferred_element_type=F32, **kw)


def _dot(a, b, **kw):
    return jnp.dot(a, b, preferred_element_type=F32, **kw)


def _mm_kernel(a_ref, w_ref, o_ref):
    o_ref[...] = _dot(a_ref[...], w_ref[...].astype(BF16)).astype(o_ref.dtype)


def _matmul(a, w, layer=None, *, n=None, tm=1024, tn=512, out_dtype=F32):
    m, k = a.shape
    n = w.shape[-1] if n is None else n
    tn = min(tn, n)
    assert m % tm == 0 and n % tn == 0
    if w.ndim == 3:
        w_spec = pl.BlockSpec((None, k, tn), lambda i, j: (layer, 0, j))
    else:
        w_spec = pl.BlockSpec((k, tn), lambda i, j: (0, j))
    return pl.pallas_call(
        _mm_kernel,
        grid=(m // tm, n // tn),
        in_specs=[pl.BlockSpec((tm, k), lambda i, j: (i, 0)), w_spec],
        out_specs=pl.BlockSpec((tm, tn), lambda i, j: (i, j)),
        out_shape=jax.ShapeDtypeStruct((m, n), out_dtype),
        compiler_params=_cparams(("parallel", "arbitrary")),
        name="matmul",
    )(a, w)


def _mm2_kernel(a1_ref, a2_ref, w1_ref, w2_ref, o_ref):
    o_ref[...] = (_dot(a1_ref[...], w1_ref[...].astype(BF16))
                  + _dot(a2_ref[...], w2_ref[...].astype(BF16)))


def _matmul2(a1, a2, w, layer, *, tm=1024, tn=512):
    m, k1 = a1.shape
    k2 = a2.shape[1]
    n = w.shape[-1]
    assert k1 == k2 and w.shape[1] == k1 + k2
    return pl.pallas_call(
        _mm2_kernel,
        grid=(m // tm, n // tn),
        in_specs=[pl.BlockSpec((tm, k1), lambda i, j: (i, 0)),
                  pl.BlockSpec((tm, k2), lambda i, j: (i, 0)),
                  pl.BlockSpec((None, k1, tn), lambda i, j: (layer, 0, j)),
                  pl.BlockSpec((None, k2, tn), lambda i, j: (layer, 1, j))],
        out_specs=pl.BlockSpec((tm, tn), lambda i, j: (i, j)),
        out_shape=jax.ShapeDtypeStruct((m, n), F32),
        compiler_params=_cparams(("parallel", "arbitrary")),
        name="matmul2",
    )(a1, a2, w, w)


def _ln_store(x, g_ref, b_ref, of_ref, ob_ref):
    mu = jnp.mean(x, axis=-1, keepdims=True)
    xc = x - mu
    var = jnp.mean(xc * xc, axis=-1, keepdims=True)
    out = xc * lax.rsqrt(var + LN_EPS) * g_ref[...] + b_ref[...]
    of_ref[...] = out
    ob_ref[...] = out.astype(BF16)


def _add_ln_kernel(h_ref, y_ref, g_ref, b_ref, of_ref, ob_ref, *, alpha):
    _ln_store(alpha * h_ref[...] + y_ref[...], g_ref, b_ref, of_ref, ob_ref)


def _combine_ln_kernel(h_ref, *refs, alpha):
    ys, (w_ref, g_ref, b_ref, of_ref, ob_ref) = refs[:TOP_K], refs[TOP_K:]
    w = w_ref[...]
    x = alpha * h_ref[...]
    for k in range(TOP_K):
        x = x + w[:, k:k + 1] * ys[k][...].astype(F32)
    _ln_store(x, g_ref, b_ref, of_ref, ob_ref)


def _combine_ln(h, ys, w128, g, b, alpha, *, tm=256):
    n, d = h.shape
    row = pl.BlockSpec((tm, d), lambda i: (i, 0))
    vec = pl.BlockSpec((1, d), lambda i: (0, 0))
    return pl.pallas_call(
        functools.partial(_combine_ln_kernel, alpha=alpha),
        grid=(n // tm,),
        in_specs=[row] * (1 + TOP_K) + [pl.BlockSpec((tm, LANES), lambda i: (i, 0)), vec, vec],
        out_specs=[row, row],
        out_shape=[jax.ShapeDtypeStruct((n, d), F32), jax.ShapeDtypeStruct((n, d), BF16)],
        compiler_params=_cparams(("parallel",)),
        name="moe_combine_ln",
    )(h, *ys, w128, g.reshape(1, d), b.reshape(1, d))


def _add_ln(h, y, g, b, alpha, *, tm=256):
    n, d = h.shape
    row = pl.BlockSpec((tm, d), lambda i: (i, 0))
    vec = pl.BlockSpec((1, d), lambda i: (0, 0))
    return pl.pallas_call(
        functools.partial(_add_ln_kernel, alpha=alpha),
        grid=(n // tm,),
        in_specs=[row, row, vec, vec],
        out_specs=[row, row],
        out_shape=[jax.ShapeDtypeStruct((n, d), F32), jax.ShapeDtypeStruct((n, d), BF16)],
        compiler_params=_cparams(("parallel",)),
        name="add_ln",
    )(h, y, g.reshape(1, d), b.reshape(1, d))


SB_TQ = 512
SB_KB = LANES
SB_DEAD = -104.0


def _sb_kernel(q_ref, k_ref, v_ref, u_ref, o_ref, kb_ref, vb_ref, acc_ref, carry_ref, *, tq, scale):
    i = pl.program_id(2)
    kb = SB_KB
    nb = tq // kb

    @pl.when(i == 0)
    def _():
        kb_ref[...] = k_ref[...].astype(BF16)
        vb_ref[...] = v_ref[...].astype(BF16)

    q = q_ref[...].astype(BF16)
    u = u_ref[...]
    acc_ref[...] = jnp.zeros_like(acc_ref)
    carry_ref[...] = jnp.zeros_like(carry_ref)

    def block_terms(z, causal=None):
        sp = jnp.log(1.0 + jnp.exp(-jnp.abs(z)))
        ls = jnp.minimum(z, 0.0) - sp
        lk = ls - z
        if causal is not None:
            lk = jnp.where(causal, lk, 0.0)
        hi = lk.astype(BF16)
        lo = (lk - hi.astype(F32)).astype(BF16)
        rt = _dot(jnp.concatenate([hi, lo], axis=1), u)
        return ls, rt[:, :kb] - lk, rt[:, kb:]

    base = pl.multiple_of(i * tq, tq)
    for j in reversed(range(nb)):
        r0 = j * kb
        rows = tq - r0
        kj = kb_ref[pl.ds(base + r0, kb), :]
        vj = vb_ref[pl.ds(base + r0, kb), :]
        z = _dot_nt(q[r0:], kj) * scale
        causal = lax.broadcasted_iota(I32, (rows, kb), 1) < lax.broadcasted_iota(I32, (rows, kb), 0)
        ls, between, tot = block_terms(z, causal)
        carry = carry_ref[r0:, :]
        w = jnp.where(causal, jnp.exp(ls + between + carry), 0.0)
        acc_ref[r0:, :] += _dot(w.astype(BF16), vj)
        carry_ref[r0:, :] = carry + tot

    def far(state):
        c, _ = state
        start = pl.multiple_of(c * tq, tq)
        z = _dot_nt(q, kb_ref[pl.ds(start, tq), :]) * scale
        terms = [block_terms(z[:, j * kb:(j + 1) * kb]) for j in range(nb)]
        carry = carry_ref[...]
        ws = [None] * nb
        for j in reversed(range(nb)):
            ls, between, tot = terms[j]
            ws[j] = jnp.exp(ls + between + carry).astype(BF16)
            carry = carry + tot
        acc_ref[...] += _dot(jnp.concatenate(ws, axis=1), vb_ref[pl.ds(start, tq), :])
        carry_ref[...] = carry
        return c - 1, jnp.max(carry)

    lax.while_loop(lambda st: (st[0] >= 0) & (st[1] > SB_DEAD), far, (i - 1, jnp.max(carry_ref[...])))
    o_ref[...] = acc_ref[...].astype(o_ref.dtype)


def _sb_attention(proj, b, t):
    tq, kb = min(SB_TQ, t), SB_KB
    nq = t // tq
    r = np.arange(2 * kb)[:, None] % kb
    c = np.arange(2 * kb)[None, :]
    u = jnp.asarray(np.where(c < kb, r >= c, True), BF16)
    kv_spec = lambda off: pl.BlockSpec((t, HEAD_DIM), lambda bi, h, i: (bi, off + h))
    return pl.pallas_call(
        functools.partial(_sb_kernel, tq=tq, scale=HEAD_DIM ** -0.5),
        grid=(b, SB_HEADS, nq),
        in_specs=[pl.BlockSpec((tq, HEAD_DIM), lambda bi, h, i: (bi * nq + i, h)),
                  kv_spec(SB_HEADS), kv_spec(2 * SB_HEADS),
                  pl.BlockSpec((2 * kb, 2 * kb), lambda bi, h, i: (0, 0))],
        out_specs=pl.BlockSpec((tq, HEAD_DIM), lambda bi, h, i: (bi * nq + i, h)),
        out_shape=jax.ShapeDtypeStruct((b * t, SB_WIDTH), BF16),
        scratch_shapes=[pltpu.VMEM((t, HEAD_DIM), BF16), pltpu.VMEM((t, HEAD_DIM), BF16),
                        pltpu.VMEM((tq, HEAD_DIM), F32), pltpu.VMEM((tq, kb), F32)],
        compiler_params=_cparams(("parallel", "parallel", "arbitrary")),
        name="sb_attention",
    )(proj, proj, proj, u)


HG_C = 64
HG_TB = 512
HG_NH = 8


def _hg_tables(c):
    levels = []
    m = c // 2
    while m >= 1:
        levels.append(m)
        m //= 2
    nl = len(levels)
    e = np.zeros((nl + 2, c, c), np.float32)
    mask = np.zeros((nl + 1, c, c), np.float32)
    p = np.arange(c)
    for li, m in enumerate(levels):
        blk = p // (2 * m)
        half = (p // m) % 2
        mid = blk * 2 * m + m - 1
        for r in range(c):
            if half[r] == 1:
                e[li, r, mid[r] + 1:r + 1] = 1.0
            else:
                e[li, r, r + 1:mid[r] + 1] = 1.0
        mask[li] = ((half[:, None] == 1) & (half[None, :] == 0) & (blk[:, None] == blk[None, :]))
    mask[nl] = np.eye(c)
    e[nl] = np.tril(np.ones((c, c)))
    e[nl + 1] = np.triu(np.ones((c, c)), 1)
    return e.reshape((nl + 2) * c, c), mask, nl


def _hg_kernel(qh_ref, fh_ref, ih_ref, gh_ref, lb_ref, ng_ref, e_ref, mask_ref, o_ref, st_ref,
               *, c, tb, nl, nh):
    @pl.when(pl.program_id(2) == 0)
    def _():
        st_ref[...] = jnp.zeros_like(st_ref)

    ng = ng_ref[...]
    emat = e_ref[...]

    def one_head(rows, hh):
        cols = slice(hh * HEAD_DIM, (hh + 1) * HEAD_DIM)
        lb = lb_ref[:, cols]
        qh = qh_ref[rows, cols]
        f = lb + (1.0 - lb) * jax.nn.sigmoid(fh_ref[rows, cols])
        g = jnp.log(jnp.maximum(f, F_MIN))
        kk = 1.0 - f
        q = qh * jax.nn.sigmoid(qh)
        v = ih_ref[rows, cols]
        vb = v.astype(BF16)
        g1 = g.astype(BF16)
        r1 = g - g1.astype(F32)
        g2 = r1.astype(BF16)
        g3 = (r1 - g2.astype(F32)).astype(BF16)
        ex = jnp.exp(_dot(emat, jnp.concatenate([g1, g2, g3], axis=0)))
        scores = mask_ref[nl] * _dot_nt(q.astype(BF16), kk.astype(BF16))
        for li in range(nl):
            a = ex[li * c:(li + 1) * c]
            scores = scores + mask_ref[li] * _dot_nt((q * a).astype(BF16), (kk * a).astype(BF16))
        ecum = ex[nl * c:(nl + 1) * c]
        erest = ex[(nl + 1) * c:(nl + 2) * c]
        st = st_ref[hh]
        o = _dot(scores.astype(BF16), vb) + _dot_nt((q * ecum).astype(BF16), st.astype(BF16))
        st_ref[hh] = ecum[c - 1:c, :] * st + _dot(v.T.astype(BF16), (kk * erest).astype(BF16))
        gh = gh_ref[rows, cols]
        o = o * lax.rsqrt(jnp.mean(o * o, axis=-1, keepdims=True) + LN_EPS) * ng
        o_ref[rows, cols] = (o * (gh * jax.nn.sigmoid(gh))).astype(o_ref.dtype)

    def chunk(ci, carry):
        rows = pl.ds(pl.multiple_of(ci * c, c), c)
        for hh in range(nh):
            one_head(rows, hh)
        return carry

    lax.fori_loop(0, tb // c, chunk, 0)


def _hgrn2(proj, lb, norm_g, b, t):
    c, tb, nh = HG_C, min(HG_TB, t), HG_NH
    e, mask, nl = _hg_tables(c)
    e3 = np.concatenate([e, e, e], axis=1)
    nt = t // tb
    base = 3 * SB_HEADS
    assert base % nh == 0 and HG_HEADS % nh == 0
    wide = nh * HEAD_DIM
    col = lambda k: pl.BlockSpec((tb, wide), lambda bi, h, i: (bi * nt + i, (base + k * HG_HEADS) // nh + h))
    const2 = lambda a: pl.BlockSpec(a.shape, lambda bi, h, i: (0,) * a.ndim)
    return pl.pallas_call(
        functools.partial(_hg_kernel, c=c, tb=tb, nl=nl, nh=nh),
        grid=(b, HG_HEADS // nh, nt),
        in_specs=[col(0), col(1), col(2), col(3),
                  pl.BlockSpec((1, wide), lambda bi, h, i: (0, h)),
                  pl.BlockSpec((1, HEAD_DIM), lambda bi, h, i: (0, 0)),
                  const2(e3), const2(mask)],
        out_specs=pl.BlockSpec((tb, wide), lambda bi, h, i: (bi * nt + i, h)),
        out_shape=jax.ShapeDtypeStruct((b * t, HG_WIDTH), BF16),
        scratch_shapes=[pltpu.VMEM((nh, HEAD_DIM, HEAD_DIM), F32)],
        compiler_params=_cparams(("parallel", "parallel", "arbitrary")),
        name="hgrn2",
    )(proj, proj, proj, proj, lb.reshape(1, HG_WIDTH), norm_g.reshape(1, HEAD_DIM),
      jnp.asarray(e3, BF16), jnp.asarray(mask))


def _cmp_kernel(a_ref, w1_ref, w2_ref, pos_ref, o_ref, *, ncp):
    half = CMP_LEN // 2
    p = jnp.zeros((ncp, HEAD_DIM), F32)
    q = jnp.zeros((ncp, HEAD_DIM), F32)
    for j in range(half):
        s = a_ref[pl.ds(j, ncp, stride=CMP_STRIDE), :]
        p = p + _dot((s + pos_ref[j:j + 1, :]).astype(BF16), w1_ref[j].astype(BF16))
        q = q + _dot((s + pos_ref[half + j:half + j + 1, :]).astype(BF16), w1_ref[half + j].astype(BF16))
    hid = p + pltpu.roll(q, ncp - 1, 0)
    hid = hid * jax.nn.sigmoid(hid)
    out = _dot(hid.astype(BF16), w2_ref[...].astype(BF16))
    row = lax.broadcasted_iota(I32, (ncp, HEAD_DIM), 0)
    o_ref[0, 0] = jnp.where(row < ncp - 1, out, 0.0)


def _compress(proj3, col0, w1, w2, pos):
    b, t, _ = proj3.shape
    ncp = t // CMP_STRIDE
    assert CMP_LEN == 2 * CMP_STRIDE
    return pl.pallas_call(
        functools.partial(_cmp_kernel, ncp=ncp),
        grid=(b, NSA_KV_GROUPS),
        in_specs=[pl.BlockSpec((None, t, HEAD_DIM), lambda bi, g: (bi, 0, col0 + g)),
                  pl.BlockSpec((CMP_LEN, HEAD_DIM, HEAD_DIM), lambda bi, g: (0, 0, 0)),
                  pl.BlockSpec((HEAD_DIM, HEAD_DIM), lambda bi, g: (0, 0)),
                  pl.BlockSpec((CMP_LEN, HEAD_DIM), lambda bi, g: (0, 0))],
        out_specs=pl.BlockSpec((1, 1, ncp, HEAD_DIM), lambda bi, g: (bi, g, 0, 0)),
        out_shape=jax.ShapeDtypeStruct((b, NSA_KV_GROUPS, ncp, HEAD_DIM), F32),
        compiler_params=_cparams(("parallel", "parallel")),
        name="nsa_compress",
    )(proj3, w1.reshape(CMP_LEN, HEAD_DIM, HEAD_DIM), w2, pos)


NSA_TQ = 128
CMP_NEAR = 16
CMP_NEAR_LO = 9


def _rel_bucket_np(dist):
    dist = np.maximum(dist, 0)
    max_exact = REL_BUCKETS // 2
    ratio = (np.log(np.maximum(dist, max_exact).astype(np.float32) / np.float32(max_exact))
             / np.float32(math.log(REL_MAX_DIST / max_exact)))
    large = np.minimum(max_exact + (ratio * np.float32(REL_BUCKETS - max_exact)).astype(np.int32),
                       REL_BUCKETS - 1)
    return np.where(dist < max_exact, dist, large).astype(np.int32)


def _cmpsel_kernel(q_ref, kc_ref, vc_ref, pb_ref, ovl_ref, oc_ref, sel_ref,
                   *, tq, ncp, nslc, nsel, scale):
    i = pl.program_id(2)
    kc = kc_ref[0, 0].astype(BF16)
    vc = vc_ref[0, 0].astype(BF16)
    qpos = i * tq + lax.broadcasted_iota(I32, (tq, ncp), 0)
    ncol = lax.broadcasted_iota(I32, (tq, ncp), 1)
    valid = (ncol * CMP_STRIDE + (CMP_LEN - 1) <= qpos) & (ncol < ncp - 1)
    sr = lax.broadcasted_iota(I32, (LANES, ncp), 0)
    sc = lax.broadcasted_iota(I32, (LANES, ncp), 1)
    first = i * (tq // CMP_STRIDE) - CMP_NEAR_LO
    place = jnp.where(sr < CMP_NEAR, jnp.where(sc == first + sr, 1.0, 0.0),
                      jnp.where(sr == CMP_NEAR, jnp.where(sc < first, 1.0, 0.0), 0.0))
    place = place.astype(BF16)
    bias = _dot(pb_ref[0], jnp.concatenate([place, place], axis=0))
    psum = jnp.zeros((tq, ncp), F32)
    for r in range(NSA_HPG):
        hs = slice(r * HEAD_DIM, (r + 1) * HEAD_DIM)
        s = _dot_nt(q_ref[0, :, hs].astype(BF16), kc) * scale + bias[r * tq:(r + 1) * tq]
        s = jnp.where(valid, s, NEG_INF)
        mx = jnp.max(s, axis=-1, keepdims=True)
        e = jnp.where(valid, jnp.exp(s - mx), 0.0)
        den = jnp.sum(e, axis=-1, keepdims=True)
        p = e / jnp.maximum(den, 1e-30)
        oc_ref[0, :, hs] = _dot(p.astype(BF16), vc)
        psum = psum + p
    imp = _dot_nt(ovl_ref[...], psum, precision=HI)
    jrow = lax.broadcasted_iota(I32, (nslc, tq), 0)
    qblk = (i * tq + lax.broadcasted_iota(I32, (nslc, tq), 1)) // SLC_BLOCK
    ok = jrow <= qblk
    forced = (jrow == 0) | (jrow == qblk) | (jrow == qblk - 1)
    imp = jnp.where(ok, jnp.where(forced, FORCED, imp), NEG_INF)
    rank = jnp.zeros((nslc, tq), F32)
    for j2 in range(nslc):
        row = imp[j2:j2 + 1, :]
        tie = jnp.where(jrow > j2, 1.0, 0.0)
        rank = rank + jnp.where(row > imp, 1.0, jnp.where(row == imp, tie, 0.0))
    sel_ref[0, 0] = jnp.where((rank < nsel) & ok, 1.0, 0.0)


def _cmp_select(proj3, k_cmp, v_cmp, pb, b, t):
    tq = NSA_TQ
    ncp = t // CMP_STRIDE
    nslc = t // SLC_BLOCK
    nsel = min(N_SELECT, nslc)
    n_idx = np.arange(ncp)
    slc_start = np.arange(nslc) * SLC_BLOCK
    cs = n_idx * CMP_STRIDE
    ovl = ((cs[None, :] < slc_start[:, None] + SLC_BLOCK)
           & (cs[None, :] + CMP_LEN - 1 >= slc_start[:, None])
           & (n_idx[None, :] < ncp - 1)).astype(np.float32)
    gw = NSA_HPG * HEAD_DIM
    return pl.pallas_call(
        functools.partial(_cmpsel_kernel, tq=tq, ncp=ncp, nslc=nslc, nsel=nsel, scale=HEAD_DIM ** -0.5),
        grid=(b, NSA_KV_GROUPS, t // tq),
        in_specs=[pl.BlockSpec((1, tq, gw), lambda bi, g, i: (bi, i, g)),
                  pl.BlockSpec((1, 1, ncp, HEAD_DIM), lambda bi, g, i: (bi, g, 0, 0)),
                  pl.BlockSpec((1, 1, ncp, HEAD_DIM), lambda bi, g, i: (bi, g, 0, 0)),
                  pl.BlockSpec((1, NSA_HPG * tq, 2 * LANES), lambda bi, g, i: (g, 0, 0)),
                  pl.BlockSpec((nslc, ncp), lambda bi, g, i: (0, 0))],
        out_specs=[pl.BlockSpec((1, tq, gw), lambda bi, g, i: (bi, i, g)),
                   pl.BlockSpec((1, 1, nslc, tq), lambda bi, g, i: (bi, g, 0, i))],
        out_shape=[jax.ShapeDtypeStruct((b, t, NSA_WIDTH), F32),
                   jax.ShapeDtypeStruct((b, NSA_KV_GROUPS, nslc, t), F32)],
        compiler_params=_cparams(("parallel", "parallel", "parallel")),
        name="nsa_cmp_select",
    )(proj3, k_cmp, v_cmp, pb, jnp.asarray(ovl))


SEL_CHUNK = 512
SEL_VEXT = 16
MASKED = -1e30
FAR_BIAS_COL = SLC_BLOCK


def _softmax_step(st, vt, m_ref, acc_ref, c1):
    m_old = m_ref[...]
    m_new = jnp.maximum(m_old, jnp.max(st, axis=0, keepdims=True))
    p = jnp.exp2((st - m_new) * c1)
    acc_ref[...] = jnp.exp2((m_old - m_new) * c1) * acc_ref[...] + _dot(vt, p.astype(BF16))
    m_ref[...] = m_new


def _sel_kernel(q_ref, k_ref, v_ref, kext_ref, vext_ref, bnear_ref, bfar_ref, sel_ref, o_ref,
                qa_ref, ka_ref, vt_ref, m_ref, acc_ref, s0_ref, s1_ref, *, tq, c1):
    i = pl.program_id(2)
    hpg = NSA_HPG
    d = HEAD_DIM

    @pl.when(i == 0)
    def _():
        ka_ref[:, :d] = k_ref[...].astype(BF16)
        ka_ref[:, d:] = kext_ref[...]
        for c in range(vt_ref.shape[0]):
            vt_ref[c, :d, :] = v_ref[c * SEL_CHUNK:(c + 1) * SEL_CHUNK, :].T.astype(BF16)
            vt_ref[c, d:, :] = vext_ref[...]

    sel = sel_ref[0, 0]
    nslc = sel.shape[1]
    blk = lax.broadcasted_iota(I32, (tq, nslc), 1)
    far_blocks = (i - 1) * (tq // SLC_BLOCK)
    maskq = jnp.where(blk < far_blocks, jnp.where(sel > 0.5, 0.0, MASKED), MASKED)
    ext = jnp.concatenate([maskq, jnp.zeros((tq, LANES - nslc), F32)], axis=1)
    for r in range(hpg):
        qa_ref[r * tq:(r + 1) * tq, :d] = q_ref[0, :, r * d:(r + 1) * d].astype(BF16)
        qa_ref[r * tq:(r + 1) * tq, d:] = (ext + bfar_ref[0, r:r + 1, :]).astype(BF16)
    m_ref[...] = jnp.full_like(m_ref, MASKED)
    acc_ref[...] = jnp.zeros_like(acc_ref)

    n_chunks = vt_ref.shape[0]

    def logits(c):
        start = pl.multiple_of(jnp.minimum(c, n_chunks - 1) * SEL_CHUNK, SEL_CHUNK)
        return _dot_nt(ka_ref[pl.ds(start, SEL_CHUNK), :], qa_ref[...])

    n_far = (jnp.maximum(i - 1, 0) * tq + SEL_CHUNK - 1) // SEL_CHUNK
    n_pairs = (n_far + 1) // 2

    @pl.when(n_pairs > 0)
    def _():
        s0_ref[...] = logits(0)

    def pair(j, carry):
        c0 = 2 * j
        s1_ref[...] = logits(c0 + 1)
        _softmax_step(s0_ref[...], vt_ref[c0], m_ref, acc_ref, c1)
        s0_ref[...] = logits(c0 + 2)
        _softmax_step(s1_ref[...], vt_ref[c0 + 1], m_ref, acc_ref, c1)
        return carry

    lax.fori_loop(0, n_pairs, pair, 0)

    p0 = pl.multiple_of(jnp.maximum(i - 1, 0) * tq, tq)
    p1 = pl.multiple_of(i * tq, tq)
    kn = jnp.concatenate([ka_ref[pl.ds(p0, tq), :d], ka_ref[pl.ds(p1, tq), :d]], axis=0)
    vn = jnp.concatenate([v_ref[pl.ds(p0, tq), :], v_ref[pl.ds(p1, tq), :]], axis=0)
    vnt = jnp.concatenate([vn.T.astype(BF16), vext_ref[:, :2 * tq]], axis=0)
    er = lax.broadcasted_iota(I32, (2 * tq, nslc), 1)
    ec = lax.broadcasted_iota(I32, (2 * tq, nslc), 0) // SLC_BLOCK
    expand = jnp.where(er == far_blocks + ec, 1.0, 0.0).astype(BF16)
    picked = _dot_nt(expand, sel.astype(BF16))
    kj = lax.broadcasted_iota(I32, (2 * tq, tq), 0)
    qi = lax.broadcasted_iota(I32, (2 * tq, tq), 1)
    keep = jnp.where(kj <= qi + tq, picked, 0.0)
    keep = jnp.concatenate([keep] * hpg, axis=1) > 0.5
    st = _dot_nt(kn, qa_ref[:, :d]) + bnear_ref[0]
    _softmax_step(jnp.where(keep, st, MASKED), vnt, m_ref, acc_ref, c1)

    acc = acc_ref[...]
    out = acc[:d, :] / acc[d:d + 1, :]
    for r in range(hpg):
        o_ref[0, :, r * d:(r + 1) * d] = out[:, r * tq:(r + 1) * tq].T


def _nsa_selected(proj3, kcol, vcol, bnear, bfar, sel, b, t):
    tq = NSA_TQ
    d = HEAD_DIM
    gw = NSA_HPG * d
    nslc = t // SLC_BLOCK
    assert nslc <= FAR_BIAS_COL and t % SEL_CHUNK == 0
    rows = NSA_HPG * tq
    pos = np.arange(t)
    kext = np.zeros((t, LANES), np.float32)
    kext[pos, pos // SLC_BLOCK] = 1.0
    kext[:, FAR_BIAS_COL:FAR_BIAS_COL + 2] = 1.0
    vext = np.zeros((SEL_VEXT, SEL_CHUNK), np.float32)
    vext[0, :] = 1.0
    kv = lambda col: pl.BlockSpec((None, t, d), lambda bi, g, i: (bi, 0, col + g))
    const = lambda shape: pl.BlockSpec(shape, lambda bi, g, i: (0,) * len(shape))
    return pl.pallas_call(
        functools.partial(_sel_kernel, tq=tq, c1=d ** -0.5 * LOG2E),
        grid=(b, NSA_KV_GROUPS, t // tq),
        in_specs=[pl.BlockSpec((1, tq, gw), lambda bi, g, i: (bi, i, g)),
                  kv(kcol), kv(vcol), const((t, LANES)), const((SEL_VEXT, SEL_CHUNK)),
                  pl.BlockSpec((1, 2 * tq, rows), lambda bi, g, i: (g, 0, 0)),
                  pl.BlockSpec((1, NSA_HPG, LANES), lambda bi, g, i: (g, 0, 0)),
                  pl.BlockSpec((1, 1, tq, nslc), lambda bi, g, i: (bi, g, i, 0))],
        out_specs=pl.BlockSpec((1, tq, gw), lambda bi, g, i: (bi, i, g)),
        out_shape=jax.ShapeDtypeStruct((b, t, NSA_WIDTH), F32),
        scratch_shapes=[pltpu.VMEM((rows, 2 * d), BF16),
                        pltpu.VMEM((t, 2 * d), BF16),
                        pltpu.VMEM((t // SEL_CHUNK, d + SEL_VEXT, SEL_CHUNK), BF16),
                        pltpu.VMEM((1, rows), F32),
                        pltpu.VMEM((d + SEL_VEXT, rows), F32),
                        pltpu.VMEM((SEL_CHUNK, rows), F32),
                        pltpu.VMEM((SEL_CHUNK, rows), F32)],
        compiler_params=_cparams(("parallel", "parallel", "arbitrary")),
        name="nsa_selected",
    )(proj3, proj3, proj3, jnp.asarray(kext, BF16), jnp.asarray(vext, BF16),
      jnp.swapaxes(bnear, 1, 2), bfar, sel)


def _win_kernel(q_ref, k_ref, v_ref, bpat_ref, o_ref, kb_ref, vb_ref, *, tq, window, c1):
    i = pl.program_id(2)
    hpg = NSA_HPG
    d = HEAD_DIM
    nt = window // tq + 1

    @pl.when(i == 0)
    def _():
        kb_ref[...] = k_ref[...].astype(BF16)
        vb_ref[...] = v_ref[...].astype(BF16)

    qs = jnp.concatenate([q_ref[0, :, r * d:(r + 1) * d] for r in range(hpg)], axis=0).astype(BF16)
    t0 = jnp.maximum(i - (nt - 1), 0)
    qi = lax.broadcasted_iota(I32, (tq, tq), 0)
    kj = lax.broadcasted_iota(I32, (tq, tq), 1)
    parts = []
    for kk in range(nt):
        dt = i - (t0 + kk)
        start = pl.multiple_of((t0 + kk) * tq, tq)
        z = _dot_nt(qs, kb_ref[pl.ds(start, tq), :]) + bpat_ref[0, jnp.clip(dt, 0, 2)]
        dist = dt * tq + qi - kj
        keep = jnp.where(dist >= 0, jnp.where(dist < window, 1.0, 0.0), 0.0) > 0.5
        parts.append(jnp.where(keep[None], z.reshape(hpg, tq, tq), MASKED))
    s = jnp.concatenate(parts, axis=-1)
    m = jnp.max(s, axis=-1, keepdims=True)
    p = jnp.exp2((s - m) * c1)
    den = jnp.sum(p, axis=-1, keepdims=True)
    vwin = vb_ref[pl.ds(pl.multiple_of(t0 * tq, tq), nt * tq), :]
    out = (_dot(p.reshape(hpg * tq, nt * tq).astype(BF16), vwin).reshape(hpg, tq, d) / den)
    for r in range(hpg):
        o_ref[0, :, r * d:(r + 1) * d] = out[r]


def _nsa_window(proj3, kcol, vcol, bpat, b, t):
    tq = NSA_TQ
    d = HEAD_DIM
    gw = NSA_HPG * d
    assert WINDOW % tq == 0 and t >= WINDOW + tq
    kv = lambda col: pl.BlockSpec((None, t, d), lambda bi, g, i: (bi, 0, col + g))
    return pl.pallas_call(
        functools.partial(_win_kernel, tq=tq, window=WINDOW, c1=d ** -0.5 * LOG2E),
        grid=(b, NSA_KV_GROUPS, t // tq),
        in_specs=[pl.BlockSpec((1, tq, gw), lambda bi, g, i: (bi, i, g)),
                  kv(kcol), kv(vcol),
                  pl.BlockSpec((1, 3, NSA_HPG * tq, tq), lambda bi, g, i: (g, 0, 0, 0))],
        out_specs=pl.BlockSpec((1, tq, gw), lambda bi, g, i: (bi, i, g)),
        out_shape=jax.ShapeDtypeStruct((b, t, NSA_WIDTH), F32),
        scratch_shapes=[pltpu.VMEM((t, d), BF16), pltpu.VMEM((t, d), BF16)],
        compiler_params=_cparams(("parallel", "parallel", "arbitrary")),
        name="nsa_window",
    )(proj3, proj3, proj3, bpat)


def _gate_mix_kernel(g_ref, oc_ref, os_ref, ow_ref, o_ref):
    gates = jax.nn.sigmoid(g_ref[...])
    for h in range(NSA_HEADS):
        hs = slice(h * HEAD_DIM, (h + 1) * HEAD_DIM)
        mix = (gates[:, 3 * h:3 * h + 1] * oc_ref[:, hs]
               + gates[:, 3 * h + 1:3 * h + 2] * os_ref[:, hs]
               + gates[:, 3 * h + 2:3 * h + 3] * ow_ref[:, hs])
        o_ref[:, hs] = mix.astype(o_ref.dtype)


def _gate_mix(gate_logits, o_c, o_s, o_w, *, tm=256):
    n = gate_logits.shape[0]
    row = pl.BlockSpec((tm, NSA_WIDTH), lambda i: (i, 0))
    return pl.pallas_call(
        _gate_mix_kernel,
        grid=(n // tm,),
        in_specs=[pl.BlockSpec((tm, 3 * NSA_HEADS), lambda i: (i, 0)), row, row, row],
        out_specs=row,
        out_shape=jax.ShapeDtypeStruct((n, NSA_WIDTH), BF16),
        compiler_params=_cparams(("parallel",)),
        name="nsa_gate_mix",
    )(gate_logits, o_c, o_s, o_w)


def _nsa_bias_tables(rel_bias, t):
    tq = NSA_TQ
    g, hpg = NSA_KV_GROUPS, NSA_HPG
    inv_scale = HEAD_DIM ** 0.5
    tab = rel_bias.astype(F32)
    last = REL_BUCKETS - 1
    far_from = int(np.nonzero(_rel_bucket_np(np.arange(4 * REL_MAX_DIST)) < last)[0].max()) + 1
    assert (_rel_bucket_np(np.arange(far_from, t + tq)) == last).all()

    def by_group(a):
        a = jnp.moveaxis(a, -1, 0)
        return a.reshape((g, hpg) + a.shape[1:])

    iq = np.arange(tq)[:, None]
    m = np.arange(CMP_NEAR)[None, :]
    dist_c = iq - CMP_STRIDE * (m - CMP_NEAR_LO) - (CMP_LEN - 1)
    assert dist_c[:, 0].min() >= far_from
    assert (iq - CMP_STRIDE * (CMP_NEAR - CMP_NEAR_LO) - (CMP_LEN - 1)).max() < 0
    pb = jnp.zeros((g, hpg, tq, LANES), F32)
    pb = pb.at[..., :CMP_NEAR].set(by_group(tab[_rel_bucket_np(dist_c)]))
    pb = pb.at[..., CMP_NEAR].set(by_group(tab[last])[..., None])
    pb = pb.reshape(g, hpg * tq, LANES)
    pb_hi = pb.astype(BF16)
    pb = jnp.concatenate([pb_hi, (pb - pb_hi.astype(F32)).astype(BF16)], axis=-1)
    assert 2 * tq - (tq - 1) >= far_from
    jk = np.arange(tq)[None, :]
    idx = np.stack([_rel_bucket_np(iq - jk), _rel_bucket_np(tq + iq - jk), np.full((tq, tq), last, np.int32)])
    pat = by_group(tab[idx]) * inv_scale
    bpat = jnp.transpose(pat, (0, 2, 1, 3, 4)).reshape(g, 3, hpg * tq, tq)
    bnear = jnp.concatenate([pat[:, :, 1], pat[:, :, 0]], axis=-1).reshape(g, hpg * tq, 2 * tq)
    far = by_group(tab[last]) * inv_scale
    far_hi = far.astype(BF16).astype(F32)
    bfar = jnp.zeros((g, hpg, LANES), F32)
    bfar = bfar.at[..., FAR_BIAS_COL].set(far_hi).at[..., FAR_BIAS_COL + 1].set(far - far_hi)
    return pb, bnear, bfar, bpat


def _router_kernel(h_ref, w_ref, b_ref, tri_ref, idx_ref, wt_ref, rank_ref, cnt_ref, *, tm):
    @pl.when(pl.program_id(0) == 0)
    def _():
        cnt_ref[...] = jnp.zeros_like(cnt_ref)

    logits = _dot(h_ref[...], w_ref[...], precision=HI) + b_ref[...]
    lane = lax.broadcasted_iota(I32, (tm, N_EXPERTS), 1).astype(F32)
    out_lane = lax.broadcasted_iota(I32, (tm, LANES), 1)
    work = logits
    idx_out = jnp.zeros((tm, LANES), F32)
    val_out = jnp.zeros((tm, LANES), F32)
    top = None
    den = jnp.zeros((tm, 1), F32)
    hots = []
    for k in range(TOP_K):
        mx = jnp.max(work, axis=-1, keepdims=True)
        idx = jnp.min(jnp.where(work == mx, lane, float(N_EXPERTS)), axis=-1, keepdims=True)
        if top is None:
            top = mx
        e = jnp.exp(mx - top)
        den = den + e
        idx_out = jnp.where(out_lane == k, idx, idx_out)
        val_out = jnp.where(out_lane == k, e, val_out)
        hots.append(jnp.where(lane == idx, 1.0, 0.0))
        work = jnp.where(lane == idx, -jnp.inf, work)
    idx_ref[...] = idx_out.astype(I32)
    wt_ref[...] = val_out / den
    hot = hots[0] + hots[1] + hots[2] + hots[3]
    before = cnt_ref[...] + _dot(tri_ref[...], hot.astype(BF16))
    rank_out = jnp.zeros((tm, LANES), F32)
    for k in range(TOP_K):
        rank_out = jnp.where(out_lane == k, jnp.sum(hots[k] * before, axis=-1, keepdims=True), rank_out)
    rank_ref[...] = rank_out.astype(I32)
    cnt_ref[...] += jnp.sum(hot, axis=0, keepdims=True)


def _router(h, w, bias, *, tm=512):
    n, d = h.shape
    out = pl.BlockSpec((tm, LANES), lambda i: (i, 0))
    tri = jnp.asarray(np.tril(np.ones((tm, tm), np.float32), -1), BF16)
    return pl.pallas_call(
        functools.partial(_router_kernel, tm=tm),
        grid=(n // tm,),
        in_specs=[pl.BlockSpec((tm, d), lambda i: (i, 0)),
                  pl.BlockSpec((d, N_EXPERTS), lambda i: (0, 0)),
                  pl.BlockSpec((1, N_EXPERTS), lambda i: (0, 0)),
                  pl.BlockSpec((tm, tm), lambda i: (0, 0))],
        out_specs=[out, out, out, pl.BlockSpec((1, N_EXPERTS), lambda i: (0, 0))],
        out_shape=[jax.ShapeDtypeStruct((n, LANES), I32), jax.ShapeDtypeStruct((n, LANES), F32),
                   jax.ShapeDtypeStruct((n, LANES), I32), jax.ShapeDtypeStruct((1, N_EXPERTS), F32)],
        compiler_params=_cparams(("arbitrary",)),
        name="moe_router",
    )(h, w, bias.reshape(1, N_EXPERTS), tri)


GU_BLK = 2 * LANES


def _gu_prep_kernel(w_ref, p_ref, o_ref):
    for c in range(w_ref.shape[-1] // GU_BLK):
        cols = slice(c * GU_BLK, (c + 1) * GU_BLK)
        o_ref[0, :, cols] = _dot(w_ref[0, :, cols].astype(BF16), p_ref[...]).astype(BF16)


def _gu_prep(w_gu, *, tk=4096):
    e, d, f2 = w_gu.shape
    assert f2 % GU_BLK == 0
    k = np.arange(GU_BLK)
    perm = np.zeros((GU_BLK, GU_BLK), np.float32)
    perm[k, (k % 2) * LANES + k // 2] = 1.0
    return pl.pallas_call(
        _gu_prep_kernel,
        grid=(e, d // tk),
        in_specs=[pl.BlockSpec((1, tk, f2), lambda ei, ki: (ei, ki, 0)),
                  pl.BlockSpec((GU_BLK, GU_BLK), lambda ei, ki: (0, 0))],
        out_specs=pl.BlockSpec((1, tk, f2), lambda ei, ki: (ei, ki, 0)),
        out_shape=jax.ShapeDtypeStruct((e, d, f2), BF16),
        compiler_params=_cparams(("parallel", "parallel")),
        name="moe_gu_prep",
    )(w_gu, jnp.asarray(perm, BF16))


MOE_TM = 512


def _expert_kernel(te_ref, x_ref, wgu_ref, bg_ref, bl_ref, wd_ref, bd_ref, o_ref):
    hgu = _dot(x_ref[...], wgu_ref[0])
    nblk = hgu.shape[1] // GU_BLK
    glu = jnp.concatenate([hgu[:, c * GU_BLK:c * GU_BLK + LANES] for c in range(nblk)], axis=1)
    lin = jnp.concatenate([hgu[:, c * GU_BLK + LANES:(c + 1) * GU_BLK] for c in range(nblk)], axis=1)
    glu = jnp.minimum(glu + bg_ref[0], SWIGLU_LIMIT)
    lin = jnp.clip(lin + bl_ref[0], -SWIGLU_LIMIT, SWIGLU_LIMIT)
    act = glu * jax.nn.sigmoid(SWIGLU_ALPHA * glu) * (lin + 1.0)
    o_ref[...] = (_dot(act.astype(BF16), wd_ref[0].astype(BF16)) + bd_ref[0]).astype(o_ref.dtype)


def _experts(tile_expert, xs, w_gu, b_glu, b_lin, w_down, b_down):
    p, d = xs.shape
    tm = MOE_TM
    ff = w_down.shape[1]
    by_expert = lambda shape: pl.BlockSpec((1,) + shape, lambda i, te: (te[i], 0, 0))
    grid_spec = pltpu.PrefetchScalarGridSpec(
        num_scalar_prefetch=1,
        grid=(p // tm,),
        in_specs=[pl.BlockSpec((tm, d), lambda i, te: (i, 0)),
                  by_expert((d, 2 * ff)), by_expert((1, ff)), by_expert((1, ff)),
                  by_expert((ff, d)), by_expert((1, d))],
        out_specs=pl.BlockSpec((tm, d), lambda i, te: (i, 0)),
    )
    return pl.pallas_call(
        _expert_kernel,
        grid_spec=grid_spec,
        out_shape=jax.ShapeDtypeStruct((p, d), BF16),
        compiler_params=_cparams(("arbitrary",)),
        name="moe_experts",
    )(tile_expert, xs, w_gu, b_glu, b_lin, w_down, b_down)


def _moe(h_f32, h_bf16, w_router, b_router, layer, w_gu, b_glu, b_lin, w_down, b_down):
    n, d = h_f32.shape
    tm = MOE_TM
    idx128, wt128, rank128, cnt = _router(h_f32, w_router, b_router)
    expert = idx128[:, :TOP_K]
    na = n * TOP_K
    p = na + N_EXPERTS * tm
    counts = cnt[0].astype(I32)
    padded = ((counts + tm - 1) // tm) * tm
    pend = jnp.cumsum(padded)
    pstart = pend - padded
    first = jnp.sum(jnp.where(expert[..., None] == jnp.arange(N_EXPERTS, dtype=I32), pstart, 0), axis=-1)
    slot = first + rank128[:, :TOP_K]
    token = jnp.arange(na, dtype=I32) // TOP_K
    row_token = (jnp.arange(p, dtype=I32) % n).at[slot.reshape(-1)].set(
        token, mode="promise_in_bounds", unique_indices=True)
    tile_start = jnp.arange(p // tm, dtype=I32) * tm
    tile_expert = jnp.minimum(jnp.sum((tile_start[:, None] >= pend[None, :]).astype(I32), axis=1), N_EXPERTS - 1)
    xs = jnp.take(h_bf16, row_token, axis=0, mode="clip")
    ys = _experts(tile_expert + layer * N_EXPERTS, xs, w_gu, b_glu, b_lin, w_down, b_down)
    return [ys.at[slot[:, k]].get(mode="promise_in_bounds") for k in range(TOP_K)], wt128


def kernel(x, ln1_g, ln1_b, ln2_g, ln2_b, ev_w_in, ev_w_out, hg_lb_raw, hg_norm_g, od_w_in, od_w_out,
           cmp_k_w1, cmp_k_w2, cmp_k_pos, cmp_v_w1, cmp_v_w2, cmp_v_pos, rel_bias, router_w, router_b,
           exp_w_gu, exp_b_gu, exp_w_down, exp_b_down):
    b, t, d = x.shape
    n = b * t
    depth = ln1_g.shape[0]
    alpha = (2 * depth) ** 0.25
    lb_soft = jax.nn.softmax(hg_lb_raw.astype(F32), axis=0)
    lower_bounds = jnp.cumsum(lb_soft, axis=0) - lb_soft[0]
    pb, bnear, bfar, bpat = _nsa_bias_tables(rel_bias, t)

    n_all = depth * N_EXPERTS
    ff = exp_w_down.shape[2]
    wgu_all = _gu_prep(exp_w_gu.reshape(n_all, d, 2 * ff))
    wd_all = exp_w_down.reshape(n_all, ff, d)
    bg_all = exp_b_gu[..., 0::2].reshape(n_all, 1, ff)
    bl_all = exp_b_gu[..., 1::2].reshape(n_all, 1, ff)
    bd_all = exp_b_down.reshape(n_all, 1, d)

    h = x.reshape(n, d)
    hb = h.astype(BF16)
    for layer in range(depth):
        if layer % 2 == 0:
            e = layer // 2
            proj = _matmul(hb, ev_w_in, e)
            o_a = _sb_attention(proj, b, t)
            o_b = _hgrn2(proj, lower_bounds[layer], hg_norm_g[e], b, t)
            mix = _matmul2(o_a, o_b, ev_w_out, e)
        else:
            o = layer // 2
            proj = _matmul(hb, od_w_in, o, n=NSA_WIDTH + 6 * NSA_KV_WIDTH)
            gate_logits = _matmul(hb, od_w_in[o, :, NSA_WIDTH + 6 * NSA_KV_WIDTH:])
            proj3 = proj.reshape(b, t, -1)
            c0 = NSA_WIDTH // HEAD_DIM
            k_cmp = _compress(proj3, c0, cmp_k_w1[o], cmp_k_w2[o], cmp_k_pos[o])
            v_cmp = _compress(proj3, c0 + NSA_KV_GROUPS, cmp_v_w1[o], cmp_v_w2[o], cmp_v_pos[o])
            o_c, sel_t = _cmp_select(proj3, k_cmp, v_cmp, pb, b, t)
            sel = jnp.swapaxes(sel_t, 2, 3)
            o_s = _nsa_selected(proj3, c0 + 2 * NSA_KV_GROUPS, c0 + 3 * NSA_KV_GROUPS, bnear, bfar, sel, b, t)
            o_w = _nsa_window(proj3, c0 + 4 * NSA_KV_GROUPS, c0 + 5 * NSA_KV_GROUPS, bpat, b, t)
            mixed = _gate_mix(gate_logits, o_c.reshape(n, -1), o_s.reshape(n, -1), o_w.reshape(n, -1))
            mix = _matmul(mixed, od_w_out, o)
        h, hb = _add_ln(h, mix, ln1_g[layer], ln1_b[layer], alpha)
        ys, gate_w = _moe(h, hb, router_w[layer], router_b[layer], layer,
                          wgu_all, bg_all, bl_all, wd_all, bd_all)
        h, hb = _combine_ln(h, ys, gate_w, ln2_g[layer], ln2_b[layer], alpha)
    return h.reshape(b, t, d)
```

```python
import functools
import math

import jax
import jax.numpy as jnp
import numpy as np
from jax import lax
from jax.experimental import pallas as pl
from jax.experimental.pallas import tpu as pltpu

F32 = jnp.float32
BF16 = jnp.bfloat16
I32 = jnp.int32

HEAD_DIM = 128
SB_HEADS = 16
HG_HEADS = 16
SB_WIDTH = SB_HEADS * HEAD_DIM
HG_WIDTH = HG_HEADS * HEAD_DIM
F_MIN = 1e-6
NSA_HEADS = 32
NSA_KV_GROUPS = 4
NSA_HPG = NSA_HEADS // NSA_KV_GROUPS
NSA_WIDTH = NSA_HEADS * HEAD_DIM
NSA_KV_WIDTH = NSA_KV_GROUPS * HEAD_DIM
CMP_LEN = 32
CMP_STRIDE = 16
SLC_BLOCK = 64
N_SELECT = 16
WINDOW = 512
REL_BUCKETS = 32
REL_MAX_DIST = 128
N_EXPERTS = 32
TOP_K = 4
EXPERT_FF = 384
SWIGLU_LIMIT = 7.0
SWIGLU_ALPHA = 1.702
LN_EPS = 1e-5
NEG_INF = -1e30
FORCED = 1e9

LANES = 128
VMEM_LIMIT = 56 * 1024 * 1024

HI = lax.Precision.HIGHEST
LOG2E = 1.4426950408889634


def _cparams(sem):
    return pltpu.CompilerParams(dimension_semantics=sem, vmem_limit_bytes=VMEM_LIMIT)


def _dot_nt(a, b, **kw):
    return lax.dot_general(a, b, (((1,), (1,)), ((), ())), preferred_element_type=F32, **kw)


def _dot(a, b, **kw):
    return jnp.dot(a, b, preferred_element_type=F32, **kw)


def _mm_kernel(a_ref, w_ref, o_ref):
    o_ref[...] = _dot(a_ref[...], w_ref[...].astype(BF16)).astype(o_ref.dtype)


def _matmul(a, w, layer=None, *, n=None, tm=1024, tn=512, out_dtype=F32):
    m, k = a.shape
    n = w.shape[-1] if n is None else n
    tn = min(tn, n)
    assert m % tm == 0 and n % tn == 0
    if w.ndim == 3:
        w_spec = pl.BlockSpec((None, k, tn), lambda i, j: (layer, 0, j))
    else:
        w_spec = pl.BlockSpec((k, tn), lambda i, j: (0, j))
    return pl.pallas_call(
        _mm_kernel,
        grid=(m // tm, n // tn),
        in_specs=[pl.BlockSpec((tm, k), lambda i, j: (i, 0)), w_spec],
        out_specs=pl.BlockSpec((tm, tn), lambda i, j: (i, j)),
        out_shape=jax.ShapeDtypeStruct((m, n), out_dtype),
        compiler_params=_cparams(("parallel", "arbitrary")),
        name="matmul",
    )(a, w)


def _mm2_kernel(a1_ref, a2_ref, w1_ref, w2_ref, o_ref):
    o_ref[...] = (_dot(a1_ref[...], w1_ref[...].astype(BF16))
                  + _dot(a2_ref[...], w2_ref[...].astype(BF16)))


def _matmul2(a1, a2, w, layer, *, tm=1024, tn=512):
    m, k1 = a1.shape
    k2 = a2.shape[1]
    n = w.shape[-1]
    assert k1 == k2 and w.shape[1] == k1 + k2
    return pl.pallas_call(
        _mm2_kernel,
        grid=(m // tm, n // tn),
        in_specs=[pl.BlockSpec((tm, k1), lambda i, j: (i, 0)),
                  pl.BlockSpec((tm, k2), lambda i, j: (i, 0)),
                  pl.BlockSpec((None, k1, tn), lambda i, j: (layer, 0, j)),
                  pl.BlockSpec((None, k2, tn), lambda i, j: (layer, 1, j))],
        out_specs=pl.BlockSpec((tm, tn), lambda i, j: (i, j)),
        out_shape=jax.ShapeDtypeStruct((m, n), F32),
        compiler_params=_cparams(("parallel", "arbitrary")),
        name="matmul2",
    )(a1, a2, w, w)


def _ln_store(x, g_ref, b_ref, of_ref, ob_ref):
    mu = jnp.mean(x, axis=-1, keepdims=True)
    xc = x - mu
    var = jnp.mean(xc * xc, axis=-1, keepdims=True)
    out = xc * lax.rsqrt(var + LN_EPS) * g_ref[...] + b_ref[...]
    of_ref[...] = out
    ob_ref[...] = out.astype(BF16)


def _add_ln_kernel(h_ref, y_ref, g_ref, b_ref, of_ref, ob_ref, *, alpha):
    _ln_store(alpha * h_ref[...] + y_ref[...], g_ref, b_ref, of_ref, ob_ref)


def _combine_ln_kernel(h_ref, *refs, alpha):
    ys, (w_ref, g_ref, b_ref, of_ref, ob_ref) = refs[:TOP_K], refs[TOP_K:]
    w = w_ref[...]
    x = alpha * h_ref[...]
    for k in range(TOP_K):
        x = x + w[:, k:k + 1] * ys[k][...].astype(F32)
    _ln_store(x, g_ref, b_ref, of_ref, ob_ref)


def _combine_ln(h, ys, w128, g, b, alpha, *, tm=256):
    n, d = h.shape
    row = pl.BlockSpec((tm, d), lambda i: (i, 0))
    vec = pl.BlockSpec((1, d), lambda i: (0, 0))
    return pl.pallas_call(
        functools.partial(_combine_ln_kernel, alpha=alpha),
        grid=(n // tm,),
        in_specs=[row] * (1 + TOP_K) + [pl.BlockSpec((tm, LANES), lambda i: (i, 0)), vec, vec],
        out_specs=[row, row],
        out_shape=[jax.ShapeDtypeStruct((n, d), F32), jax.ShapeDtypeStruct((n, d), BF16)],
        compiler_params=_cparams(("parallel",)),
        name="moe_combine_ln",
    )(h, *ys, w128, g.reshape(1, d), b.reshape(1, d))


def _add_ln(h, y, g, b, alpha, *, tm=256):
    n, d = h.shape
    row = pl.BlockSpec((tm, d), lambda i: (i, 0))
    vec = pl.BlockSpec((1, d), lambda i: (0, 0))
    return pl.pallas_call(
        functools.partial(_add_ln_kernel, alpha=alpha),
        grid=(n // tm,),
        in_specs=[row, row, vec, vec],
        out_specs=[row, row],
        out_shape=[jax.ShapeDtypeStruct((n, d), F32), jax.ShapeDtypeStruct((n, d), BF16)],
        compiler_params=_cparams(("parallel",)),
        name="add_ln",
    )(h, y, g.reshape(1, d), b.reshape(1, d))


SB_TQ = 512
SB_KB = LANES
SB_DEAD = -104.0


def _sb_kernel(q_ref, k_ref, v_ref, u_ref, o_ref, kb_ref, vb_ref, acc_ref, carry_ref, *, tq, scale):
    i = pl.program_id(2)
    kb = SB_KB
    nb = tq // kb

    @pl.when(i == 0)
    def _():
        kb_ref[...] = k_ref[...].astype(BF16)
        vb_ref[...] = v_ref[...].astype(BF16)

    q = q_ref[...].astype(BF16)
    u = u_ref[...]
    acc_ref[...] = jnp.zeros_like(acc_ref)
    carry_ref[...] = jnp.zeros_like(carry_ref)

    def block_terms(z, causal=None):
        sp = jnp.log(1.0 + jnp.exp(-jnp.abs(z)))
        ls = jnp.minimum(z, 0.0) - sp
        lk = ls - z
        if causal is not None:
            lk = jnp.where(causal, lk, 0.0)
        hi = lk.astype(BF16)
        lo = (lk - hi.astype(F32)).astype(BF16)
        rt = _dot(jnp.concatenate([hi, lo], axis=1), u)
        return ls, rt[:, :kb] - lk, rt[:, kb:]

    base = pl.multiple_of(i * tq, tq)
    for j in reversed(range(nb)):
        r0 = j * kb
        rows = tq - r0
        kj = kb_ref[pl.ds(base + r0, kb), :]
        vj = vb_ref[pl.ds(base + r0, kb), :]
        z = _dot_nt(q[r0:], kj) * scale
        causal = lax.broadcasted_iota(I32, (rows, kb), 1) < lax.broadcasted_iota(I32, (rows, kb), 0)
        ls, between, tot = block_terms(z, causal)
        carry = carry_ref[r0:, :]
        w = jnp.where(causal, jnp.exp(ls + between + carry), 0.0)
        acc_ref[r0:, :] += _dot(w.astype(BF16), vj)
        carry_ref[r0:, :] = carry + tot

    def far(state):
        c, _ = state
        start = pl.multiple_of(c * tq, tq)
        z = _dot_nt(q, kb_ref[pl.ds(start, tq), :]) * scale
        terms = [block_terms(z[:, j * kb:(j + 1) * kb]) for j in range(nb)]
        carry = carry_ref[...]
        ws = [None] * nb
        for j in reversed(range(nb)):
            ls, between, tot = terms[j]
            ws[j] = jnp.exp(ls + between + carry).astype(BF16)
            carry = carry + tot
        acc_ref[...] += _dot(jnp.concatenate(ws, axis=1), vb_ref[pl.ds(start, tq), :])
        carry_ref[...] = carry
        return c - 1, jnp.max(carry)

    lax.while_loop(lambda st: (st[0] >= 0) & (st[1] > SB_DEAD), far, (i - 1, jnp.max(carry_ref[...])))
    o_ref[...] = acc_ref[...].astype(o_ref.dtype)


def _sb_attention(proj, b, t):
    tq, kb = min(SB_TQ, t), SB_KB
    nq = t // tq
    r = np.arange(2 * kb)[:, None] % kb
    c = np.arange(2 * kb)[None, :]
    u = jnp.asarray(np.where(c < kb, r >= c, True), BF16)
    kv_spec = lambda off: pl.BlockSpec((t, HEAD_DIM), lambda bi, h, i: (bi, off + h))
    return pl.pallas_call(
        functools.partial(_sb_kernel, tq=tq, scale=HEAD_DIM ** -0.5),
        grid=(b, SB_HEADS, nq),
        in_specs=[pl.BlockSpec((tq, HEAD_DIM), lambda bi, h, i: (bi * nq + i, h)),
                  kv_spec(SB_HEADS), kv_spec(2 * SB_HEADS),
                  pl.BlockSpec((2 * kb, 2 * kb), lambda bi, h, i: (0, 0))],
        out_specs=pl.BlockSpec((tq, HEAD_DIM), lambda bi, h, i: (bi * nq + i, h)),
        out_shape=jax.ShapeDtypeStruct((b * t, SB_WIDTH), BF16),
        scratch_shapes=[pltpu.VMEM((t, HEAD_DIM), BF16), pltpu.VMEM((t, HEAD_DIM), BF16),
                        pltpu.VMEM((tq, HEAD_DIM), F32), pltpu.VMEM((tq, kb), F32)],
        compiler_params=_cparams(("parallel", "parallel", "arbitrary")),
        name="sb_attention",
    )(proj, proj, proj, u)


HG_C = 64
HG_TB = 512
HG_NH = 8


def _hg_tables(c):
    levels = []
    m = c // 2
    while m >= 1:
        levels.append(m)
        m //= 2
    nl = len(levels)
    e = np.zeros((nl + 2, c, c), np.float32)
    mask = np.zeros((nl + 1, c, c), np.float32)
    p = np.arange(c)
    for li, m in enumerate(levels):
        blk = p // (2 * m)
        half = (p // m) % 2
        mid = blk * 2 * m + m - 1
        for r in range(c):
            if half[r] == 1:
                e[li, r, mid[r] + 1:r + 1] = 1.0
            else:
                e[li, r, r + 1:mid[r] + 1] = 1.0
        mask[li] = ((half[:, None] == 1) & (half[None, :] == 0) & (blk[:, None] == blk[None, :]))
    mask[nl] = np.eye(c)
    e[nl] = np.tril(np.ones((c, c)))
    e[nl + 1] = np.triu(np.ones((c, c)), 1)
    return e.reshape((nl + 2) * c, c), mask, nl


def _hg_kernel(qh_ref, fh_ref, ih_ref, gh_ref, lb_ref, ng_ref, e_ref, mask_ref, o_ref, st_ref,
               *, c, tb, nl, nh):
    @pl.when(pl.program_id(2) == 0)
    def _():
        st_ref[...] = jnp.zeros_like(st_ref)

    ng = ng_ref[...]
    emat = e_ref[...]

    def one_head(rows, hh):
        cols = slice(hh * HEAD_DIM, (hh + 1) * HEAD_DIM)
        lb = lb_ref[:, cols]
        qh = qh_ref[rows, cols]
        f = lb + (1.0 - lb) * jax.nn.sigmoid(fh_ref[rows, cols])
        g = jnp.log(jnp.maximum(f, F_MIN))
        kk = 1.0 - f
        q = qh * jax.nn.sigmoid(qh)
        v = ih_ref[rows, cols]
        vb = v.astype(BF16)
        g1 = g.astype(BF16)
        r1 = g - g1.astype(F32)
        g2 = r1.astype(BF16)
        g3 = (r1 - g2.astype(F32)).astype(BF16)
        ex = jnp.exp(_dot(emat, jnp.concatenate([g1, g2, g3], axis=0)))
        scores = mask_ref[nl] * _dot_nt(q.astype(BF16), kk.astype(BF16))
        for li in range(nl):
            a = ex[li * c:(li + 1) * c]
            scores = scores + mask_ref[li] * _dot_nt((q * a).astype(BF16), (kk * a).astype(BF16))
        ecum = ex[nl * c:(nl + 1) * c]
        erest = ex[(nl + 1) * c:(nl + 2) * c]
        st = st_ref[hh]
        o = _dot(scores.astype(BF16), vb) + _dot_nt((q * ecum).astype(BF16), st.astype(BF16))
        st_ref[hh] = ecum[c - 1:c, :] * st + _dot(v.T.astype(BF16), (kk * erest).astype(BF16))
        gh = gh_ref[rows, cols]
        o = o * lax.rsqrt(jnp.mean(o * o, axis=-1, keepdims=True) + LN_EPS) * ng
        o_ref[rows, cols] = (o * (gh * jax.nn.sigmoid(gh))).astype(o_ref.dtype)

    def chunk(ci, carry):
        rows = pl.ds(pl.multiple_of(ci * c, c), c)
        for hh in range(nh):
            one_head(rows, hh)
        return carry

    lax.fori_loop(0, tb // c, chunk, 0)


def _hgrn2(proj, lb, norm_g, b, t):
    c, tb, nh = HG_C, min(HG_TB, t), HG_NH
    e, mask, nl = _hg_tables(c)
    e3 = np.concatenate([e, e, e], axis=1)
    nt = t // tb
    base = 3 * SB_HEADS
    assert base % nh == 0 and HG_HEADS % nh == 0
    wide = nh * HEAD_DIM
    col = lambda k: pl.BlockSpec((tb, wide), lambda bi, h, i: (bi * nt + i, (base + k * HG_HEADS) // nh + h))
    const2 = lambda a: pl.BlockSpec(a.shape, lambda bi, h, i: (0,) * a.ndim)
    return pl.pallas_call(
        functools.partial(_hg_kernel, c=c, tb=tb, nl=nl, nh=nh),
        grid=(b, HG_HEADS // nh, nt),
        in_specs=[col(0), col(1), col(2), col(3),
                  pl.BlockSpec((1, wide), lambda bi, h, i: (0, h)),
                  pl.BlockSpec((1, HEAD_DIM), lambda bi, h, i: (0, 0)),
                  const2(e3), const2(mask)],
        out_specs=pl.BlockSpec((tb, wide), lambda bi, h, i: (bi * nt + i, h)),
        out_shape=jax.ShapeDtypeStruct((b * t, HG_WIDTH), BF16),
        scratch_shapes=[pltpu.VMEM((nh, HEAD_DIM, HEAD_DIM), F32)],
        compiler_params=_cparams(("parallel", "parallel", "arbitrary")),
        name="hgrn2",
    )(proj, proj, proj, proj, lb.reshape(1, HG_WIDTH), norm_g.reshape(1, HEAD_DIM),
      jnp.asarray(e3, BF16), jnp.asarray(mask))


def _cmp_kernel(a_ref, w1_ref, w2_ref, pos_ref, o_ref, *, ncp):
    half = CMP_LEN // 2
    p = jnp.zeros((ncp, HEAD_DIM), F32)
    q = jnp.zeros((ncp, HEAD_DIM), F32)
    for j in range(half):
        s = a_ref[pl.ds(j, ncp, stride=CMP_STRIDE), :]
        p = p + _dot((s + pos_ref[j:j + 1, :]).astype(BF16), w1_ref[j].astype(BF16))
        q = q + _dot((s + pos_ref[half + j:half + j + 1, :]).astype(BF16), w1_ref[half + j].astype(BF16))
    hid = p + pltpu.roll(q, ncp - 1, 0)
    hid = hid * jax.nn.sigmoid(hid)
    out = _dot(hid.astype(BF16), w2_ref[...].astype(BF16))
    row = lax.broadcasted_iota(I32, (ncp, HEAD_DIM), 0)
    o_ref[0, 0] = jnp.where(row < ncp - 1, out, 0.0)


def _compress(proj3, col0, w1, w2, pos):
    b, t, _ = proj3.shape
    ncp = t // CMP_STRIDE
    assert CMP_LEN == 2 * CMP_STRIDE
    return pl.pallas_call(
        functools.partial(_cmp_kernel, ncp=ncp),
        grid=(b, NSA_KV_GROUPS),
        in_specs=[pl.BlockSpec((None, t, HEAD_DIM), lambda bi, g: (bi, 0, col0 + g)),
                  pl.BlockSpec((CMP_LEN, HEAD_DIM, HEAD_DIM), lambda bi, g: (0, 0, 0)),
                  pl.BlockSpec((HEAD_DIM, HEAD_DIM), lambda bi, g: (0, 0)),
                  pl.BlockSpec((CMP_LEN, HEAD_DIM), lambda bi, g: (0, 0))],
        out_specs=pl.BlockSpec((1, 1, ncp, HEAD_DIM), lambda bi, g: (bi, g, 0, 0)),
        out_shape=jax.ShapeDtypeStruct((b, NSA_KV_GROUPS, ncp, HEAD_DIM), F32),
        compiler_params=_cparams(("parallel", "parallel")),
        name="nsa_compress",
    )(proj3, w1.reshape(CMP_LEN, HEAD_DIM, HEAD_DIM), w2, pos)


NSA_TQ = 128
CMP_NEAR = 16
CMP_NEAR_LO = 9


def _rel_bucket_np(dist):
    dist = np.maximum(dist, 0)
    max_exact = REL_BUCKETS // 2
    ratio = (np.log(np.maximum(dist, max_exact).astype(np.float32) / np.float32(max_exact))
             / np.float32(math.log(REL_MAX_DIST / max_exact)))
    large = np.minimum(max_exact + (ratio * np.float32(REL_BUCKETS - max_exact)).astype(np.int32),
                       REL_BUCKETS - 1)
    return np.where(dist < max_exact, dist, large).astype(np.int32)


def _cmpsel_kernel(q_ref, kc_ref, vc_ref, pb_ref, ovl_ref, oc_ref, sel_ref,
                   *, tq, ncp, nslc, nsel, scale):
    i = pl.program_id(2)
    kc = kc_ref[0, 0].astype(BF16)
    vc = vc_ref[0, 0].astype(BF16)
    qpos = i * tq + lax.broadcasted_iota(I32, (tq, ncp), 0)
    ncol = lax.broadcasted_iota(I32, (tq, ncp), 1)
    valid = (ncol * CMP_STRIDE + (CMP_LEN - 1) <= qpos) & (ncol < ncp - 1)
    sr = lax.broadcasted_iota(I32, (LANES, ncp), 0)
    sc = lax.broadcasted_iota(I32, (LANES, ncp), 1)
    first = i * (tq // CMP_STRIDE) - CMP_NEAR_LO
    place = jnp.where(sr < CMP_NEAR, jnp.where(sc == first + sr, 1.0, 0.0),
                      jnp.where(sr == CMP_NEAR, jnp.where(sc < first, 1.0, 0.0), 0.0))
    place = place.astype(BF16)
    bias = _dot(pb_ref[0], jnp.concatenate([place, place], axis=0))
    psum = jnp.zeros((tq, ncp), F32)
    for r in range(NSA_HPG):
        hs = slice(r * HEAD_DIM, (r + 1) * HEAD_DIM)
        s = _dot_nt(q_ref[0, :, hs].astype(BF16), kc) * scale + bias[r * tq:(r + 1) * tq]
        s = jnp.where(valid, s, NEG_INF)
        mx = jnp.max(s, axis=-1, keepdims=True)
        e = jnp.where(valid, jnp.exp(s - mx), 0.0)
        den = jnp.sum(e, axis=-1, keepdims=True)
        p = e / jnp.maximum(den, 1e-30)
        oc_ref[0, :, hs] = _dot(p.astype(BF16), vc)
        psum = psum + p
    imp = _dot_nt(ovl_ref[...], psum, precision=HI)
    jrow = lax.broadcasted_iota(I32, (nslc, tq), 0)
    qblk = (i * tq + lax.broadcasted_iota(I32, (nslc, tq), 1)) // SLC_BLOCK
    ok = jrow <= qblk
    forced = (jrow == 0) | (jrow == qblk) | (jrow == qblk - 1)
    imp = jnp.where(ok, jnp.where(forced, FORCED, imp), NEG_INF)
    rank = jnp.zeros((nslc, tq), F32)
    for j2 in range(nslc):
        row = imp[j2:j2 + 1, :]
        tie = jnp.where(jrow > j2, 1.0, 0.0)
        rank = rank + jnp.where(row > imp, 1.0, jnp.where(row == imp, tie, 0.0))
    sel_ref[0, 0] = jnp.where((rank < nsel) & ok, 1.0, 0.0)


def _cmp_select(proj3, k_cmp, v_cmp, pb, b, t):
    tq = NSA_TQ
    ncp = t // CMP_STRIDE
    nslc = t // SLC_BLOCK
    nsel = min(N_SELECT, nslc)
    n_idx = np.arange(ncp)
    slc_start = np.arange(nslc) * SLC_BLOCK
    cs = n_idx * CMP_STRIDE
    ovl = ((cs[None, :] < slc_start[:, None] + SLC_BLOCK)
           & (cs[None, :] + CMP_LEN - 1 >= slc_start[:, None])
           & (n_idx[None, :] < ncp - 1)).astype(np.float32)
    gw = NSA_HPG * HEAD_DIM
    return pl.pallas_call(
        functools.partial(_cmpsel_kernel, tq=tq, ncp=ncp, nslc=nslc, nsel=nsel, scale=HEAD_DIM ** -0.5),
        grid=(b, NSA_KV_GROUPS, t // tq),
        in_specs=[pl.BlockSpec((1, tq, gw), lambda bi, g, i: (bi, i, g)),
                  pl.BlockSpec((1, 1, ncp, HEAD_DIM), lambda bi, g, i: (bi, g, 0, 0)),
                  pl.BlockSpec((1, 1, ncp, HEAD_DIM), lambda bi, g, i: (bi, g, 0, 0)),
                  pl.BlockSpec((1, NSA_HPG * tq, 2 * LANES), lambda bi, g, i: (g, 0, 0)),
                  pl.BlockSpec((nslc, ncp), lambda bi, g, i: (0, 0))],
        out_specs=[pl.BlockSpec((1, tq, gw), lambda bi, g, i: (bi, i, g)),
                   pl.BlockSpec((1, 1, nslc, tq), lambda bi, g, i: (bi, g, 0, i))],
        out_shape=[jax.ShapeDtypeStruct((b, t, NSA_WIDTH), F32),
                   jax.ShapeDtypeStruct((b, NSA_KV_GROUPS, nslc, t), F32)],
        compiler_params=_cparams(("parallel", "parallel", "parallel")),
        name="nsa_cmp_select",
    )(proj3, k_cmp, v_cmp, pb, jnp.asarray(ovl))


SEL_CHUNK = 512
SEL_VEXT = 16
MASKED = -1e30
FAR_BIAS_COL = SLC_BLOCK


def _softmax_step(st, vt, m_ref, acc_ref, c1):
    m_old = m_ref[...]
    m_new = jnp.maximum(m_old, jnp.max(st, axis=0, keepdims=True))
    p = jnp.exp2((st - m_new) * c1)
    acc_ref[...] = jnp.exp2((m_old - m_new) * c1) * acc_ref[...] + _dot(vt, p.astype(BF16))
    m_ref[...] = m_new


def _sel_kernel(q_ref, k_ref, v_ref, kext_ref, vext_ref, bnear_ref, bfar_ref, sel_ref, o_ref,
                qa_ref, ka_ref, vt_ref, m_ref, acc_ref, s0_ref, s1_ref, *, tq, c1):
    i = pl.program_id(2)
    hpg = NSA_HPG
    d = HEAD_DIM

    @pl.when(i == 0)
    def _():
        ka_ref[:, :d] = k_ref[...].astype(BF16)
        ka_ref[:, d:] = kext_ref[...]
        for c in range(vt_ref.shape[0]):
            vt_ref[c, :d, :] = v_ref[c * SEL_CHUNK:(c + 1) * SEL_CHUNK, :].T.astype(BF16)
            vt_ref[c, d:, :] = vext_ref[...]

    sel = sel_ref[0, 0]
    nslc = sel.shape[1]
    blk = lax.broadcasted_iota(I32, (tq, nslc), 1)
    far_blocks = (i - 1) * (tq // SLC_BLOCK)
    maskq = jnp.where(blk < far_blocks, jnp.where(sel > 0.5, 0.0, MASKED), MASKED)
    ext = jnp.concatenate([maskq, jnp.zeros((tq, LANES - nslc), F32)], axis=1)
    for r in range(hpg):
        qa_ref[r * tq:(r + 1) * tq, :d] = q_ref[0, :, r * d:(r + 1) * d].astype(BF16)
        qa_ref[r * tq:(r + 1) * tq, d:] = (ext + bfar_ref[0, r:r + 1, :]).astype(BF16)
    m_ref[...] = jnp.full_like(m_ref, MASKED)
    acc_ref[...] = jnp.zeros_like(acc_ref)

    n_chunks = vt_ref.shape[0]

    def logits(c):
        start = pl.multiple_of(jnp.minimum(c, n_chunks - 1) * SEL_CHUNK, SEL_CHUNK)
        return _dot_nt(ka_ref[pl.ds(start, SEL_CHUNK), :], qa_ref[...])

    n_far = (jnp.maximum(i - 1, 0) * tq + SEL_CHUNK - 1) // SEL_CHUNK
    n_pairs = n_far // 2

    @pl.when(n_far > 0)
    def _():
        s0_ref[...] = logits(0)

    def pair(j, carry):
        c0 = 2 * j
        s1_ref[...] = logits(c0 + 1)
        _softmax_step(s0_ref[...], vt_ref[c0], m_ref, acc_ref, c1)
        s0_ref[...] = logits(c0 + 2)
        _softmax_step(s1_ref[...], vt_ref[c0 + 1], m_ref, acc_ref, c1)
        return carry

    lax.fori_loop(0, n_pairs, pair, 0)

    @pl.when(n_far % 2 == 1)
    def _():
        _softmax_step(s0_ref[...], vt_ref[n_far - 1], m_ref, acc_ref, c1)

    p0 = pl.multiple_of(jnp.maximum(i - 1, 0) * tq, tq)
    p1 = pl.multiple_of(i * tq, tq)
    kn = jnp.concatenate([ka_ref[pl.ds(p0, tq), :d], ka_ref[pl.ds(p1, tq), :d]], axis=0)
    vn = jnp.concatenate([v_ref[pl.ds(p0, tq), :], v_ref[pl.ds(p1, tq), :]], axis=0)
    vnt = jnp.concatenate([vn.T.astype(BF16), vext_ref[:, :2 * tq]], axis=0)
    er = lax.broadcasted_iota(I32, (2 * tq, nslc), 1)
    ec = lax.broadcasted_iota(I32, (2 * tq, nslc), 0) // SLC_BLOCK
    expand = jnp.where(er == far_blocks + ec, 1.0, 0.0).astype(BF16)
    picked = _dot_nt(expand, sel.astype(BF16))
    kj = lax.broadcasted_iota(I32, (2 * tq, tq), 0)
    qi = lax.broadcasted_iota(I32, (2 * tq, tq), 1)
    keep = jnp.where(kj <= qi + tq, picked, 0.0)
    keep = jnp.concatenate([keep] * hpg, axis=1) > 0.5
    st = _dot_nt(kn, qa_ref[:, :d]) + bnear_ref[0]
    _softmax_step(jnp.where(keep, st, MASKED), vnt, m_ref, acc_ref, c1)

    acc = acc_ref[...]
    out = acc[:d, :] / acc[d:d + 1, :]
    for r in range(hpg):
        o_ref[0, :, r * d:(r + 1) * d] = out[:, r * tq:(r + 1) * tq].T


def _nsa_selected(proj3, kcol, vcol, bnear, bfar, sel, b, t):
    tq = NSA_TQ
    d = HEAD_DIM
    gw = NSA_HPG * d
    nslc = t // SLC_BLOCK
    assert nslc <= FAR_BIAS_COL and t % SEL_CHUNK == 0
    rows = NSA_HPG * tq
    pos = np.arange(t)
    kext = np.zeros((t, LANES), np.float32)
    kext[pos, pos // SLC_BLOCK] = 1.0
    kext[:, FAR_BIAS_COL:FAR_BIAS_COL + 2] = 1.0
    vext = np.zeros((SEL_VEXT, SEL_CHUNK), np.float32)
    vext[0, :] = 1.0
    kv = lambda col: pl.BlockSpec((None, t, d), lambda bi, g, i: (bi, 0, col + g))
    const = lambda shape: pl.BlockSpec(shape, lambda bi, g, i: (0,) * len(shape))
    return pl.pallas_call(
        functools.partial(_sel_kernel, tq=tq, c1=d ** -0.5 * LOG2E),
        grid=(b, NSA_KV_GROUPS, t // tq),
        in_specs=[pl.BlockSpec((1, tq, gw), lambda bi, g, i: (bi, i, g)),
                  kv(kcol), kv(vcol), const((t, LANES)), const((SEL_VEXT, SEL_CHUNK)),
                  pl.BlockSpec((1, 2 * tq, rows), lambda bi, g, i: (g, 0, 0)),
                  pl.BlockSpec((1, NSA_HPG, LANES), lambda bi, g, i: (g, 0, 0)),
                  pl.BlockSpec((1, 1, tq, nslc), lambda bi, g, i: (bi, g, i, 0))],
        out_specs=pl.BlockSpec((1, tq, gw), lambda bi, g, i: (bi, i, g)),
        out_shape=jax.ShapeDtypeStruct((b, t, NSA_WIDTH), F32),
        scratch_shapes=[pltpu.VMEM((rows, 2 * d), BF16),
                        pltpu.VMEM((t, 2 * d), BF16),
                        pltpu.VMEM((t // SEL_CHUNK, d + SEL_VEXT, SEL_CHUNK), BF16),
                        pltpu.VMEM((1, rows), F32),
                        pltpu.VMEM((d + SEL_VEXT, rows), F32),
                        pltpu.VMEM((SEL_CHUNK, rows), F32),
                        pltpu.VMEM((SEL_CHUNK, rows), F32)],
        compiler_params=_cparams(("parallel", "parallel", "arbitrary")),
        name="nsa_selected",
    )(proj3, proj3, proj3, jnp.asarray(kext, BF16), jnp.asarray(vext, BF16),
      jnp.swapaxes(bnear, 1, 2), bfar, sel)


def _win_kernel(q_ref, k_ref, v_ref, bpat_ref, g_ref, oc_ref, os_ref, o_ref, kb_ref, vb_ref,
                *, tq, window, c1):
    i = pl.program_id(2)
    hpg = NSA_HPG
    d = HEAD_DIM
    nt = window // tq + 1

    @pl.when(i == 0)
    def _():
        kb_ref[...] = k_ref[...].astype(BF16)
        vb_ref[...] = v_ref[...].astype(BF16)

    qs = jnp.concatenate([q_ref[0, :, r * d:(r + 1) * d] for r in range(hpg)], axis=0).astype(BF16)
    t0 = jnp.maximum(i - (nt - 1), 0)
    qi = lax.broadcasted_iota(I32, (tq, tq), 0)
    kj = lax.broadcasted_iota(I32, (tq, tq), 1)
    parts = []
    for kk in range(nt):
        dt = i - (t0 + kk)
        start = pl.multiple_of((t0 + kk) * tq, tq)
        z = _dot_nt(qs, kb_ref[pl.ds(start, tq), :]) + bpat_ref[0, jnp.clip(dt, 0, 2)]
        dist = dt * tq + qi - kj
        keep = jnp.where(dist >= 0, jnp.where(dist < window, 1.0, 0.0), 0.0) > 0.5
        parts.append(jnp.where(keep[None], z.reshape(hpg, tq, tq), MASKED))
    s = jnp.concatenate(parts, axis=-1)
    m = jnp.max(s, axis=-1, keepdims=True)
    p = jnp.exp2((s - m) * c1)
    den = jnp.sum(p, axis=-1, keepdims=True)
    vwin = vb_ref[pl.ds(pl.multiple_of(t0 * tq, tq), nt * tq), :]
    out = (_dot(p.reshape(hpg * tq, nt * tq).astype(BF16), vwin).reshape(hpg, tq, d) / den)
    gate = jax.nn.sigmoid(g_ref[0, 0])
    for r in range(hpg):
        hs = slice(r * d, (r + 1) * d)
        mix = (gate[:, 3 * r:3 * r + 1] * oc_ref[0, :, hs] + gate[:, 3 * r + 1:3 * r + 2] * os_ref[0, :, hs]
               + gate[:, 3 * r + 2:3 * r + 3] * out[r])
        o_ref[0, :, hs] = mix.astype(o_ref.dtype)


def _nsa_window_mix(proj3, kcol, vcol, bpat, gate_logits, o_c, o_s, b, t):
    tq = NSA_TQ
    d = HEAD_DIM
    gw = NSA_HPG * d
    ng = 3 * NSA_HPG
    assert WINDOW % tq == 0 and t >= WINDOW + tq
    gates = jnp.transpose(gate_logits.reshape(b, t, NSA_KV_GROUPS, ng), (0, 2, 1, 3))
    kv = lambda col: pl.BlockSpec((None, t, d), lambda bi, g, i: (bi, 0, col + g))
    tile = pl.BlockSpec((1, tq, gw), lambda bi, g, i: (bi, i, g))
    return pl.pallas_call(
        functools.partial(_win_kernel, tq=tq, window=WINDOW, c1=d ** -0.5 * LOG2E),
        grid=(b, NSA_KV_GROUPS, t // tq),
        in_specs=[tile, kv(kcol), kv(vcol),
                  pl.BlockSpec((1, 3, NSA_HPG * tq, tq), lambda bi, g, i: (g, 0, 0, 0)),
                  pl.BlockSpec((1, 1, tq, ng), lambda bi, g, i: (bi, g, i, 0)),
                  tile, tile],
        out_specs=tile,
        out_shape=jax.ShapeDtypeStruct((b, t, NSA_WIDTH), BF16),
        scratch_shapes=[pltpu.VMEM((t, d), BF16), pltpu.VMEM((t, d), BF16)],
        compiler_params=_cparams(("parallel", "parallel", "arbitrary")),
        name="nsa_window_mix",
    )(proj3, proj3, proj3, bpat, gates, o_c, o_s)


def _nsa_bias_tables(rel_bias, t):
    tq = NSA_TQ
    g, hpg = NSA_KV_GROUPS, NSA_HPG
    inv_scale = HEAD_DIM ** 0.5
    tab = rel_bias.astype(F32)
    last = REL_BUCKETS - 1
    far_from = int(np.nonzero(_rel_bucket_np(np.arange(4 * REL_MAX_DIST)) < last)[0].max()) + 1
    assert (_rel_bucket_np(np.arange(far_from, t + tq)) == last).all()

    def by_group(a):
        a = jnp.moveaxis(a, -1, 0)
        return a.reshape((g, hpg) + a.shape[1:])

    iq = np.arange(tq)[:, None]
    m = np.arange(CMP_NEAR)[None, :]
    dist_c = iq - CMP_STRIDE * (m - CMP_NEAR_LO) - (CMP_LEN - 1)
    assert dist_c[:, 0].min() >= far_from
    assert (iq - CMP_STRIDE * (CMP_NEAR - CMP_NEAR_LO) - (CMP_LEN - 1)).max() < 0
    pb = jnp.zeros((g, hpg, tq, LANES), F32)
    pb = pb.at[..., :CMP_NEAR].set(by_group(tab[_rel_bucket_np(dist_c)]))
    pb = pb.at[..., CMP_NEAR].set(by_group(tab[last])[..., None])
    pb = pb.reshape(g, hpg * tq, LANES)
    pb_hi = pb.astype(BF16)
    pb = jnp.concatenate([pb_hi, (pb - pb_hi.astype(F32)).astype(BF16)], axis=-1)
    assert 2 * tq - (tq - 1) >= far_from
    jk = np.arange(tq)[None, :]
    idx = np.stack([_rel_bucket_np(iq - jk), _rel_bucket_np(tq + iq - jk), np.full((tq, tq), last, np.int32)])
    pat = by_group(tab[idx]) * inv_scale
    bpat = jnp.transpose(pat, (0, 2, 1, 3, 4)).reshape(g, 3, hpg * tq, tq)
    bnear = jnp.concatenate([pat[:, :, 1], pat[:, :, 0]], axis=-1).reshape(g, hpg * tq, 2 * tq)
    far = by_group(tab[last]) * inv_scale
    far_hi = far.astype(BF16).astype(F32)
    bfar = jnp.zeros((g, hpg, LANES), F32)
    bfar = bfar.at[..., FAR_BIAS_COL].set(far_hi).at[..., FAR_BIAS_COL + 1].set(far - far_hi)
    return pb, bnear, bfar, bpat


def _router_kernel(h_ref, w_ref, b_ref, tri_ref, idx_ref, wt_ref, rank_ref, cnt_ref, *, tm):
    @pl.when(pl.program_id(0) == 0)
    def _():
        cnt_ref[...] = jnp.zeros_like(cnt_ref)

    logits = _dot(h_ref[...], w_ref[...], precision=HI) + b_ref[...]
    lane = lax.broadcasted_iota(I32, (tm, N_EXPERTS), 1).astype(F32)
    out_lane = lax.broadcasted_iota(I32, (tm, LANES), 1)
    work = logits
    idx_out = jnp.zeros((tm, LANES), F32)
    val_out = jnp.zeros((tm, LANES), F32)
    top = None
    den = jnp.zeros((tm, 1), F32)
    hots = []
    for k in range(TOP_K):
        mx = jnp.max(work, axis=-1, keepdims=True)
        idx = jnp.min(jnp.where(work == mx, lane, float(N_EXPERTS)), axis=-1, keepdims=True)
        if top is None:
            top = mx
        e = jnp.exp(mx - top)
        den = den + e
        idx_out = jnp.where(out_lane == k, idx, idx_out)
        val_out = jnp.where(out_lane == k, e, val_out)
        hots.append(jnp.where(lane == idx, 1.0, 0.0))
        work = jnp.where(lane == idx, -jnp.inf, work)
    idx_ref[...] = idx_out.astype(I32)
    wt_ref[...] = val_out / den
    hot = hots[0] + hots[1] + hots[2] + hots[3]
    before = cnt_ref[...] + _dot(tri_ref[...], hot.astype(BF16))
    rank_out = jnp.zeros((tm, LANES), F32)
    for k in range(TOP_K):
        rank_out = jnp.where(out_lane == k, jnp.sum(hots[k] * before, axis=-1, keepdims=True), rank_out)
    rank_ref[...] = rank_out.astype(I32)
    cnt_ref[...] += jnp.sum(hot, axis=0, keepdims=True)


def _router(h, w, bias, *, tm=512):
    n, d = h.shape
    out = pl.BlockSpec((tm, LANES), lambda i: (i, 0))
    tri = jnp.asarray(np.tril(np.ones((tm, tm), np.float32), -1), BF16)
    return pl.pallas_call(
        functools.partial(_router_kernel, tm=tm),
        grid=(n // tm,),
        in_specs=[pl.BlockSpec((tm, d), lambda i: (i, 0)),
                  pl.BlockSpec((d, N_EXPERTS), lambda i: (0, 0)),
                  pl.BlockSpec((1, N_EXPERTS), lambda i: (0, 0)),
                  pl.BlockSpec((tm, tm), lambda i: (0, 0))],
        out_specs=[out, out, out, pl.BlockSpec((1, N_EXPERTS), lambda i: (0, 0))],
        out_shape=[jax.ShapeDtypeStruct((n, LANES), I32), jax.ShapeDtypeStruct((n, LANES), F32),
                   jax.ShapeDtypeStruct((n, LANES), I32), jax.ShapeDtypeStruct((1, N_EXPERTS), F32)],
        compiler_params=_cparams(("arbitrary",)),
        name="moe_router",
    )(h, w, bias.reshape(1, N_EXPERTS), tri)


GU_BLK = 2 * LANES


def _gu_prep_kernel(w_ref, p_ref, o_ref):
    for c in range(w_ref.shape[-1] // GU_BLK):
        cols = slice(c * GU_BLK, (c + 1) * GU_BLK)
        o_ref[0, :, cols] = _dot(w_ref[0, :, cols].astype(BF16), p_ref[...]).astype(BF16)


def _gu_prep(w_gu, *, tk=4096):
    e, d, f2 = w_gu.shape
    assert f2 % GU_BLK == 0
    k = np.arange(GU_BLK)
    perm = np.zeros((GU_BLK, GU_BLK), np.float32)
    perm[k, (k % 2) * LANES + k // 2] = 1.0
    return pl.pallas_call(
        _gu_prep_kernel,
        grid=(e, d // tk),
        in_specs=[pl.BlockSpec((1, tk, f2), lambda ei, ki: (ei, ki, 0)),
                  pl.BlockSpec((GU_BLK, GU_BLK), lambda ei, ki: (0, 0))],
        out_specs=pl.BlockSpec((1, tk, f2), lambda ei, ki: (ei, ki, 0)),
        out_shape=jax.ShapeDtypeStruct((e, d, f2), BF16),
        compiler_params=_cparams(("parallel", "parallel")),
        name="moe_gu_prep",
    )(w_gu, jnp.asarray(perm, BF16))


MOE_TM = 512


def _expert_kernel(te_ref, x_ref, wgu_ref, bg_ref, bl_ref, wd_ref, bd_ref, o_ref):
    @pl.when(pl.program_id(0) < te_ref[pl.num_programs(0)])
    def _():
        hgu = _dot(x_ref[...], wgu_ref[0])
        nblk = hgu.shape[1] // GU_BLK
        glu = jnp.concatenate([hgu[:, c * GU_BLK:c * GU_BLK + LANES] for c in range(nblk)], axis=1)
        lin = jnp.concatenate([hgu[:, c * GU_BLK + LANES:(c + 1) * GU_BLK] for c in range(nblk)], axis=1)
        glu = jnp.minimum(glu + bg_ref[0], SWIGLU_LIMIT)
        lin = jnp.clip(lin + bl_ref[0], -SWIGLU_LIMIT, SWIGLU_LIMIT)
        act = glu * jax.nn.sigmoid(SWIGLU_ALPHA * glu) * (lin + 1.0)
        o_ref[...] = (_dot(act.astype(BF16), wd_ref[0].astype(BF16)) + bd_ref[0]).astype(o_ref.dtype)


def _experts(tile_expert, xs, w_gu, b_glu, b_lin, w_down, b_down):
    p, d = xs.shape
    tm = MOE_TM
    ff = w_down.shape[1]
    by_expert = lambda shape: pl.BlockSpec((1,) + shape, lambda i, te: (te[i], 0, 0))
    grid_spec = pltpu.PrefetchScalarGridSpec(
        num_scalar_prefetch=1,
        grid=(p // tm,),
        in_specs=[pl.BlockSpec((tm, d), lambda i, te: (i, 0)),
                  by_expert((d, 2 * ff)), by_expert((1, ff)), by_expert((1, ff)),
                  by_expert((ff, d)), by_expert((1, d))],
        out_specs=pl.BlockSpec((tm, d), lambda i, te: (i, 0)),
    )
    return pl.pallas_call(
        _expert_kernel,
        grid_spec=grid_spec,
        out_shape=jax.ShapeDtypeStruct((p, d), BF16),
        compiler_params=_cparams(("arbitrary",)),
        name="moe_experts",
    )(tile_expert, xs, w_gu, b_glu, b_lin, w_down, b_down)


def _moe(h_f32, h_bf16, w_router, b_router, layer, w_gu, b_glu, b_lin, w_down, b_down):
    n, d = h_f32.shape
    tm = MOE_TM
    idx128, wt128, rank128, cnt = _router(h_f32, w_router, b_router)
    expert = idx128[:, :TOP_K]
    na = n * TOP_K
    p = na + N_EXPERTS * tm
    counts = cnt[0].astype(I32)
    padded = ((counts + tm - 1) // tm) * tm
    pend = jnp.cumsum(padded)
    pstart = pend - padded
    first = jnp.sum(jnp.where(expert[..., None] == jnp.arange(N_EXPERTS, dtype=I32), pstart, 0), axis=-1)
    slot = first + rank128[:, :TOP_K]
    token = jnp.arange(na, dtype=I32) // TOP_K
    row_token = (jnp.arange(p, dtype=I32) % n).at[slot.reshape(-1)].set(
        token, mode="promise_in_bounds", unique_indices=True)
    tile_start = jnp.arange(p // tm, dtype=I32) * tm
    tile_expert = jnp.minimum(jnp.sum((tile_start[:, None] >= pend[None, :]).astype(I32), axis=1), N_EXPERTS - 1)
    xs = jnp.take(h_bf16, row_token, axis=0, mode="clip")
    tile_info = jnp.concatenate([tile_expert + layer * N_EXPERTS, pend[-1:] // tm]).astype(I32)
    ys = _experts(tile_info, xs, w_gu, b_glu, b_lin, w_down, b_down)
    return [ys.at[slot[:, k]].get(mode="promise_in_bounds") for k in range(TOP_K)], wt128


def kernel(x, ln1_g, ln1_b, ln2_g, ln2_b, ev_w_in, ev_w_out, hg_lb_raw, hg_norm_g, od_w_in, od_w_out,
           cmp_k_w1, cmp_k_w2, cmp_k_pos, cmp_v_w1, cmp_v_w2, cmp_v_pos, rel_bias, router_w, router_b,
           exp_w_gu, exp_b_gu, exp_w_down, exp_b_down):
    b, t, d = x.shape
    n = b * t
    depth = ln1_g.shape[0]
    alpha = (2 * depth) ** 0.25
    lb_soft = jax.nn.softmax(hg_lb_raw.astype(F32), axis=0)
    lower_bounds = jnp.cumsum(lb_soft, axis=0) - lb_soft[0]
    pb, bnear, bfar, bpat = _nsa_bias_tables(rel_bias, t)

    n_all = depth * N_EXPERTS
    ff = exp_w_down.shape[2]
    wgu_all = _gu_prep(exp_w_gu.reshape(n_all, d, 2 * ff))
    wd_all = exp_w_down.reshape(n_all, ff, d)
    bg_all = exp_b_gu[..., 0::2].reshape(n_all, 1, ff)
    bl_all = exp_b_gu[..., 1::2].reshape(n_all, 1, ff)
    bd_all = exp_b_down.reshape(n_all, 1, d)

    h = x.reshape(n, d)
    hb = h.astype(BF16)
    for layer in range(depth):
        if layer % 2 == 0:
            e = layer // 2
            proj = _matmul(hb, ev_w_in, e)
            o_a = _sb_attention(proj, b, t)
            o_b = _hgrn2(proj, lower_bounds[layer], hg_norm_g[e], b, t)
            mix = _matmul2(o_a, o_b, ev_w_out, e)
        else:
            o = layer // 2
            proj = _matmul(hb, od_w_in, o, n=NSA_WIDTH + 6 * NSA_KV_WIDTH)
            gate_logits = _matmul(hb, od_w_in[o, :, NSA_WIDTH + 6 * NSA_KV_WIDTH:])
            proj3 = proj.reshape(b, t, -1)
            c0 = NSA_WIDTH // HEAD_DIM
            k_cmp = _compress(proj3, c0, cmp_k_w1[o], cmp_k_w2[o], cmp_k_pos[o])
            v_cmp = _compress(proj3, c0 + NSA_KV_GROUPS, cmp_v_w1[o], cmp_v_w2[o], cmp_v_pos[o])
            o_c, sel_t = _cmp_select(proj3, k_cmp, v_cmp, pb, b, t)
            sel = jnp.swapaxes(sel_t, 2, 3)
            o_s = _nsa_selected(proj3, c0 + 2 * NSA_KV_GROUPS, c0 + 3 * NSA_KV_GROUPS, bnear, bfar, sel, b, t)
            mixed = _nsa_window_mix(proj3, c0 + 4 * NSA_KV_GROUPS, c0 + 5 * NSA_KV_GROUPS, bpat,
                                    gate_logits, o_c, o_s, b, t)
            mix = _matmul(mixed.reshape(n, -1), od_w_out, o)
        h, hb = _add_ln(h, mix, ln1_g[layer], ln1_b[layer], alpha)
        ys, gate_w = _moe(h, hb, router_w[layer], router_b[layer], layer,
                          wgu_all, bg_all, bl_all, wd_all, bd_all)
        h, hb = _combine_ln(h, ys, gate_w, ln2_g[layer], ln2_b[layer], alpha)
    return h.reshape(b, t, d)
```

```python
import functools
import math

import jax
import jax.numpy as jnp
import numpy as np
from jax import lax
from jax.experimental import pallas as pl
from jax.experimental.pallas import tpu as pltpu

F32 = jnp.float32
BF16 = jnp.bfloat16
I32 = jnp.int32

HEAD_DIM = 128
SB_HEADS = 16
HG_HEADS = 16
SB_WIDTH = SB_HEADS * HEAD_DIM
HG_WIDTH = HG_HEADS * HEAD_DIM
F_MIN = 1e-6
NSA_HEADS = 32
NSA_KV_GROUPS = 4
NSA_HPG = NSA_HEADS // NSA_KV_GROUPS
NSA_WIDTH = NSA_HEADS * HEAD_DIM
NSA_KV_WIDTH = NSA_KV_GROUPS * HEAD_DIM
CMP_LEN = 32
CMP_STRIDE = 16
SLC_BLOCK = 64
N_SELECT = 16
WINDOW = 512
REL_BUCKETS = 32
REL_MAX_DIST = 128
N_EXPERTS = 32
TOP_K = 4
EXPERT_FF = 384
SWIGLU_LIMIT = 7.0
SWIGLU_ALPHA = 1.702
LN_EPS = 1e-5
NEG_INF = -1e30
FORCED = 1e9

LANES = 128
VMEM_LIMIT = 56 * 1024 * 1024

HI = lax.Precision.HIGHEST
LOG2E = 1.4426950408889634


def _cparams(sem):
    return pltpu.CompilerParams(dimension_semantics=sem, vmem_limit_bytes=VMEM_LIMIT)


def _dot_nt(a, b, **kw):
    return lax.dot_general(a, b, (((1,), (1,)), ((), ())), preferred_element_type=F32, **kw)


def _dot(a, b, **kw):
    return jnp.dot(a, b, preferred_element_type=F32, **kw)


def _mm_kernel(a_ref, w_ref, o_ref):
    o_ref[...] = _dot(a_ref[...], w_ref[...].astype(BF16)).astype(o_ref.dtype)


def _matmul(a, w, layer=None, *, n=None, tm=1024, tn=512, out_dtype=F32):
    m, k = a.shape
    n = w.shape[-1] if n is None else n
    tn = min(tn, n)
    assert m % tm == 0 and n % tn == 0
    if w.ndim == 3:
        w_spec = pl.BlockSpec((None, k, tn), lambda i, j: (layer, 0, j))
    else:
        w_spec = pl.BlockSpec((k, tn), lambda i, j: (0, j))
    return pl.pallas_call(
        _mm_kernel,
        grid=(m // tm, n // tn),
        in_specs=[pl.BlockSpec((tm, k), lambda i, j: (i, 0)), w_spec],
        out_specs=pl.BlockSpec((tm, tn), lambda i, j: (i, j)),
        out_shape=jax.ShapeDtypeStruct((m, n), out_dtype),
        compiler_params=_cparams(("parallel", "arbitrary")),
        name="matmul",
    )(a, w)


def _mm2_kernel(a1_ref, a2_ref, w1_ref, w2_ref, o_ref):
    o_ref[...] = (_dot(a1_ref[...], w1_ref[...].astype(BF16))
                  + _dot(a2_ref[...], w2_ref[...].astype(BF16)))


def _matmul2(a1, a2, w, layer, *, tm=1024, tn=512):
    m, k1 = a1.shape
    k2 = a2.shape[1]
    n = w.shape[-1]
    assert k1 == k2 and w.shape[1] == k1 + k2
    return pl.pallas_call(
        _mm2_kernel,
        grid=(m // tm, n // tn),
        in_specs=[pl.BlockSpec((tm, k1), lambda i, j: (i, 0)),
                  pl.BlockSpec((tm, k2), lambda i, j: (i, 0)),
                  pl.BlockSpec((None, k1, tn), lambda i, j: (layer, 0, j)),
                  pl.BlockSpec((None, k2, tn), lambda i, j: (layer, 1, j))],
        out_specs=pl.BlockSpec((tm, tn), lambda i, j: (i, j)),
        out_shape=jax.ShapeDtypeStruct((m, n), F32),
        compiler_params=_cparams(("parallel", "arbitrary")),
        name="matmul2",
    )(a1, a2, w, w)


def _ln_store(x, g_ref, b_ref, of_ref, ob_ref):
    mu = jnp.mean(x, axis=-1, keepdims=True)
    xc = x - mu
    var = jnp.mean(xc * xc, axis=-1, keepdims=True)
    out = xc * lax.rsqrt(var + LN_EPS) * g_ref[...] + b_ref[...]
    of_ref[...] = out
    ob_ref[...] = out.astype(BF16)


def _add_ln_kernel(h_ref, y_ref, g_ref, b_ref, of_ref, ob_ref, *, alpha):
    _ln_store(alpha * h_ref[...] + y_ref[...], g_ref, b_ref, of_ref, ob_ref)


def _combine_ln_kernel(h_ref, *refs, alpha):
    ys, (w_ref, g_ref, b_ref, of_ref, ob_ref) = refs[:TOP_K], refs[TOP_K:]
    w = w_ref[...]
    x = alpha * h_ref[...]
    for k in range(TOP_K):
        x = x + w[:, k:k + 1] * ys[k][...].astype(F32)
    _ln_store(x, g_ref, b_ref, of_ref, ob_ref)


def _combine_ln(h, ys, w128, g, b, alpha, *, tm=256):
    n, d = h.shape
    nt = n // tm
    row = pl.BlockSpec((tm, d), lambda i: (i, 0))
    vec = pl.BlockSpec((1, d), lambda i: (0, 0))
    part = lambda k: pl.BlockSpec((tm, d), lambda i: (k * nt + i, 0))
    return pl.pallas_call(
        functools.partial(_combine_ln_kernel, alpha=alpha),
        grid=(nt,),
        in_specs=[row] + [part(k) for k in range(TOP_K)] + [pl.BlockSpec((tm, LANES), lambda i: (i, 0)), vec, vec],
        out_specs=[row, row],
        out_shape=[jax.ShapeDtypeStruct((n, d), F32), jax.ShapeDtypeStruct((n, d), BF16)],
        compiler_params=_cparams(("parallel",)),
        name="moe_combine_ln",
    )(h, *([ys] * TOP_K), w128, g.reshape(1, d), b.reshape(1, d))


def _add_ln(h, y, g, b, alpha, *, tm=256):
    n, d = h.shape
    row = pl.BlockSpec((tm, d), lambda i: (i, 0))
    vec = pl.BlockSpec((1, d), lambda i: (0, 0))
    return pl.pallas_call(
        functools.partial(_add_ln_kernel, alpha=alpha),
        grid=(n // tm,),
        in_specs=[row, row, vec, vec],
        out_specs=[row, row],
        out_shape=[jax.ShapeDtypeStruct((n, d), F32), jax.ShapeDtypeStruct((n, d), BF16)],
        compiler_params=_cparams(("parallel",)),
        name="add_ln",
    )(h, y, g.reshape(1, d), b.reshape(1, d))


SB_TQ = 512
SB_KB = LANES
SB_DEAD = -104.0


def _sb_kernel(q_ref, k_ref, v_ref, u_ref, o_ref, kb_ref, vb_ref, acc_ref, carry_ref, *, tq, scale):
    i = pl.program_id(2)
    kb = SB_KB
    nb = tq // kb

    @pl.when(i == 0)
    def _():
        kb_ref[...] = k_ref[...].astype(BF16)
        vb_ref[...] = v_ref[...].astype(BF16)

    q = q_ref[...].astype(BF16)
    u = u_ref[...]
    acc_ref[...] = jnp.zeros_like(acc_ref)
    carry_ref[...] = jnp.zeros_like(carry_ref)

    def block_terms(z, causal=None):
        sp = jnp.log(1.0 + jnp.exp(-jnp.abs(z)))
        ls = jnp.minimum(z, 0.0) - sp
        lk = ls - z
        if causal is not None:
            lk = jnp.where(causal, lk, 0.0)
        hi = lk.astype(BF16)
        lo = (lk - hi.astype(F32)).astype(BF16)
        rt = _dot(jnp.concatenate([hi, lo], axis=1), u)
        return ls, rt[:, :kb] - lk, rt[:, kb:]

    base = pl.multiple_of(i * tq, tq)
    for j in reversed(range(nb)):
        r0 = j * kb
        rows = tq - r0
        kj = kb_ref[pl.ds(base + r0, kb), :]
        vj = vb_ref[pl.ds(base + r0, kb), :]
        z = _dot_nt(q[r0:], kj) * scale
        causal = lax.broadcasted_iota(I32, (rows, kb), 1) < lax.broadcasted_iota(I32, (rows, kb), 0)
        ls, between, tot = block_terms(z, causal)
        carry = carry_ref[r0:, :]
        w = jnp.where(causal, jnp.exp(ls + between + carry), 0.0)
        acc_ref[r0:, :] += _dot(w.astype(BF16), vj)
        carry_ref[r0:, :] = carry + tot

    def far(state):
        c, _ = state
        start = pl.multiple_of(c * tq, tq)
        z = _dot_nt(q, kb_ref[pl.ds(start, tq), :]) * scale
        terms = [block_terms(z[:, j * kb:(j + 1) * kb]) for j in range(nb)]
        carry = carry_ref[...]
        ws = [None] * nb
        for j in reversed(range(nb)):
            ls, between, tot = terms[j]
            ws[j] = jnp.exp(ls + between + carry).astype(BF16)
            carry = carry + tot
        acc_ref[...] += _dot(jnp.concatenate(ws, axis=1), vb_ref[pl.ds(start, tq), :])
        carry_ref[...] = carry
        return c - 1, jnp.max(carry)

    lax.while_loop(lambda st: (st[0] >= 0) & (st[1] > SB_DEAD), far, (i - 1, jnp.max(carry_ref[...])))
    o_ref[...] = acc_ref[...].astype(o_ref.dtype)


def _sb_attention(proj, b, t):
    tq, kb = min(SB_TQ, t), SB_KB
    nq = t // tq
    r = np.arange(2 * kb)[:, None] % kb
    c = np.arange(2 * kb)[None, :]
    u = jnp.asarray(np.where(c < kb, r >= c, True), BF16)
    kv_spec = lambda off: pl.BlockSpec((t, HEAD_DIM), lambda bi, h, i: (bi, off + h))
    return pl.pallas_call(
        functools.partial(_sb_kernel, tq=tq, scale=HEAD_DIM ** -0.5),
        grid=(b, SB_HEADS, nq),
        in_specs=[pl.BlockSpec((tq, HEAD_DIM), lambda bi, h, i: (bi * nq + i, h)),
                  kv_spec(SB_HEADS), kv_spec(2 * SB_HEADS),
                  pl.BlockSpec((2 * kb, 2 * kb), lambda bi, h, i: (0, 0))],
        out_specs=pl.BlockSpec((tq, HEAD_DIM), lambda bi, h, i: (bi * nq + i, h)),
        out_shape=jax.ShapeDtypeStruct((b * t, SB_WIDTH), BF16),
        scratch_shapes=[pltpu.VMEM((t, HEAD_DIM), BF16), pltpu.VMEM((t, HEAD_DIM), BF16),
                        pltpu.VMEM((tq, HEAD_DIM), F32), pltpu.VMEM((tq, kb), F32)],
        compiler_params=_cparams(("parallel", "parallel", "arbitrary")),
        name="sb_attention",
    )(proj, proj, proj, u)


HG_C = 128
HG_TB = 512
HG_NH = 8


def _hg_tables(c):
    levels = []
    m = c // 2
    while m >= 1:
        levels.append(m)
        m //= 2
    nl = len(levels)
    e = np.zeros((nl + 2, c, c), np.float32)
    mask = np.zeros((nl + 1, c, c), np.float32)
    p = np.arange(c)
    for li, m in enumerate(levels):
        blk = p // (2 * m)
        half = (p // m) % 2
        mid = blk * 2 * m + m - 1
        for r in range(c):
            if half[r] == 1:
                e[li, r, mid[r] + 1:r + 1] = 1.0
            else:
                e[li, r, r + 1:mid[r] + 1] = 1.0
        mask[li] = ((half[:, None] == 1) & (half[None, :] == 0) & (blk[:, None] == blk[None, :]))
    mask[nl] = np.eye(c)
    e[nl] = np.tril(np.ones((c, c)))
    e[nl + 1] = np.triu(np.ones((c, c)), 1)
    return e.reshape((nl + 2) * c, c), mask, nl


def _hg_kernel(qh_ref, fh_ref, ih_ref, gh_ref, lb_ref, ng_ref, e_ref, mask_ref, o_ref, st_ref,
               *, c, tb, nl, nh):
    @pl.when(pl.program_id(2) == 0)
    def _():
        st_ref[...] = jnp.zeros_like(st_ref)

    ng = ng_ref[...]
    emat = e_ref[...]

    def one_head(rows, hh):
        cols = slice(hh * HEAD_DIM, (hh + 1) * HEAD_DIM)
        lb = lb_ref[:, cols]
        qh = qh_ref[rows, cols]
        f = lb + (1.0 - lb) * jax.nn.sigmoid(fh_ref[rows, cols])
        g = jnp.log(jnp.maximum(f, F_MIN))
        kk = 1.0 - f
        q = qh * jax.nn.sigmoid(qh)
        v = ih_ref[rows, cols]
        vb = v.astype(BF16)
        g1 = g.astype(BF16)
        r1 = g - g1.astype(F32)
        g2 = r1.astype(BF16)
        g3 = (r1 - g2.astype(F32)).astype(BF16)
        ex = jnp.exp(_dot(emat, jnp.concatenate([g1, g2, g3], axis=0)))
        scores = mask_ref[nl] * _dot_nt(q.astype(BF16), kk.astype(BF16))
        for li in range(nl):
            a = ex[li * c:(li + 1) * c]
            scores = scores + mask_ref[li] * _dot_nt((q * a).astype(BF16), (kk * a).astype(BF16))
        ecum = ex[nl * c:(nl + 1) * c]
        erest = ex[(nl + 1) * c:(nl + 2) * c]
        st = st_ref[hh]
        o = _dot(scores.astype(BF16), vb) + _dot_nt((q * ecum).astype(BF16), st.astype(BF16))
        st_ref[hh] = ecum[c - 1:c, :] * st + _dot(v.T.astype(BF16), (kk * erest).astype(BF16))
        gh = gh_ref[rows, cols]
        o = o * lax.rsqrt(jnp.mean(o * o, axis=-1, keepdims=True) + LN_EPS) * ng
        o_ref[rows, cols] = (o * (gh * jax.nn.sigmoid(gh))).astype(o_ref.dtype)

    def chunk(ci, carry):
        rows = pl.ds(pl.multiple_of(ci * c, c), c)
        for hh in range(nh):
            one_head(rows, hh)
        return carry

    lax.fori_loop(0, tb // c, chunk, 0)


def _hgrn2(proj, lb, norm_g, b, t):
    c, tb, nh = HG_C, min(HG_TB, t), HG_NH
    e, mask, nl = _hg_tables(c)
    e3 = np.concatenate([e, e, e], axis=1)
    nt = t // tb
    base = 3 * SB_HEADS
    assert base % nh == 0 and HG_HEADS % nh == 0
    wide = nh * HEAD_DIM
    col = lambda k: pl.BlockSpec((tb, wide), lambda bi, h, i: (bi * nt + i, (base + k * HG_HEADS) // nh + h))
    const2 = lambda a: pl.BlockSpec(a.shape, lambda bi, h, i: (0,) * a.ndim)
    return pl.pallas_call(
        functools.partial(_hg_kernel, c=c, tb=tb, nl=nl, nh=nh),
        grid=(b, HG_HEADS // nh, nt),
        in_specs=[col(0), col(1), col(2), col(3),
                  pl.BlockSpec((1, wide), lambda bi, h, i: (0, h)),
                  pl.BlockSpec((1, HEAD_DIM), lambda bi, h, i: (0, 0)),
                  const2(e3), const2(mask)],
        out_specs=pl.BlockSpec((tb, wide), lambda bi, h, i: (bi * nt + i, h)),
        out_shape=jax.ShapeDtypeStruct((b * t, HG_WIDTH), BF16),
        scratch_shapes=[pltpu.VMEM((nh, HEAD_DIM, HEAD_DIM), F32)],
        compiler_params=_cparams(("parallel", "parallel", "arbitrary")),
        name="hgrn2",
    )(proj, proj, proj, proj, lb.reshape(1, HG_WIDTH), norm_g.reshape(1, HEAD_DIM),
      jnp.asarray(e3, BF16), jnp.asarray(mask))


def _cmp_kernel(a_ref, w1_ref, w2_ref, pos_ref, o_ref, *, ncp):
    half = CMP_LEN // 2
    p = jnp.zeros((ncp, HEAD_DIM), F32)
    q = jnp.zeros((ncp, HEAD_DIM), F32)
    for j in range(half):
        s = a_ref[pl.ds(j, ncp, stride=CMP_STRIDE), :]
        p = p + _dot((s + pos_ref[j:j + 1, :]).astype(BF16), w1_ref[j].astype(BF16))
        q = q + _dot((s + pos_ref[half + j:half + j + 1, :]).astype(BF16), w1_ref[half + j].astype(BF16))
    hid = p + pltpu.roll(q, ncp - 1, 0)
    hid = hid * jax.nn.sigmoid(hid)
    out = _dot(hid.astype(BF16), w2_ref[...].astype(BF16))
    row = lax.broadcasted_iota(I32, (ncp, HEAD_DIM), 0)
    o_ref[0, 0] = jnp.where(row < ncp - 1, out, 0.0)


def _compress(proj3, col0, w1, w2, pos):
    b, t, _ = proj3.shape
    ncp = t // CMP_STRIDE
    assert CMP_LEN == 2 * CMP_STRIDE
    return pl.pallas_call(
        functools.partial(_cmp_kernel, ncp=ncp),
        grid=(b, NSA_KV_GROUPS),
        in_specs=[pl.BlockSpec((None, t, HEAD_DIM), lambda bi, g: (bi, 0, col0 + g)),
                  pl.BlockSpec((CMP_LEN, HEAD_DIM, HEAD_DIM), lambda bi, g: (0, 0, 0)),
                  pl.BlockSpec((HEAD_DIM, HEAD_DIM), lambda bi, g: (0, 0)),
                  pl.BlockSpec((CMP_LEN, HEAD_DIM), lambda bi, g: (0, 0))],
        out_specs=pl.BlockSpec((1, 1, ncp, HEAD_DIM), lambda bi, g: (bi, g, 0, 0)),
        out_shape=jax.ShapeDtypeStruct((b, NSA_KV_GROUPS, ncp, HEAD_DIM), F32),
        compiler_params=_cparams(("parallel", "parallel")),
        name="nsa_compress",
    )(proj3, w1.reshape(CMP_LEN, HEAD_DIM, HEAD_DIM), w2, pos)


NSA_TQ = 128
CMP_NEAR = 16
CMP_NEAR_LO = 9


def _rel_bucket_np(dist):
    dist = np.maximum(dist, 0)
    max_exact = REL_BUCKETS // 2
    ratio = (np.log(np.maximum(dist, max_exact).astype(np.float32) / np.float32(max_exact))
             / np.float32(math.log(REL_MAX_DIST / max_exact)))
    large = np.minimum(max_exact + (ratio * np.float32(REL_BUCKETS - max_exact)).astype(np.int32),
                       REL_BUCKETS - 1)
    return np.where(dist < max_exact, dist, large).astype(np.int32)


def _cmpsel_kernel(q_ref, kc_ref, vc_ref, pb_ref, ovl_ref, oc_ref, sel_ref,
                   *, tq, ncp, nslc, nsel, scale):
    i = pl.program_id(2)
    kc = kc_ref[0, 0].astype(BF16)
    vc = vc_ref[0, 0].astype(BF16)
    qpos = i * tq + lax.broadcasted_iota(I32, (tq, ncp), 0)
    ncol = lax.broadcasted_iota(I32, (tq, ncp), 1)
    valid = (ncol * CMP_STRIDE + (CMP_LEN - 1) <= qpos) & (ncol < ncp - 1)
    sr = lax.broadcasted_iota(I32, (LANES, ncp), 0)
    sc = lax.broadcasted_iota(I32, (LANES, ncp), 1)
    first = i * (tq // CMP_STRIDE) - CMP_NEAR_LO
    place = jnp.where(sr < CMP_NEAR, jnp.where(sc == first + sr, 1.0, 0.0),
                      jnp.where(sr == CMP_NEAR, jnp.where(sc < first, 1.0, 0.0), 0.0))
    place = place.astype(BF16)
    bias = _dot(pb_ref[0], jnp.concatenate([place, place], axis=0))
    psum = jnp.zeros((tq, ncp), F32)
    for r in range(NSA_HPG):
        hs = slice(r * HEAD_DIM, (r + 1) * HEAD_DIM)
        s = _dot_nt(q_ref[0, :, hs].astype(BF16), kc) * scale + bias[r * tq:(r + 1) * tq]
        s = jnp.where(valid, s, NEG_INF)
        mx = jnp.max(s, axis=-1, keepdims=True)
        e = jnp.where(valid, jnp.exp(s - mx), 0.0)
        den = jnp.sum(e, axis=-1, keepdims=True)
        p = e / jnp.maximum(den, 1e-30)
        oc_ref[0, :, hs] = _dot(p.astype(BF16), vc)
        psum = psum + p
    imp = _dot_nt(ovl_ref[...], psum, precision=HI)
    jrow = lax.broadcasted_iota(I32, (nslc, tq), 0)
    qblk = (i * tq + lax.broadcasted_iota(I32, (nslc, tq), 1)) // SLC_BLOCK
    ok = jrow <= qblk
    forced = (jrow == 0) | (jrow == qblk) | (jrow == qblk - 1)
    imp = jnp.where(ok, jnp.where(forced, FORCED, imp), NEG_INF)
    rank = jnp.zeros((nslc, tq), F32)
    for j2 in range(nslc):
        row = imp[j2:j2 + 1, :]
        tie = jnp.where(jrow > j2, 1.0, 0.0)
        rank = rank + jnp.where(row > imp, 1.0, jnp.where(row == imp, tie, 0.0))
    sel_ref[0, 0] = jnp.where((rank < nsel) & ok, 1.0, 0.0)


def _cmp_select(proj3, k_cmp, v_cmp, pb, b, t):
    tq = NSA_TQ
    ncp = t // CMP_STRIDE
    nslc = t // SLC_BLOCK
    nsel = min(N_SELECT, nslc)
    n_idx = np.arange(ncp)
    slc_start = np.arange(nslc) * SLC_BLOCK
    cs = n_idx * CMP_STRIDE
    ovl = ((cs[None, :] < slc_start[:, None] + SLC_BLOCK)
           & (cs[None, :] + CMP_LEN - 1 >= slc_start[:, None])
           & (n_idx[None, :] < ncp - 1)).astype(np.float32)
    gw = NSA_HPG * HEAD_DIM
    return pl.pallas_call(
        functools.partial(_cmpsel_kernel, tq=tq, ncp=ncp, nslc=nslc, nsel=nsel, scale=HEAD_DIM ** -0.5),
        grid=(b, NSA_KV_GROUPS, t // tq),
        in_specs=[pl.BlockSpec((1, tq, gw), lambda bi, g, i: (bi, i, g)),
                  pl.BlockSpec((1, 1, ncp, HEAD_DIM), lambda bi, g, i: (bi, g, 0, 0)),
                  pl.BlockSpec((1, 1, ncp, HEAD_DIM), lambda bi, g, i: (bi, g, 0, 0)),
                  pl.BlockSpec((1, NSA_HPG * tq, 2 * LANES), lambda bi, g, i: (g, 0, 0)),
                  pl.BlockSpec((nslc, ncp), lambda bi, g, i: (0, 0))],
        out_specs=[pl.BlockSpec((1, tq, gw), lambda bi, g, i: (bi, i, g)),
                   pl.BlockSpec((1, 1, nslc, tq), lambda bi, g, i: (bi, g, 0, i))],
        out_shape=[jax.ShapeDtypeStruct((b, t, NSA_WIDTH), F32),
                   jax.ShapeDtypeStruct((b, NSA_KV_GROUPS, nslc, t), F32)],
        compiler_params=_cparams(("parallel", "parallel", "parallel")),
        name="nsa_cmp_select",
    )(proj3, k_cmp, v_cmp, pb, jnp.asarray(ovl))


SEL_CHUNK = 512
SEL_VEXT = 16
MASKED = -1e30
FAR_BIAS_COL = SLC_BLOCK


def _softmax_step(st, vt, m_ref, acc_ref, c1):
    m_old = m_ref[...]
    m_new = jnp.maximum(m_old, jnp.max(st, axis=0, keepdims=True))
    p = jnp.exp2((st - m_new) * c1)
    acc_ref[...] = jnp.exp2((m_old - m_new) * c1) * acc_ref[...] + _dot(vt, p.astype(BF16))
    m_ref[...] = m_new


def _sel_kernel(q_ref, k_ref, v_ref, kext_ref, vext_ref, bnear_ref, bfar_ref, sel_ref, o_ref,
                qa_ref, ka_ref, vt_ref, m_ref, acc_ref, s0_ref, s1_ref, *, tq, c1):
    i = pl.program_id(2)
    hpg = NSA_HPG
    d = HEAD_DIM

    @pl.when(i == 0)
    def _():
        ka_ref[:, :d] = k_ref[...].astype(BF16)
        ka_ref[:, d:] = kext_ref[...]
        for c in range(vt_ref.shape[0]):
            vt_ref[c, :d, :] = v_ref[c * SEL_CHUNK:(c + 1) * SEL_CHUNK, :].T.astype(BF16)
            vt_ref[c, d:, :] = vext_ref[...]

    sel = sel_ref[0, 0]
    nslc = sel.shape[1]
    blk = lax.broadcasted_iota(I32, (tq, nslc), 1)
    far_blocks = (i - 1) * (tq // SLC_BLOCK)
    maskq = jnp.where(blk < far_blocks, jnp.where(sel > 0.5, 0.0, MASKED), MASKED)
    ext = jnp.concatenate([maskq, jnp.zeros((tq, LANES - nslc), F32)], axis=1)
    for r in range(hpg):
        qa_ref[r * tq:(r + 1) * tq, :d] = q_ref[0, :, r * d:(r + 1) * d].astype(BF16)
        qa_ref[r * tq:(r + 1) * tq, d:] = (ext + bfar_ref[0, r:r + 1, :]).astype(BF16)
    m_ref[...] = jnp.full_like(m_ref, MASKED)
    acc_ref[...] = jnp.zeros_like(acc_ref)

    n_chunks = vt_ref.shape[0]

    def logits(c):
        start = pl.multiple_of(jnp.minimum(c, n_chunks - 1) * SEL_CHUNK, SEL_CHUNK)
        return _dot_nt(ka_ref[pl.ds(start, SEL_CHUNK), :], qa_ref[...])

    n_far = (jnp.maximum(i - 1, 0) * tq + SEL_CHUNK - 1) // SEL_CHUNK
    n_pairs = n_far // 2

    @pl.when(n_far > 0)
    def _():
        s0_ref[...] = logits(0)

    def pair(j, carry):
        c0 = 2 * j
        s1_ref[...] = logits(c0 + 1)
        _softmax_step(s0_ref[...], vt_ref[c0], m_ref, acc_ref, c1)
        s0_ref[...] = logits(c0 + 2)
        _softmax_step(s1_ref[...], vt_ref[c0 + 1], m_ref, acc_ref, c1)
        return carry

    lax.fori_loop(0, n_pairs, pair, 0)

    @pl.when(n_far % 2 == 1)
    def _():
        _softmax_step(s0_ref[...], vt_ref[n_far - 1], m_ref, acc_ref, c1)

    p0 = pl.multiple_of(jnp.maximum(i - 1, 0) * tq, tq)
    p1 = pl.multiple_of(i * tq, tq)
    kn = jnp.concatenate([ka_ref[pl.ds(p0, tq), :d], ka_ref[pl.ds(p1, tq), :d]], axis=0)
    vn = jnp.concatenate([v_ref[pl.ds(p0, tq), :], v_ref[pl.ds(p1, tq), :]], axis=0)
    vnt = jnp.concatenate([vn.T.astype(BF16), vext_ref[:, :2 * tq]], axis=0)
    er = lax.broadcasted_iota(I32, (2 * tq, nslc), 1)
    ec = lax.broadcasted_iota(I32, (2 * tq, nslc), 0) // SLC_BLOCK
    expand = jnp.where(er == far_blocks + ec, 1.0, 0.0).astype(BF16)
    picked = _dot_nt(expand, sel.astype(BF16))
    kj = lax.broadcasted_iota(I32, (2 * tq, tq), 0)
    qi = lax.broadcasted_iota(I32, (2 * tq, tq), 1)
    keep = jnp.where(kj <= qi + tq, picked, 0.0)
    keep = jnp.concatenate([keep] * hpg, axis=1) > 0.5
    st = _dot_nt(kn, qa_ref[:, :d]) + bnear_ref[0]
    _softmax_step(jnp.where(keep, st, MASKED), vnt, m_ref, acc_ref, c1)

    acc = acc_ref[...]
    out = acc[:d, :] / acc[d:d + 1, :]
    for r in range(hpg):
        o_ref[0, :, r * d:(r + 1) * d] = out[:, r * tq:(r + 1) * tq].T


def _nsa_selected(proj3, kcol, vcol, bnear, bfar, sel, b, t):
    tq = NSA_TQ
    d = HEAD_DIM
    gw = NSA_HPG * d
    nslc = t // SLC_BLOCK
    assert nslc <= FAR_BIAS_COL and t % SEL_CHUNK == 0
    rows = NSA_HPG * tq
    pos = np.arange(t)
    kext = np.zeros((t, LANES), np.float32)
    kext[pos, pos // SLC_BLOCK] = 1.0
    kext[:, FAR_BIAS_COL:FAR_BIAS_COL + 2] = 1.0
    vext = np.zeros((SEL_VEXT, SEL_CHUNK), np.float32)
    vext[0, :] = 1.0
    kv = lambda col: pl.BlockSpec((None, t, d), lambda bi, g, i: (bi, 0, col + g))
    const = lambda shape: pl.BlockSpec(shape, lambda bi, g, i: (0,) * len(shape))
    return pl.pallas_call(
        functools.partial(_sel_kernel, tq=tq, c1=d ** -0.5 * LOG2E),
        grid=(b, NSA_KV_GROUPS, t // tq),
        in_specs=[pl.BlockSpec((1, tq, gw), lambda bi, g, i: (bi, i, g)),
                  kv(kcol), kv(vcol), const((t, LANES)), const((SEL_VEXT, SEL_CHUNK)),
                  pl.BlockSpec((1, 2 * tq, rows), lambda bi, g, i: (g, 0, 0)),
                  pl.BlockSpec((1, NSA_HPG, LANES), lambda bi, g, i: (g, 0, 0)),
                  pl.BlockSpec((1, 1, tq, nslc), lambda bi, g, i: (bi, g, i, 0))],
        out_specs=pl.BlockSpec((1, tq, gw), lambda bi, g, i: (bi, i, g)),
        out_shape=jax.ShapeDtypeStruct((b, t, NSA_WIDTH), F32),
        scratch_shapes=[pltpu.VMEM((rows, 2 * d), BF16),
                        pltpu.VMEM((t, 2 * d), BF16),
                        pltpu.VMEM((t // SEL_CHUNK, d + SEL_VEXT, SEL_CHUNK), BF16),
                        pltpu.VMEM((1, rows), F32),
                        pltpu.VMEM((d + SEL_VEXT, rows), F32),
                        pltpu.VMEM((SEL_CHUNK, rows), F32),
                        pltpu.VMEM((SEL_CHUNK, rows), F32)],
        compiler_params=_cparams(("parallel", "parallel", "arbitrary")),
        name="nsa_selected",
    )(proj3, proj3, proj3, jnp.asarray(kext, BF16), jnp.asarray(vext, BF16),
      jnp.swapaxes(bnear, 1, 2), bfar, sel)


def _win_kernel(q_ref, k_ref, v_ref, bpat_ref, g_ref, oc_ref, os_ref, o_ref, kb_ref, vb_ref,
                *, tq, window, c1):
    i = pl.program_id(2)
    hpg = NSA_HPG
    d = HEAD_DIM
    nt = window // tq + 1

    @pl.when(i == 0)
    def _():
        kb_ref[...] = k_ref[...].astype(BF16)
        vb_ref[...] = v_ref[...].astype(BF16)

    qs = jnp.concatenate([q_ref[0, :, r * d:(r + 1) * d] for r in range(hpg)], axis=0).astype(BF16)
    t0 = jnp.maximum(i - (nt - 1), 0)
    qi = lax.broadcasted_iota(I32, (tq, tq), 0)
    kj = lax.broadcasted_iota(I32, (tq, tq), 1)
    parts = []
    for kk in range(nt):
        dt = i - (t0 + kk)
        start = pl.multiple_of((t0 + kk) * tq, tq)
        z = _dot_nt(qs, kb_ref[pl.ds(start, tq), :]) + bpat_ref[0, jnp.clip(dt, 0, 2)]
        dist = dt * tq + qi - kj
        keep = jnp.where(dist >= 0, jnp.where(dist < window, 1.0, 0.0), 0.0) > 0.5
        parts.append(jnp.where(keep[None], z.reshape(hpg, tq, tq), MASKED))
    s = jnp.concatenate(parts, axis=-1)
    m = jnp.max(s, axis=-1, keepdims=True)
    p = jnp.exp2((s - m) * c1)
    den = jnp.sum(p, axis=-1, keepdims=True)
    vwin = vb_ref[pl.ds(pl.multiple_of(t0 * tq, tq), nt * tq), :]
    out = (_dot(p.reshape(hpg * tq, nt * tq).astype(BF16), vwin).reshape(hpg, tq, d) / den)
    gate = jax.nn.sigmoid(g_ref[0, 0])
    for r in range(hpg):
        hs = slice(r * d, (r + 1) * d)
        mix = (gate[:, 3 * r:3 * r + 1] * oc_ref[0, :, hs] + gate[:, 3 * r + 1:3 * r + 2] * os_ref[0, :, hs]
               + gate[:, 3 * r + 2:3 * r + 3] * out[r])
        o_ref[0, :, hs] = mix.astype(o_ref.dtype)


def _nsa_window_mix(proj3, kcol, vcol, bpat, gate_logits, o_c, o_s, b, t):
    tq = NSA_TQ
    d = HEAD_DIM
    gw = NSA_HPG * d
    ng = 3 * NSA_HPG
    assert WINDOW % tq == 0 and t >= WINDOW + tq
    gates = jnp.transpose(gate_logits.reshape(b, t, NSA_KV_GROUPS, ng), (0, 2, 1, 3))
    kv = lambda col: pl.BlockSpec((None, t, d), lambda bi, g, i: (bi, 0, col + g))
    tile = pl.BlockSpec((1, tq, gw), lambda bi, g, i: (bi, i, g))
    return pl.pallas_call(
        functools.partial(_win_kernel, tq=tq, window=WINDOW, c1=d ** -0.5 * LOG2E),
        grid=(b, NSA_KV_GROUPS, t // tq),
        in_specs=[tile, kv(kcol), kv(vcol),
                  pl.BlockSpec((1, 3, NSA_HPG * tq, tq), lambda bi, g, i: (g, 0, 0, 0)),
                  pl.BlockSpec((1, 1, tq, ng), lambda bi, g, i: (bi, g, i, 0)),
                  tile, tile],
        out_specs=tile,
        out_shape=jax.ShapeDtypeStruct((b, t, NSA_WIDTH), BF16),
        scratch_shapes=[pltpu.VMEM((t, d), BF16), pltpu.VMEM((t, d), BF16)],
        compiler_params=_cparams(("parallel", "parallel", "arbitrary")),
        name="nsa_window_mix",
    )(proj3, proj3, proj3, bpat, gates, o_c, o_s)


def _nsa_bias_tables(rel_bias, t):
    tq = NSA_TQ
    g, hpg = NSA_KV_GROUPS, NSA_HPG
    inv_scale = HEAD_DIM ** 0.5
    tab = rel_bias.astype(F32)
    last = REL_BUCKETS - 1
    far_from = int(np.nonzero(_rel_bucket_np(np.arange(4 * REL_MAX_DIST)) < last)[0].max()) + 1
    assert (_rel_bucket_np(np.arange(far_from, t + tq)) == last).all()

    def by_group(a):
        a = jnp.moveaxis(a, -1, 0)
        return a.reshape((g, hpg) + a.shape[1:])

    iq = np.arange(tq)[:, None]
    m = np.arange(CMP_NEAR)[None, :]
    dist_c = iq - CMP_STRIDE * (m - CMP_NEAR_LO) - (CMP_LEN - 1)
    assert dist_c[:, 0].min() >= far_from
    assert (iq - CMP_STRIDE * (CMP_NEAR - CMP_NEAR_LO) - (CMP_LEN - 1)).max() < 0
    pb = jnp.zeros((g, hpg, tq, LANES), F32)
    pb = pb.at[..., :CMP_NEAR].set(by_group(tab[_rel_bucket_np(dist_c)]))
    pb = pb.at[..., CMP_NEAR].set(by_group(tab[last])[..., None])
    pb = pb.reshape(g, hpg * tq, LANES)
    pb_hi = pb.astype(BF16)
    pb = jnp.concatenate([pb_hi, (pb - pb_hi.astype(F32)).astype(BF16)], axis=-1)
    assert 2 * tq - (tq - 1) >= far_from
    jk = np.arange(tq)[None, :]
    idx = np.stack([_rel_bucket_np(iq - jk), _rel_bucket_np(tq + iq - jk), np.full((tq, tq), last, np.int32)])
    pat = by_group(tab[idx]) * inv_scale
    bpat = jnp.transpose(pat, (0, 2, 1, 3, 4)).reshape(g, 3, hpg * tq, tq)
    bnear = jnp.concatenate([pat[:, :, 1], pat[:, :, 0]], axis=-1).reshape(g, hpg * tq, 2 * tq)
    far = by_group(tab[last]) * inv_scale
    far_hi = far.astype(BF16).astype(F32)
    bfar = jnp.zeros((g, hpg, LANES), F32)
    bfar = bfar.at[..., FAR_BIAS_COL].set(far_hi).at[..., FAR_BIAS_COL + 1].set(far - far_hi)
    return pb, bnear, bfar, bpat


def _router_kernel(h_ref, w_ref, b_ref, tri_ref, idx_ref, wt_ref, rank_ref, cnt_ref, *, tm):
    @pl.when(pl.program_id(0) == 0)
    def _():
        cnt_ref[...] = jnp.zeros_like(cnt_ref)

    h = h_ref[...]
    h_hi = h.astype(BF16)
    h_lo = (h - h_hi.astype(F32)).astype(BF16)
    logits = (_dot(h_hi, w_ref[0]) + _dot(h_lo, w_ref[0]) + _dot(h_hi, w_ref[1])) + b_ref[...]
    lane = lax.broadcasted_iota(I32, (tm, N_EXPERTS), 1).astype(F32)
    out_lane = lax.broadcasted_iota(I32, (tm, LANES), 1)
    work = logits
    idx_out = jnp.zeros((tm, LANES), F32)
    val_out = jnp.zeros((tm, LANES), F32)
    top = None
    den = jnp.zeros((tm, 1), F32)
    hots = []
    for k in range(TOP_K):
        mx = jnp.max(work, axis=-1, keepdims=True)
        idx = jnp.min(jnp.where(work == mx, lane, float(N_EXPERTS)), axis=-1, keepdims=True)
        if top is None:
            top = mx
        e = jnp.exp(mx - top)
        den = den + e
        idx_out = jnp.where(out_lane == k, idx, idx_out)
        val_out = jnp.where(out_lane == k, e, val_out)
        hots.append(jnp.where(lane == idx, 1.0, 0.0))
        work = jnp.where(lane == idx, -jnp.inf, work)
    idx_ref[...] = idx_out.astype(I32)
    wt_ref[...] = val_out / den
    hot = hots[0] + hots[1] + hots[2] + hots[3]
    before = cnt_ref[...] + _dot(tri_ref[...], hot.astype(BF16))
    rank_out = jnp.zeros((tm, LANES), F32)
    for k in range(TOP_K):
        rank_out = jnp.where(out_lane == k, jnp.sum(hots[k] * before, axis=-1, keepdims=True), rank_out)
    rank_ref[...] = rank_out.astype(I32)
    cnt_ref[...] += jnp.sum(hot, axis=0, keepdims=True)


def _router(h, w, bias, *, tm=512):
    n, d = h.shape
    out = pl.BlockSpec((tm, LANES), lambda i: (i, 0))
    tri = jnp.asarray(np.tril(np.ones((tm, tm), np.float32), -1), BF16)
    w_hi = w.astype(BF16)
    return pl.pallas_call(
        functools.partial(_router_kernel, tm=tm),
        grid=(n // tm,),
        in_specs=[pl.BlockSpec((tm, d), lambda i: (i, 0)),
                  pl.BlockSpec((2, d, N_EXPERTS), lambda i: (0, 0, 0)),
                  pl.BlockSpec((1, N_EXPERTS), lambda i: (0, 0)),
                  pl.BlockSpec((tm, tm), lambda i: (0, 0))],
        out_specs=[out, out, out, pl.BlockSpec((1, N_EXPERTS), lambda i: (0, 0))],
        out_shape=[jax.ShapeDtypeStruct((n, LANES), I32), jax.ShapeDtypeStruct((n, LANES), F32),
                   jax.ShapeDtypeStruct((n, LANES), I32), jax.ShapeDtypeStruct((1, N_EXPERTS), F32)],
        compiler_params=_cparams(("arbitrary",)),
        name="moe_router",
    )(h, jnp.stack([w_hi, (w - w_hi.astype(F32)).astype(BF16)]), bias.reshape(1, N_EXPERTS), tri)


GU_BLK = 2 * LANES


def _gu_prep_kernel(w_ref, p_ref, o_ref):
    for c in range(w_ref.shape[-1] // GU_BLK):
        cols = slice(c * GU_BLK, (c + 1) * GU_BLK)
        o_ref[0, :, cols] = _dot(w_ref[0, :, cols].astype(BF16), p_ref[...]).astype(BF16)


def _gu_prep(w_gu, *, tk=4096):
    e, d, f2 = w_gu.shape
    assert f2 % GU_BLK == 0
    k = np.arange(GU_BLK)
    perm = np.zeros((GU_BLK, GU_BLK), np.float32)
    perm[k, (k % 2) * LANES + k // 2] = 1.0
    return pl.pallas_call(
        _gu_prep_kernel,
        grid=(e, d // tk),
        in_specs=[pl.BlockSpec((1, tk, f2), lambda ei, ki: (ei, ki, 0)),
                  pl.BlockSpec((GU_BLK, GU_BLK), lambda ei, ki: (0, 0))],
        out_specs=pl.BlockSpec((1, tk, f2), lambda ei, ki: (ei, ki, 0)),
        out_shape=jax.ShapeDtypeStruct((e, d, f2), BF16),
        compiler_params=_cparams(("parallel", "parallel")),
        name="moe_gu_prep",
    )(w_gu, jnp.asarray(perm, BF16))


MOE_TM = 512


def _expert_kernel(te_ref, x_ref, wgu_ref, bg_ref, bl_ref, wd_ref, bd_ref, o_ref):
    @pl.when(pl.program_id(0) < te_ref[pl.num_programs(0)])
    def _():
        hgu = _dot(x_ref[...], wgu_ref[0])
        nblk = hgu.shape[1] // GU_BLK
        glu = jnp.concatenate([hgu[:, c * GU_BLK:c * GU_BLK + LANES] for c in range(nblk)], axis=1)
        lin = jnp.concatenate([hgu[:, c * GU_BLK + LANES:(c + 1) * GU_BLK] for c in range(nblk)], axis=1)
        glu = jnp.minimum(glu + bg_ref[0], SWIGLU_LIMIT)
        lin = jnp.clip(lin + bl_ref[0], -SWIGLU_LIMIT, SWIGLU_LIMIT)
        act = glu * jax.nn.sigmoid(SWIGLU_ALPHA * glu) * (lin + 1.0)
        o_ref[...] = (_dot(act.astype(BF16), wd_ref[0].astype(BF16)) + bd_ref[0]).astype(o_ref.dtype)


def _experts(tile_expert, xs, w_gu, b_glu, b_lin, w_down, b_down):
    p, d = xs.shape
    tm = MOE_TM
    ff = w_down.shape[1]
    by_expert = lambda shape: pl.BlockSpec((1,) + shape, lambda i, te: (te[i], 0, 0))
    grid_spec = pltpu.PrefetchScalarGridSpec(
        num_scalar_prefetch=1,
        grid=(p // tm,),
        in_specs=[pl.BlockSpec((tm, d), lambda i, te: (i, 0)),
                  by_expert((d, 2 * ff)), by_expert((1, ff)), by_expert((1, ff)),
                  by_expert((ff, d)), by_expert((1, d))],
        out_specs=pl.BlockSpec((tm, d), lambda i, te: (i, 0)),
    )
    return pl.pallas_call(
        _expert_kernel,
        grid_spec=grid_spec,
        out_shape=jax.ShapeDtypeStruct((p, d), BF16),
        compiler_params=_cparams(("arbitrary",)),
        name="moe_experts",
    )(tile_expert, xs, w_gu, b_glu, b_lin, w_down, b_down)


def _moe(h_f32, h_bf16, w_router, b_router, layer, w_gu, b_glu, b_lin, w_down, b_down):
    n, d = h_f32.shape
    tm = MOE_TM
    idx128, wt128, rank128, cnt = _router(h_f32, w_router, b_router)
    expert = idx128[:, :TOP_K]
    na = n * TOP_K
    p = na + N_EXPERTS * tm
    counts = cnt[0].astype(I32)
    padded = ((counts + tm - 1) // tm) * tm
    pend = jnp.cumsum(padded)
    pstart = pend - padded
    first = jnp.sum(jnp.where(expert[..., None] == jnp.arange(N_EXPERTS, dtype=I32), pstart, 0), axis=-1)
    slot = first + rank128[:, :TOP_K]
    token = jnp.arange(na, dtype=I32) // TOP_K
    row_token = (jnp.arange(p, dtype=I32) % n).at[slot.reshape(-1)].set(
        token, mode="promise_in_bounds", unique_indices=True)
    tile_start = jnp.arange(p // tm, dtype=I32) * tm
    tile_expert = jnp.minimum(jnp.sum((tile_start[:, None] >= pend[None, :]).astype(I32), axis=1), N_EXPERTS - 1)
    xs = jnp.take(h_bf16, row_token, axis=0, mode="clip")
    tile_info = jnp.concatenate([tile_expert + layer * N_EXPERTS, pend[-1:] // tm]).astype(I32)
    ys = _experts(tile_info, xs, w_gu, b_glu, b_lin, w_down, b_down)
    return ys.at[slot.T.reshape(-1)].get(mode="promise_in_bounds"), wt128


def kernel(x, ln1_g, ln1_b, ln2_g, ln2_b, ev_w_in, ev_w_out, hg_lb_raw, hg_norm_g, od_w_in, od_w_out,
           cmp_k_w1, cmp_k_w2, cmp_k_pos, cmp_v_w1, cmp_v_w2, cmp_v_pos, rel_bias, router_w, router_b,
           exp_w_gu, exp_b_gu, exp_w_down, exp_b_down):
    b, t, d = x.shape
    n = b * t
    depth = ln1_g.shape[0]
    alpha = (2 * depth) ** 0.25
    lb_soft = jax.nn.softmax(hg_lb_raw.astype(F32), axis=0)
    lower_bounds = jnp.cumsum(lb_soft, axis=0) - lb_soft[0]
    pb, bnear, bfar, bpat = _nsa_bias_tables(rel_bias, t)

    n_all = depth * N_EXPERTS
    ff = exp_w_down.shape[2]
    wgu_all = _gu_prep(exp_w_gu.reshape(n_all, d, 2 * ff))
    wd_all = exp_w_down.reshape(n_all, ff, d)
    bg_all = exp_b_gu[..., 0::2].reshape(n_all, 1, ff)
    bl_all = exp_b_gu[..., 1::2].reshape(n_all, 1, ff)
    bd_all = exp_b_down.reshape(n_all, 1, d)

    h = x.reshape(n, d)
    hb = h.astype(BF16)
    for layer in range(depth):
        if layer % 2 == 0:
            e = layer // 2
            proj = _matmul(hb, ev_w_in, e)
            o_a = _sb_attention(proj, b, t)
            o_b = _hgrn2(proj, lower_bounds[layer], hg_norm_g[e], b, t)
            mix = _matmul2(o_a, o_b, ev_w_out, e)
        else:
            o = layer // 2
            proj = _matmul(hb, od_w_in, o, n=NSA_WIDTH + 6 * NSA_KV_WIDTH)
            gate_logits = _matmul(hb, od_w_in[o, :, NSA_WIDTH + 6 * NSA_KV_WIDTH:])
            proj3 = proj.reshape(b, t, -1)
            c0 = NSA_WIDTH // HEAD_DIM
            k_cmp = _compress(proj3, c0, cmp_k_w1[o], cmp_k_w2[o], cmp_k_pos[o])
            v_cmp = _compress(proj3, c0 + NSA_KV_GROUPS, cmp_v_w1[o], cmp_v_w2[o], cmp_v_pos[o])
            o_c, sel_t = _cmp_select(proj3, k_cmp, v_cmp, pb, b, t)
            sel = jnp.swapaxes(sel_t, 2, 3)
            o_s = _nsa_selected(proj3, c0 + 2 * NSA_KV_GROUPS, c0 + 3 * NSA_KV_GROUPS, bnear, bfar, sel, b, t)
            mixed = _nsa_window_mix(proj3, c0 + 4 * NSA_KV_GROUPS, c0 + 5 * NSA_KV_GROUPS, bpat,
                                    gate_logits, o_c, o_s, b, t)
            mix = _matmul(mixed.reshape(n, -1), od_w_out, o)
        h, hb = _add_ln(h, mix, ln1_g[layer], ln1_b[layer], alpha)
        ys, gate_w = _moe(h, hb, router_w[layer], router_b[layer], layer,
                          wgu_all, bg_all, bl_all, wd_all, bd_all)
        h, hb = _combine_ln(h, ys, gate_w, ln2_g[layer], ln2_b[layer], alpha)
    return h.reshape(b, t, d)
```

```python
import functools
import math

import jax
import jax.numpy as jnp
import numpy as np
from jax import lax
from jax.experimental import pallas as pl
from jax.experimental.pallas import tpu as pltpu

F32 = jnp.float32
BF16 = jnp.bfloat16
I32 = jnp.int32

HEAD_DIM = 128
SB_HEADS = 16
HG_HEADS = 16
SB_WIDTH = SB_HEADS * HEAD_DIM
HG_WIDTH = HG_HEADS * HEAD_DIM
F_MIN = 1e-6
NSA_HEADS = 32
NSA_KV_GROUPS = 4
NSA_HPG = NSA_HEADS // NSA_KV_GROUPS
NSA_WIDTH = NSA_HEADS * HEAD_DIM
NSA_KV_WIDTH = NSA_KV_GROUPS * HEAD_DIM
CMP_LEN = 32
CMP_STRIDE = 16
SLC_BLOCK = 64
N_SELECT = 16
WINDOW = 512
REL_BUCKETS = 32
REL_MAX_DIST = 128
N_EXPERTS = 32
TOP_K = 4
EXPERT_FF = 384
SWIGLU_LIMIT = 7.0
SWIGLU_ALPHA = 1.702
LN_EPS = 1e-5
NEG_INF = -1e30
FORCED = 1e9

LANES = 128
VMEM_LIMIT = 56 * 1024 * 1024

HI = lax.Precision.HIGHEST
LOG2E = 1.4426950408889634


def _cparams(sem):
    return pltpu.CompilerParams(dimension_semantics=sem, vmem_limit_bytes=VMEM_LIMIT)


def _dot_nt(a, b, **kw):
    return lax.dot_general(a, b, (((1,), (1,)), ((), ())), preferred_element_type=F32, **kw)


def _dot(a, b, **kw):
    return jnp.dot(a, b, preferred_element_type=F32, **kw)


def _mm_kernel(a_ref, w_ref, o_ref):
    o_ref[...] = _dot(a_ref[...], w_ref[...].astype(BF16)).astype(o_ref.dtype)


def _matmul(a, w, layer=None, *, n=None, tm=1024, tn=512, out_dtype=F32):
    m, k = a.shape
    n = w.shape[-1] if n is None else n
    tn = min(tn, n)
    assert m % tm == 0 and n % tn == 0
    if w.ndim == 3:
        w_spec = pl.BlockSpec((None, k, tn), lambda i, j: (layer, 0, j))
    else:
        w_spec = pl.BlockSpec((k, tn), lambda i, j: (0, j))
    return pl.pallas_call(
        _mm_kernel,
        grid=(m // tm, n // tn),
        in_specs=[pl.BlockSpec((tm, k), lambda i, j: (i, 0)), w_spec],
        out_specs=pl.BlockSpec((tm, tn), lambda i, j: (i, j)),
        out_shape=jax.ShapeDtypeStruct((m, n), out_dtype),
        compiler_params=_cparams(("parallel", "arbitrary")),
        name="matmul",
    )(a, w)


def _mm2_kernel(a1_ref, a2_ref, w1_ref, w2_ref, o_ref):
    o_ref[...] = (_dot(a1_ref[...], w1_ref[...].astype(BF16))
                  + _dot(a2_ref[...], w2_ref[...].astype(BF16)))


def _matmul2(a1, a2, w, layer, *, tm=1024, tn=512):
    m, k1 = a1.shape
    k2 = a2.shape[1]
    n = w.shape[-1]
    assert k1 == k2 and w.shape[1] == k1 + k2
    return pl.pallas_call(
        _mm2_kernel,
        grid=(m // tm, n // tn),
        in_specs=[pl.BlockSpec((tm, k1), lambda i, j: (i, 0)),
                  pl.BlockSpec((tm, k2), lambda i, j: (i, 0)),
                  pl.BlockSpec((None, k1, tn), lambda i, j: (layer, 0, j)),
                  pl.BlockSpec((None, k2, tn), lambda i, j: (layer, 1, j))],
        out_specs=pl.BlockSpec((tm, tn), lambda i, j: (i, j)),
        out_shape=jax.ShapeDtypeStruct((m, n), F32),
        compiler_params=_cparams(("parallel", "arbitrary")),
        name="matmul2",
    )(a1, a2, w, w)


def _ln_store(x, g_ref, b_ref, of_ref, ob_ref):
    mu = jnp.mean(x, axis=-1, keepdims=True)
    xc = x - mu
    var = jnp.mean(xc * xc, axis=-1, keepdims=True)
    out = xc * lax.rsqrt(var + LN_EPS) * g_ref[...] + b_ref[...]
    of_ref[...] = out
    ob_ref[...] = out.astype(BF16)


def _add_ln_kernel(h_ref, y_ref, g_ref, b_ref, of_ref, ob_ref, *, alpha):
    _ln_store(alpha * h_ref[...] + y_ref[...], g_ref, b_ref, of_ref, ob_ref)


def _combine_ln_kernel(h_ref, *refs, alpha):
    ys, (w_ref, g_ref, b_ref, of_ref, ob_ref) = refs[:TOP_K], refs[TOP_K:]
    w = w_ref[...]
    x = alpha * h_ref[...]
    for k in range(TOP_K):
        x = x + w[:, k:k + 1] * ys[k][...].astype(F32)
    _ln_store(x, g_ref, b_ref, of_ref, ob_ref)


def _combine_ln(h, ys, w128, g, b, alpha, *, tm=256):
    n, d = h.shape
    nt = n // tm
    row = pl.BlockSpec((tm, d), lambda i: (i, 0))
    vec = pl.BlockSpec((1, d), lambda i: (0, 0))
    part = lambda k: pl.BlockSpec((tm, d), lambda i: (k * nt + i, 0))
    return pl.pallas_call(
        functools.partial(_combine_ln_kernel, alpha=alpha),
        grid=(nt,),
        in_specs=[row] + [part(k) for k in range(TOP_K)] + [pl.BlockSpec((tm, LANES), lambda i: (i, 0)), vec, vec],
        out_specs=[row, row],
        out_shape=[jax.ShapeDtypeStruct((n, d), F32), jax.ShapeDtypeStruct((n, d), BF16)],
        compiler_params=_cparams(("parallel",)),
        name="moe_combine_ln",
    )(h, *([ys] * TOP_K), w128, g.reshape(1, d), b.reshape(1, d))


def _add_ln(h, y, g, b, alpha, *, tm=256):
    n, d = h.shape
    row = pl.BlockSpec((tm, d), lambda i: (i, 0))
    vec = pl.BlockSpec((1, d), lambda i: (0, 0))
    return pl.pallas_call(
        functools.partial(_add_ln_kernel, alpha=alpha),
        grid=(n // tm,),
        in_specs=[row, row, vec, vec],
        out_specs=[row, row],
        out_shape=[jax.ShapeDtypeStruct((n, d), F32), jax.ShapeDtypeStruct((n, d), BF16)],
        compiler_params=_cparams(("parallel",)),
        name="add_ln",
    )(h, y, g.reshape(1, d), b.reshape(1, d))


SB_TQ = 512
SB_FAR = 256
SB_KB = LANES
SB_DEAD = -104.0


def _sb_kernel(q_ref, k_ref, v_ref, u_ref, o_ref, kb_ref, vb_ref, acc_ref, carry_ref, *, tq, scale):
    i = pl.program_id(2)
    kb = SB_KB
    nb = tq // kb

    @pl.when(i == 0)
    def _():
        kb_ref[...] = k_ref[...].astype(BF16)
        vb_ref[...] = v_ref[...].astype(BF16)

    q = q_ref[...].astype(BF16)
    u = u_ref[...]
    acc_ref[...] = jnp.zeros_like(acc_ref)
    carry_ref[...] = jnp.zeros_like(carry_ref)

    def block_terms(z, causal=None):
        sp = jnp.log(1.0 + jnp.exp(-jnp.abs(z)))
        ls = jnp.minimum(z, 0.0) - sp
        lk = ls - z
        if causal is not None:
            lk = jnp.where(causal, lk, 0.0)
        hi = lk.astype(BF16)
        lo = (lk - hi.astype(F32)).astype(BF16)
        rt = _dot(jnp.concatenate([hi, lo], axis=1), u)
        return ls, rt[:, :kb] - lk, rt[:, kb:]

    base = pl.multiple_of(i * tq, tq)
    for j in reversed(range(nb)):
        r0 = j * kb
        rows = tq - r0
        kj = kb_ref[pl.ds(base + r0, kb), :]
        vj = vb_ref[pl.ds(base + r0, kb), :]
        z = _dot_nt(q[r0:], kj) * scale
        causal = lax.broadcasted_iota(I32, (rows, kb), 1) < lax.broadcasted_iota(I32, (rows, kb), 0)
        ls, between, tot = block_terms(z, causal)
        carry = carry_ref[r0:, :]
        w = jnp.where(causal, jnp.exp(ls + between + carry), 0.0)
        acc_ref[r0:, :] += _dot(w.astype(BF16), vj)
        carry_ref[r0:, :] = carry + tot

    nf = SB_FAR // kb

    def far(state):
        c, _ = state
        start = pl.multiple_of(c * SB_FAR, SB_FAR)
        z = _dot_nt(q, kb_ref[pl.ds(start, SB_FAR), :]) * scale
        terms = [block_terms(z[:, j * kb:(j + 1) * kb]) for j in range(nf)]
        carry = carry_ref[...]
        ws = [None] * nf
        for j in reversed(range(nf)):
            ls, between, tot = terms[j]
            ws[j] = jnp.exp(ls + between + carry).astype(BF16)
            carry = carry + tot
        acc_ref[...] += _dot(jnp.concatenate(ws, axis=1), vb_ref[pl.ds(start, SB_FAR), :])
        carry_ref[...] = carry
        return c - 1, jnp.max(carry)

    lax.while_loop(lambda st: (st[0] >= 0) & (st[1] > SB_DEAD), far,
                   (i * (tq // SB_FAR) - 1, jnp.max(carry_ref[...])))
    o_ref[...] = acc_ref[...].astype(o_ref.dtype)


def _sb_attention(proj, b, t):
    tq, kb = min(SB_TQ, t), SB_KB
    nq = t // tq
    r = np.arange(2 * kb)[:, None] % kb
    c = np.arange(2 * kb)[None, :]
    u = jnp.asarray(np.where(c < kb, r >= c, True), BF16)
    kv_spec = lambda off: pl.BlockSpec((t, HEAD_DIM), lambda bi, h, i: (bi, off + h))
    return pl.pallas_call(
        functools.partial(_sb_kernel, tq=tq, scale=HEAD_DIM ** -0.5),
        grid=(b, SB_HEADS, nq),
        in_specs=[pl.BlockSpec((tq, HEAD_DIM), lambda bi, h, i: (bi * nq + i, h)),
                  kv_spec(SB_HEADS), kv_spec(2 * SB_HEADS),
                  pl.BlockSpec((2 * kb, 2 * kb), lambda bi, h, i: (0, 0))],
        out_specs=pl.BlockSpec((tq, HEAD_DIM), lambda bi, h, i: (bi * nq + i, h)),
        out_shape=jax.ShapeDtypeStruct((b * t, SB_WIDTH), BF16),
        scratch_shapes=[pltpu.VMEM((t, HEAD_DIM), BF16), pltpu.VMEM((t, HEAD_DIM), BF16),
                        pltpu.VMEM((tq, HEAD_DIM), F32), pltpu.VMEM((tq, kb), F32)],
        compiler_params=_cparams(("parallel", "parallel", "arbitrary")),
        name="sb_attention",
    )(proj, proj, proj, u)


HG_C = 128
HG_TB = 512
HG_NH = 8


def _hg_tables(c):
    levels = []
    m = c // 2
    while m >= 1:
        levels.append(m)
        m //= 2
    nl = len(levels)
    e = np.zeros((nl + 2, c, c), np.float32)
    mask = np.zeros((nl + 1, c, c), np.float32)
    p = np.arange(c)
    for li, m in enumerate(levels):
        blk = p // (2 * m)
        half = (p // m) % 2
        mid = blk * 2 * m + m - 1
        for r in range(c):
            if half[r] == 1:
                e[li, r, mid[r] + 1:r + 1] = 1.0
            else:
                e[li, r, r + 1:mid[r] + 1] = 1.0
        mask[li] = ((half[:, None] == 1) & (half[None, :] == 0) & (blk[:, None] == blk[None, :]))
    mask[nl] = np.eye(c)
    e[nl] = np.tril(np.ones((c, c)))
    e[nl + 1] = np.triu(np.ones((c, c)), 1)
    return e.reshape((nl + 2) * c, c), mask, nl


def _hg_kernel(qh_ref, fh_ref, ih_ref, gh_ref, lb_ref, ng_ref, e_ref, mask_ref, o_ref, st_ref,
               *, c, tb, nl, nh):
    @pl.when(pl.program_id(2) == 0)
    def _():
        st_ref[...] = jnp.zeros_like(st_ref)

    ng = ng_ref[...]
    emat = e_ref[...]

    def one_head(rows, hh):
        cols = slice(hh * HEAD_DIM, (hh + 1) * HEAD_DIM)
        lb = lb_ref[:, cols]
        qh = qh_ref[rows, cols]
        f = lb + (1.0 - lb) * jax.nn.sigmoid(fh_ref[rows, cols])
        g = jnp.log(jnp.maximum(f, F_MIN))
        kk = 1.0 - f
        q = qh * jax.nn.sigmoid(qh)
        v = ih_ref[rows, cols]
        vb = v.astype(BF16)
        g1 = g.astype(BF16)
        r1 = g - g1.astype(F32)
        g2 = r1.astype(BF16)
        g3 = (r1 - g2.astype(F32)).astype(BF16)
        ex = jnp.exp(_dot(emat, jnp.concatenate([g1, g2, g3], axis=0)))
        scores = mask_ref[nl] * _dot_nt(q.astype(BF16), kk.astype(BF16))
        for li in range(nl):
            a = ex[li * c:(li + 1) * c]
            scores = scores + mask_ref[li] * _dot_nt((q * a).astype(BF16), (kk * a).astype(BF16))
        ecum = ex[nl * c:(nl + 1) * c]
        erest = ex[(nl + 1) * c:(nl + 2) * c]
        st = st_ref[hh]
        o = _dot(scores.astype(BF16), vb) + _dot_nt((q * ecum).astype(BF16), st.astype(BF16))
        st_ref[hh] = ecum[c - 1:c, :] * st + _dot(v.T.astype(BF16), (kk * erest).astype(BF16))
        gh = gh_ref[rows, cols]
        o = o * lax.rsqrt(jnp.mean(o * o, axis=-1, keepdims=True) + LN_EPS) * ng
        o_ref[rows, cols] = (o * (gh * jax.nn.sigmoid(gh))).astype(o_ref.dtype)

    def chunk(ci, carry):
        rows = pl.ds(pl.multiple_of(ci * c, c), c)
        for hh in range(nh):
            one_head(rows, hh)
        return carry

    lax.fori_loop(0, tb // c, chunk, 0)


def _hgrn2(proj, lb, norm_g, b, t):
    c, tb, nh = HG_C, min(HG_TB, t), HG_NH
    e, mask, nl = _hg_tables(c)
    e3 = np.concatenate([e, e, e], axis=1)
    nt = t // tb
    base = 3 * SB_HEADS
    assert base % nh == 0 and HG_HEADS % nh == 0
    wide = nh * HEAD_DIM
    col = lambda k: pl.BlockSpec((tb, wide), lambda bi, h, i: (bi * nt + i, (base + k * HG_HEADS) // nh + h))
    const2 = lambda a: pl.BlockSpec(a.shape, lambda bi, h, i: (0,) * a.ndim)
    return pl.pallas_call(
        functools.partial(_hg_kernel, c=c, tb=tb, nl=nl, nh=nh),
        grid=(b, HG_HEADS // nh, nt),
        in_specs=[col(0), col(1), col(2), col(3),
                  pl.BlockSpec((1, wide), lambda bi, h, i: (0, h)),
                  pl.BlockSpec((1, HEAD_DIM), lambda bi, h, i: (0, 0)),
                  const2(e3), const2(mask)],
        out_specs=pl.BlockSpec((tb, wide), lambda bi, h, i: (bi * nt + i, h)),
        out_shape=jax.ShapeDtypeStruct((b * t, HG_WIDTH), BF16),
        scratch_shapes=[pltpu.VMEM((nh, HEAD_DIM, HEAD_DIM), F32)],
        compiler_params=_cparams(("parallel", "parallel", "arbitrary")),
        name="hgrn2",
    )(proj, proj, proj, proj, lb.reshape(1, HG_WIDTH), norm_g.reshape(1, HEAD_DIM),
      jnp.asarray(e3, BF16), jnp.asarray(mask))


def _cmp_kernel(a_ref, w1_ref, w2_ref, pos_ref, o_ref, *, ncp):
    half = CMP_LEN // 2
    p = jnp.zeros((ncp, HEAD_DIM), F32)
    q = jnp.zeros((ncp, HEAD_DIM), F32)
    for j in range(half):
        s = a_ref[pl.ds(j, ncp, stride=CMP_STRIDE), :]
        p = p + _dot((s + pos_ref[j:j + 1, :]).astype(BF16), w1_ref[j].astype(BF16))
        q = q + _dot((s + pos_ref[half + j:half + j + 1, :]).astype(BF16), w1_ref[half + j].astype(BF16))
    hid = p + pltpu.roll(q, ncp - 1, 0)
    hid = hid * jax.nn.sigmoid(hid)
    out = _dot(hid.astype(BF16), w2_ref[...].astype(BF16))
    row = lax.broadcasted_iota(I32, (ncp, HEAD_DIM), 0)
    o_ref[0, 0] = jnp.where(row < ncp - 1, out, 0.0)


def _compress(proj3, col0, w1, w2, pos):
    b, t, _ = proj3.shape
    ncp = t // CMP_STRIDE
    assert CMP_LEN == 2 * CMP_STRIDE
    return pl.pallas_call(
        functools.partial(_cmp_kernel, ncp=ncp),
        grid=(b, NSA_KV_GROUPS),
        in_specs=[pl.BlockSpec((None, t, HEAD_DIM), lambda bi, g: (bi, 0, col0 + g)),
                  pl.BlockSpec((CMP_LEN, HEAD_DIM, HEAD_DIM), lambda bi, g: (0, 0, 0)),
                  pl.BlockSpec((HEAD_DIM, HEAD_DIM), lambda bi, g: (0, 0)),
                  pl.BlockSpec((CMP_LEN, HEAD_DIM), lambda bi, g: (0, 0))],
        out_specs=pl.BlockSpec((1, 1, ncp, HEAD_DIM), lambda bi, g: (bi, g, 0, 0)),
        out_shape=jax.ShapeDtypeStruct((b, NSA_KV_GROUPS, ncp, HEAD_DIM), F32),
        compiler_params=_cparams(("parallel", "parallel")),
        name="nsa_compress",
    )(proj3, w1.reshape(CMP_LEN, HEAD_DIM, HEAD_DIM), w2, pos)


NSA_TQ = 128
CMP_NEAR = 16
CMP_NEAR_LO = 9


def _rel_bucket_np(dist):
    dist = np.maximum(dist, 0)
    max_exact = REL_BUCKETS // 2
    ratio = (np.log(np.maximum(dist, max_exact).astype(np.float32) / np.float32(max_exact))
             / np.float32(math.log(REL_MAX_DIST / max_exact)))
    large = np.minimum(max_exact + (ratio * np.float32(REL_BUCKETS - max_exact)).astype(np.int32),
                       REL_BUCKETS - 1)
    return np.where(dist < max_exact, dist, large).astype(np.int32)


def _cmpsel_kernel(q_ref, kc_ref, vc_ref, pb_ref, ovl_ref, oc_ref, sel_ref,
                   *, tq, ncp, nslc, nsel, scale):
    i = pl.program_id(2)
    kc = kc_ref[0, 0].astype(BF16)
    vc = vc_ref[0, 0].astype(BF16)
    qpos = i * tq + lax.broadcasted_iota(I32, (tq, ncp), 0)
    ncol = lax.broadcasted_iota(I32, (tq, ncp), 1)
    valid = (ncol * CMP_STRIDE + (CMP_LEN - 1) <= qpos) & (ncol < ncp - 1)
    sr = lax.broadcasted_iota(I32, (LANES, ncp), 0)
    sc = lax.broadcasted_iota(I32, (LANES, ncp), 1)
    first = i * (tq // CMP_STRIDE) - CMP_NEAR_LO
    place = jnp.where(sr < CMP_NEAR, jnp.where(sc == first + sr, 1.0, 0.0),
                      jnp.where(sr == CMP_NEAR, jnp.where(sc < first, 1.0, 0.0), 0.0))
    place = place.astype(BF16)
    bias = _dot(pb_ref[0], jnp.concatenate([place, place], axis=0))
    psum = jnp.zeros((tq, ncp), F32)
    for r in range(NSA_HPG):
        hs = slice(r * HEAD_DIM, (r + 1) * HEAD_DIM)
        s = _dot_nt(q_ref[0, :, hs].astype(BF16), kc) * scale + bias[r * tq:(r + 1) * tq]
        s = jnp.where(valid, s, NEG_INF)
        mx = jnp.max(s, axis=-1, keepdims=True)
        e = jnp.where(valid, jnp.exp(s - mx), 0.0)
        den = jnp.sum(e, axis=-1, keepdims=True)
        p = e / jnp.maximum(den, 1e-30)
        oc_ref[0, :, hs] = _dot(p.astype(BF16), vc)
        psum = psum + p
    imp = _dot_nt(ovl_ref[...], psum, precision=HI)
    jrow = lax.broadcasted_iota(I32, (nslc, tq), 0)
    qblk = (i * tq + lax.broadcasted_iota(I32, (nslc, tq), 1)) // SLC_BLOCK
    ok = jrow <= qblk
    forced = (jrow == 0) | (jrow == qblk) | (jrow == qblk - 1)
    imp = jnp.where(ok, jnp.where(forced, FORCED, imp), NEG_INF)
    rank = jnp.zeros((nslc, tq), F32)
    for j2 in range(nslc):
        row = imp[j2:j2 + 1, :]
        tie = jnp.where(jrow > j2, 1.0, 0.0)
        rank = rank + jnp.where(row > imp, 1.0, jnp.where(row == imp, tie, 0.0))
    sel_ref[0, 0] = jnp.where((rank < nsel) & ok, 1.0, 0.0)


def _cmp_select(proj3, k_cmp, v_cmp, pb, b, t):
    tq = NSA_TQ
    ncp = t // CMP_STRIDE
    nslc = t // SLC_BLOCK
    nsel = min(N_SELECT, nslc)
    n_idx = np.arange(ncp)
    slc_start = np.arange(nslc) * SLC_BLOCK
    cs = n_idx * CMP_STRIDE
    ovl = ((cs[None, :] < slc_start[:, None] + SLC_BLOCK)
           & (cs[None, :] + CMP_LEN - 1 >= slc_start[:, None])
           & (n_idx[None, :] < ncp - 1)).astype(np.float32)
    gw = NSA_HPG * HEAD_DIM
    return pl.pallas_call(
        functools.partial(_cmpsel_kernel, tq=tq, ncp=ncp, nslc=nslc, nsel=nsel, scale=HEAD_DIM ** -0.5),
        grid=(b, NSA_KV_GROUPS, t // tq),
        in_specs=[pl.BlockSpec((1, tq, gw), lambda bi, g, i: (bi, i, g)),
                  pl.BlockSpec((1, 1, ncp, HEAD_DIM), lambda bi, g, i: (bi, g, 0, 0)),
                  pl.BlockSpec((1, 1, ncp, HEAD_DIM), lambda bi, g, i: (bi, g, 0, 0)),
                  pl.BlockSpec((1, NSA_HPG * tq, 2 * LANES), lambda bi, g, i: (g, 0, 0)),
                  pl.BlockSpec((nslc, ncp), lambda bi, g, i: (0, 0))],
        out_specs=[pl.BlockSpec((1, tq, gw), lambda bi, g, i: (bi, i, g)),
                   pl.BlockSpec((1, 1, nslc, tq), lambda bi, g, i: (bi, g, 0, i))],
        out_shape=[jax.ShapeDtypeStruct((b, t, NSA_WIDTH), F32),
                   jax.ShapeDtypeStruct((b, NSA_KV_GROUPS, nslc, t), F32)],
        compiler_params=_cparams(("parallel", "parallel", "parallel")),
        name="nsa_cmp_select",
    )(proj3, k_cmp, v_cmp, pb, jnp.asarray(ovl))


SEL_CHUNK = 512
SEL_VEXT = 16
MASKED = -1e30
FAR_BIAS_COL = SLC_BLOCK


def _softmax_step(st, vt, m_ref, acc_ref, c1):
    m_old = m_ref[...]
    m_new = jnp.maximum(m_old, jnp.max(st, axis=0, keepdims=True))
    p = jnp.exp2((st - m_new) * c1)
    acc_ref[...] = jnp.exp2((m_old - m_new) * c1) * acc_ref[...] + _dot(vt, p.astype(BF16))
    m_ref[...] = m_new


def _sel_kernel(q_ref, k_ref, v_ref, kext_ref, vext_ref, bnear_ref, bfar_ref, sel_ref, o_ref,
                qa_ref, ka_ref, vt_ref, m_ref, acc_ref, s0_ref, s1_ref, *, tq, c1):
    i = pl.program_id(2)
    hpg = NSA_HPG
    d = HEAD_DIM

    @pl.when(i == 0)
    def _():
        ka_ref[:, :d] = k_ref[...].astype(BF16)
        ka_ref[:, d:] = kext_ref[...]
        for c in range(vt_ref.shape[0]):
            vt_ref[c, :d, :] = v_ref[c * SEL_CHUNK:(c + 1) * SEL_CHUNK, :].T.astype(BF16)
            vt_ref[c, d:, :] = vext_ref[...]

    sel = sel_ref[0, 0]
    nslc = sel.shape[1]
    blk = lax.broadcasted_iota(I32, (tq, nslc), 1)
    far_blocks = (i - 1) * (tq // SLC_BLOCK)
    maskq = jnp.where(blk < far_blocks, jnp.where(sel > 0.5, 0.0, MASKED), MASKED)
    ext = jnp.concatenate([maskq, jnp.zeros((tq, LANES - nslc), F32)], axis=1)
    for r in range(hpg):
        qa_ref[r * tq:(r + 1) * tq, :d] = q_ref[0, :, r * d:(r + 1) * d].astype(BF16)
        qa_ref[r * tq:(r + 1) * tq, d:] = (ext + bfar_ref[0, r:r + 1, :]).astype(BF16)
    m_ref[...] = jnp.full_like(m_ref, MASKED)
    acc_ref[...] = jnp.zeros_like(acc_ref)

    n_chunks = vt_ref.shape[0]

    def logits(c):
        start = pl.multiple_of(jnp.minimum(c, n_chunks - 1) * SEL_CHUNK, SEL_CHUNK)
        return _dot_nt(ka_ref[pl.ds(start, SEL_CHUNK), :], qa_ref[...])

    n_far = (jnp.maximum(i - 1, 0) * tq + SEL_CHUNK - 1) // SEL_CHUNK
    n_pairs = n_far // 2

    @pl.when(n_far > 0)
    def _():
        s0_ref[...] = logits(0)

    def pair(j, carry):
        c0 = 2 * j
        s1_ref[...] = logits(c0 + 1)
        _softmax_step(s0_ref[...], vt_ref[c0], m_ref, acc_ref, c1)
        s0_ref[...] = logits(c0 + 2)
        _softmax_step(s1_ref[...], vt_ref[c0 + 1], m_ref, acc_ref, c1)
        return carry

    lax.fori_loop(0, n_pairs, pair, 0)

    @pl.when(n_far % 2 == 1)
    def _():
        _softmax_step(s0_ref[...], vt_ref[n_far - 1], m_ref, acc_ref, c1)

    p0 = pl.multiple_of(jnp.maximum(i - 1, 0) * tq, tq)
    p1 = pl.multiple_of(i * tq, tq)
    kn = jnp.concatenate([ka_ref[pl.ds(p0, tq), :d], ka_ref[pl.ds(p1, tq), :d]], axis=0)
    vn = jnp.concatenate([v_ref[pl.ds(p0, tq), :], v_ref[pl.ds(p1, tq), :]], axis=0)
    vnt = jnp.concatenate([vn.T.astype(BF16), vext_ref[:, :2 * tq]], axis=0)
    er = lax.broadcasted_iota(I32, (2 * tq, nslc), 1)
    ec = lax.broadcasted_iota(I32, (2 * tq, nslc), 0) // SLC_BLOCK
    expand = jnp.where(er == far_blocks + ec, 1.0, 0.0).astype(BF16)
    picked = _dot_nt(expand, sel.astype(BF16))
    kj = lax.broadcasted_iota(I32, (2 * tq, tq), 0)
    qi = lax.broadcasted_iota(I32, (2 * tq, tq), 1)
    keep = jnp.where(kj <= qi + tq, picked, 0.0)
    keep = jnp.concatenate([keep] * hpg, axis=1) > 0.5
    st = _dot_nt(kn, qa_ref[:, :d]) + bnear_ref[0]
    _softmax_step(jnp.where(keep, st, MASKED), vnt, m_ref, acc_ref, c1)

    acc = acc_ref[...]
    out = acc[:d, :] / acc[d:d + 1, :]
    for r in range(hpg):
        o_ref[0, :, r * d:(r + 1) * d] = out[:, r * tq:(r + 1) * tq].T


def _nsa_selected(proj3, kcol, vcol, bnear, bfar, sel, b, t):
    tq = NSA_TQ
    d = HEAD_DIM
    gw = NSA_HPG * d
    nslc = t // SLC_BLOCK
    assert nslc <= FAR_BIAS_COL and t % SEL_CHUNK == 0
    rows = NSA_HPG * tq
    pos = np.arange(t)
    kext = np.zeros((t, LANES), np.float32)
    kext[pos, pos // SLC_BLOCK] = 1.0
    kext[:, FAR_BIAS_COL:FAR_BIAS_COL + 2] = 1.0
    vext = np.zeros((SEL_VEXT, SEL_CHUNK), np.float32)
    vext[0, :] = 1.0
    kv = lambda col: pl.BlockSpec((None, t, d), lambda bi, g, i: (bi, 0, col + g))
    const = lambda shape: pl.BlockSpec(shape, lambda bi, g, i: (0,) * len(shape))
    return pl.pallas_call(
        functools.partial(_sel_kernel, tq=tq, c1=d ** -0.5 * LOG2E),
        grid=(b, NSA_KV_GROUPS, t // tq),
        in_specs=[pl.BlockSpec((1, tq, gw), lambda bi, g, i: (bi, i, g)),
                  kv(kcol), kv(vcol), const((t, LANES)), const((SEL_VEXT, SEL_CHUNK)),
                  pl.BlockSpec((1, 2 * tq, rows), lambda bi, g, i: (g, 0, 0)),
                  pl.BlockSpec((1, NSA_HPG, LANES), lambda bi, g, i: (g, 0, 0)),
                  pl.BlockSpec((1, 1, tq, nslc), lambda bi, g, i: (bi, g, i, 0))],
        out_specs=pl.BlockSpec((1, tq, gw), lambda bi, g, i: (bi, i, g)),
        out_shape=jax.ShapeDtypeStruct((b, t, NSA_WIDTH), F32),
        scratch_shapes=[pltpu.VMEM((rows, 2 * d), BF16),
                        pltpu.VMEM((t, 2 * d), BF16),
                        pltpu.VMEM((t // SEL_CHUNK, d + SEL_VEXT, SEL_CHUNK), BF16),
                        pltpu.VMEM((1, rows), F32),
                        pltpu.VMEM((d + SEL_VEXT, rows), F32),
                        pltpu.VMEM((SEL_CHUNK, rows), F32),
                        pltpu.VMEM((SEL_CHUNK, rows), F32)],
        compiler_params=_cparams(("parallel", "parallel", "arbitrary")),
        name="nsa_selected",
    )(proj3, proj3, proj3, jnp.asarray(kext, BF16), jnp.asarray(vext, BF16),
      jnp.swapaxes(bnear, 1, 2), bfar, sel)


def _win_kernel(q_ref, k_ref, v_ref, bpat_ref, g_ref, oc_ref, os_ref, o_ref, kb_ref, vb_ref,
                *, tq, window, c1):
    i = pl.program_id(2)
    hpg = NSA_HPG
    d = HEAD_DIM
    nt = window // tq + 1

    @pl.when(i == 0)
    def _():
        kb_ref[...] = k_ref[...].astype(BF16)
        vb_ref[...] = v_ref[...].astype(BF16)

    qs = jnp.concatenate([q_ref[0, :, r * d:(r + 1) * d] for r in range(hpg)], axis=0).astype(BF16)
    t0 = jnp.maximum(i - (nt - 1), 0)
    qi = lax.broadcasted_iota(I32, (tq, tq), 0)
    kj = lax.broadcasted_iota(I32, (tq, tq), 1)
    parts = []
    for kk in range(nt):
        dt = i - (t0 + kk)
        start = pl.multiple_of((t0 + kk) * tq, tq)
        z = _dot_nt(qs, kb_ref[pl.ds(start, tq), :]) + bpat_ref[0, jnp.clip(dt, 0, 2)]
        dist = dt * tq + qi - kj
        keep = jnp.where(dist >= 0, jnp.where(dist < window, 1.0, 0.0), 0.0) > 0.5
        parts.append(jnp.where(keep[None], z.reshape(hpg, tq, tq), MASKED))
    s = jnp.concatenate(parts, axis=-1)
    m = jnp.max(s, axis=-1, keepdims=True)
    p = jnp.exp2((s - m) * c1)
    den = jnp.sum(p, axis=-1, keepdims=True)
    vwin = vb_ref[pl.ds(pl.multiple_of(t0 * tq, tq), nt * tq), :]
    out = (_dot(p.reshape(hpg * tq, nt * tq).astype(BF16), vwin).reshape(hpg, tq, d) / den)
    gate = jax.nn.sigmoid(g_ref[0, 0])
    for r in range(hpg):
        hs = slice(r * d, (r + 1) * d)
        mix = (gate[:, 3 * r:3 * r + 1] * oc_ref[0, :, hs] + gate[:, 3 * r + 1:3 * r + 2] * os_ref[0, :, hs]
               + gate[:, 3 * r + 2:3 * r + 3] * out[r])
        o_ref[0, :, hs] = mix.astype(o_ref.dtype)


def _nsa_window_mix(proj3, kcol, vcol, bpat, gate_logits, o_c, o_s, b, t):
    tq = NSA_TQ
    d = HEAD_DIM
    gw = NSA_HPG * d
    ng = 3 * NSA_HPG
    assert WINDOW % tq == 0 and t >= WINDOW + tq
    gates = jnp.transpose(gate_logits.reshape(b, t, NSA_KV_GROUPS, ng), (0, 2, 1, 3))
    kv = lambda col: pl.BlockSpec((None, t, d), lambda bi, g, i: (bi, 0, col + g))
    tile = pl.BlockSpec((1, tq, gw), lambda bi, g, i: (bi, i, g))
    return pl.pallas_call(
        functools.partial(_win_kernel, tq=tq, window=WINDOW, c1=d ** -0.5 * LOG2E),
        grid=(b, NSA_KV_GROUPS, t // tq),
        in_specs=[tile, kv(kcol), kv(vcol),
                  pl.BlockSpec((1, 3, NSA_HPG * tq, tq), lambda bi, g, i: (g, 0, 0, 0)),
                  pl.BlockSpec((1, 1, tq, ng), lambda bi, g, i: (bi, g, i, 0)),
                  tile, tile],
        out_specs=tile,
        out_shape=jax.ShapeDtypeStruct((b, t, NSA_WIDTH), BF16),
        scratch_shapes=[pltpu.VMEM((t, d), BF16), pltpu.VMEM((t, d), BF16)],
        compiler_params=_cparams(("parallel", "parallel", "arbitrary")),
        name="nsa_window_mix",
    )(proj3, proj3, proj3, bpat, gates, o_c, o_s)


def _nsa_bias_tables(rel_bias, t):
    tq = NSA_TQ
    g, hpg = NSA_KV_GROUPS, NSA_HPG
    inv_scale = HEAD_DIM ** 0.5
    tab = rel_bias.astype(F32)
    last = REL_BUCKETS - 1
    far_from = int(np.nonzero(_rel_bucket_np(np.arange(4 * REL_MAX_DIST)) < last)[0].max()) + 1
    assert (_rel_bucket_np(np.arange(far_from, t + tq)) == last).all()

    def by_group(a):
        a = jnp.moveaxis(a, -1, 0)
        return a.reshape((g, hpg) + a.shape[1:])

    iq = np.arange(tq)[:, None]
    m = np.arange(CMP_NEAR)[None, :]
    dist_c = iq - CMP_STRIDE * (m - CMP_NEAR_LO) - (CMP_LEN - 1)
    assert dist_c[:, 0].min() >= far_from
    assert (iq - CMP_STRIDE * (CMP_NEAR - CMP_NEAR_LO) - (CMP_LEN - 1)).max() < 0
    pb = jnp.zeros((g, hpg, tq, LANES), F32)
    pb = pb.at[..., :CMP_NEAR].set(by_group(tab[_rel_bucket_np(dist_c)]))
    pb = pb.at[..., CMP_NEAR].set(by_group(tab[last])[..., None])
    pb = pb.reshape(g, hpg * tq, LANES)
    pb_hi = pb.astype(BF16)
    pb = jnp.concatenate([pb_hi, (pb - pb_hi.astype(F32)).astype(BF16)], axis=-1)
    assert 2 * tq - (tq - 1) >= far_from
    jk = np.arange(tq)[None, :]
    idx = np.stack([_rel_bucket_np(iq - jk), _rel_bucket_np(tq + iq - jk), np.full((tq, tq), last, np.int32)])
    pat = by_group(tab[idx]) * inv_scale
    bpat = jnp.transpose(pat, (0, 2, 1, 3, 4)).reshape(g, 3, hpg * tq, tq)
    bnear = jnp.concatenate([pat[:, :, 1], pat[:, :, 0]], axis=-1).reshape(g, hpg * tq, 2 * tq)
    far = by_group(tab[last]) * inv_scale
    far_hi = far.astype(BF16).astype(F32)
    bfar = jnp.zeros((g, hpg, LANES), F32)
    bfar = bfar.at[..., FAR_BIAS_COL].set(far_hi).at[..., FAR_BIAS_COL + 1].set(far - far_hi)
    return pb, bnear, bfar, bpat


def _router_kernel(h_ref, w_ref, b_ref, tri_ref, idx_ref, wt_ref, rank_ref, cnt_ref, *, tm):
    @pl.when(pl.program_id(0) == 0)
    def _():
        cnt_ref[...] = jnp.zeros_like(cnt_ref)

    h = h_ref[...]
    h_hi = h.astype(BF16)
    h_lo = (h - h_hi.astype(F32)).astype(BF16)
    logits = (_dot(h_hi, w_ref[0]) + _dot(h_lo, w_ref[0]) + _dot(h_hi, w_ref[1])) + b_ref[...]
    lane = lax.broadcasted_iota(I32, (tm, N_EXPERTS), 1).astype(F32)
    out_lane = lax.broadcasted_iota(I32, (tm, LANES), 1)
    work = logits
    idx_out = jnp.zeros((tm, LANES), F32)
    val_out = jnp.zeros((tm, LANES), F32)
    top = None
    den = jnp.zeros((tm, 1), F32)
    hots = []
    for k in range(TOP_K):
        mx = jnp.max(work, axis=-1, keepdims=True)
        idx = jnp.min(jnp.where(work == mx, lane, float(N_EXPERTS)), axis=-1, keepdims=True)
        if top is None:
            top = mx
        e = jnp.exp(mx - top)
        den = den + e
        idx_out = jnp.where(out_lane == k, idx, idx_out)
        val_out = jnp.where(out_lane == k, e, val_out)
        hots.append(jnp.where(lane == idx, 1.0, 0.0))
        work = jnp.where(lane == idx, -jnp.inf, work)
    idx_ref[...] = idx_out.astype(I32)
    wt_ref[...] = val_out / den
    hot = hots[0] + hots[1] + hots[2] + hots[3]
    before = cnt_ref[...] + _dot(tri_ref[...], hot.astype(BF16))
    rank_out = jnp.zeros((tm, LANES), F32)
    for k in range(TOP_K):
        rank_out = jnp.where(out_lane == k, jnp.sum(hots[k] * before, axis=-1, keepdims=True), rank_out)
    rank_ref[...] = rank_out.astype(I32)
    cnt_ref[...] += jnp.sum(hot, axis=0, keepdims=True)


def _router(h, w, bias, *, tm=512):
    n, d = h.shape
    out = pl.BlockSpec((tm, LANES), lambda i: (i, 0))
    tri = jnp.asarray(np.tril(np.ones((tm, tm), np.float32), -1), BF16)
    w_hi = w.astype(BF16)
    return pl.pallas_call(
        functools.partial(_router_kernel, tm=tm),
        grid=(n // tm,),
        in_specs=[pl.BlockSpec((tm, d), lambda i: (i, 0)),
                  pl.BlockSpec((2, d, N_EXPERTS), lambda i: (0, 0, 0)),
                  pl.BlockSpec((1, N_EXPERTS), lambda i: (0, 0)),
                  pl.BlockSpec((tm, tm), lambda i: (0, 0))],
        out_specs=[out, out, out, pl.BlockSpec((1, N_EXPERTS), lambda i: (0, 0))],
        out_shape=[jax.ShapeDtypeStruct((n, LANES), I32), jax.ShapeDtypeStruct((n, LANES), F32),
                   jax.ShapeDtypeStruct((n, LANES), I32), jax.ShapeDtypeStruct((1, N_EXPERTS), F32)],
        compiler_params=_cparams(("arbitrary",)),
        name="moe_router",
    )(h, jnp.stack([w_hi, (w - w_hi.astype(F32)).astype(BF16)]), bias.reshape(1, N_EXPERTS), tri)


GU_BLK = 2 * LANES


def _gu_prep_kernel(w_ref, p_ref, o_ref):
    for c in range(w_ref.shape[-1] // GU_BLK):
        cols = slice(c * GU_BLK, (c + 1) * GU_BLK)
        o_ref[0, :, cols] = _dot(w_ref[0, :, cols].astype(BF16), p_ref[...]).astype(BF16)


def _gu_prep(w_gu, *, tk=4096):
    e, d, f2 = w_gu.shape
    assert f2 % GU_BLK == 0
    k = np.arange(GU_BLK)
    perm = np.zeros((GU_BLK, GU_BLK), np.float32)
    perm[k, (k % 2) * LANES + k // 2] = 1.0
    return pl.pallas_call(
        _gu_prep_kernel,
        grid=(e, d // tk),
        in_specs=[pl.BlockSpec((1, tk, f2), lambda ei, ki: (ei, ki, 0)),
                  pl.BlockSpec((GU_BLK, GU_BLK), lambda ei, ki: (0, 0))],
        out_specs=pl.BlockSpec((1, tk, f2), lambda ei, ki: (ei, ki, 0)),
        out_shape=jax.ShapeDtypeStruct((e, d, f2), BF16),
        compiler_params=_cparams(("parallel", "parallel")),
        name="moe_gu_prep",
    )(w_gu, jnp.asarray(perm, BF16))


MOE_TM = 512


def _expert_kernel(te_ref, x_ref, wgu_ref, bg_ref, bl_ref, wd_ref, bd_ref, o_ref):
    @pl.when(pl.program_id(0) < te_ref[pl.num_programs(0)])
    def _():
        wd = wd_ref[0].astype(BF16)
        half = x_ref.shape[0] // 2
        for part in range(2):
            rows = slice(part * half, (part + 1) * half)
            hgu = _dot(x_ref[rows, :], wgu_ref[0])
            nblk = hgu.shape[1] // GU_BLK
            glu = jnp.concatenate([hgu[:, c * GU_BLK:c * GU_BLK + LANES] for c in range(nblk)], axis=1)
            lin = jnp.concatenate([hgu[:, c * GU_BLK + LANES:(c + 1) * GU_BLK] for c in range(nblk)], axis=1)
            glu = jnp.minimum(glu + bg_ref[0], SWIGLU_LIMIT)
            lin = jnp.clip(lin + bl_ref[0], -SWIGLU_LIMIT, SWIGLU_LIMIT)
            act = glu * jax.nn.sigmoid(SWIGLU_ALPHA * glu) * (lin + 1.0)
            o_ref[rows, :] = (_dot(act.astype(BF16), wd) + bd_ref[0]).astype(o_ref.dtype)


def _experts(tile_expert, xs, w_gu, b_glu, b_lin, w_down, b_down):
    p, d = xs.shape
    tm = MOE_TM
    ff = w_down.shape[1]
    by_expert = lambda shape: pl.BlockSpec((1,) + shape, lambda i, te: (te[i], 0, 0))
    grid_spec = pltpu.PrefetchScalarGridSpec(
        num_scalar_prefetch=1,
        grid=(p // tm,),
        in_specs=[pl.BlockSpec((tm, d), lambda i, te: (i, 0)),
                  by_expert((d, 2 * ff)), by_expert((1, ff)), by_expert((1, ff)),
                  by_expert((ff, d)), by_expert((1, d))],
        out_specs=pl.BlockSpec((tm, d), lambda i, te: (i, 0)),
    )
    return pl.pallas_call(
        _expert_kernel,
        grid_spec=grid_spec,
        out_shape=jax.ShapeDtypeStruct((p, d), BF16),
        compiler_params=_cparams(("arbitrary",)),
        name="moe_experts",
    )(tile_expert, xs, w_gu, b_glu, b_lin, w_down, b_down)


def _moe(h_f32, h_bf16, w_router, b_router, layer, w_gu, b_glu, b_lin, w_down, b_down):
    n, d = h_f32.shape
    tm = MOE_TM
    idx128, wt128, rank128, cnt = _router(h_f32, w_router, b_router)
    expert = idx128[:, :TOP_K]
    na = n * TOP_K
    p = na + N_EXPERTS * tm
    counts = cnt[0].astype(I32)
    padded = ((counts + tm - 1) // tm) * tm
    pend = jnp.cumsum(padded)
    pstart = pend - padded
    first = jnp.sum(jnp.where(expert[..., None] == jnp.arange(N_EXPERTS, dtype=I32), pstart, 0), axis=-1)
    slot = first + rank128[:, :TOP_K]
    token = jnp.arange(na, dtype=I32) // TOP_K
    row_token = (jnp.arange(p, dtype=I32) % n).at[slot.reshape(-1)].set(
        token, mode="promise_in_bounds", unique_indices=True)
    tile_start = jnp.arange(p // tm, dtype=I32) * tm
    tile_expert = jnp.minimum(jnp.sum((tile_start[:, None] >= pend[None, :]).astype(I32), axis=1), N_EXPERTS - 1)
    xs = jnp.take(h_bf16, row_token, axis=0, mode="clip")
    tile_info = jnp.concatenate([tile_expert + layer * N_EXPERTS, pend[-1:] // tm]).astype(I32)
    ys = _experts(tile_info, xs, w_gu, b_glu, b_lin, w_down, b_down)
    return ys.at[slot.T.reshape(-1)].get(mode="promise_in_bounds"), wt128


def kernel(x, ln1_g, ln1_b, ln2_g, ln2_b, ev_w_in, ev_w_out, hg_lb_raw, hg_norm_g, od_w_in, od_w_out,
           cmp_k_w1, cmp_k_w2, cmp_k_pos, cmp_v_w1, cmp_v_w2, cmp_v_pos, rel_bias, router_w, router_b,
           exp_w_gu, exp_b_gu, exp_w_down, exp_b_down):
    b, t, d = x.shape
    n = b * t
    depth = ln1_g.shape[0]
    alpha = (2 * depth) ** 0.25
    lb_soft = jax.nn.softmax(hg_lb_raw.astype(F32), axis=0)
    lower_bounds = jnp.cumsum(lb_soft, axis=0) - lb_soft[0]
    pb, bnear, bfar, bpat = _nsa_bias_tables(rel_bias, t)

    n_all = depth * N_EXPERTS
    ff = exp_w_down.shape[2]
    wgu_all = _gu_prep(exp_w_gu.reshape(n_all, d, 2 * ff))
    wd_all = exp_w_down.reshape(n_all, ff, d)
    bg_all = exp_b_gu[..., 0::2].reshape(n_all, 1, ff)
    bl_all = exp_b_gu[..., 1::2].reshape(n_all, 1, ff)
    bd_all = exp_b_down.reshape(n_all, 1, d)

    h = x.reshape(n, d)
    hb = h.astype(BF16)
    for layer in range(depth):
        if layer % 2 == 0:
            e = layer // 2
            proj = _matmul(hb, ev_w_in, e)
            o_a = _sb_attention(proj, b, t)
            o_b = _hgrn2(proj, lower_bounds[layer], hg_norm_g[e], b, t)
            mix = _matmul2(o_a, o_b, ev_w_out, e)
        else:
            o = layer // 2
            proj = _matmul(hb, od_w_in, o, n=NSA_WIDTH + 6 * NSA_KV_WIDTH)
            gate_logits = _matmul(hb, od_w_in[o, :, NSA_WIDTH + 6 * NSA_KV_WIDTH:])
            proj3 = proj.reshape(b, t, -1)
            c0 = NSA_WIDTH // HEAD_DIM
            k_cmp = _compress(proj3, c0, cmp_k_w1[o], cmp_k_w2[o], cmp_k_pos[o])
            v_cmp = _compress(proj3, c0 + NSA_KV_GROUPS, cmp_v_w1[o], cmp_v_w2[o], cmp_v_pos[o])
            o_c, sel_t = _cmp_select(proj3, k_cmp, v_cmp, pb, b, t)
            sel = jnp.swapaxes(sel_t, 2, 3)
            o_s = _nsa_selected(proj3, c0 + 2 * NSA_KV_GROUPS, c0 + 3 * NSA_KV_GROUPS, bnear, bfar, sel, b, t)
            mixed = _nsa_window_mix(proj3, c0 + 4 * NSA_KV_GROUPS, c0 + 5 * NSA_KV_GROUPS, bpat,
                                    gate_logits, o_c, o_s, b, t)
            mix = _matmul(mixed.reshape(n, -1), od_w_out, o)
        h, hb = _add_ln(h, mix, ln1_g[layer], ln1_b[layer], alpha)
        ys, gate_w = _moe(h, hb, router_w[layer], router_b[layer], layer,
                          wgu_all, bg_all, bl_all, wd_all, bd_all)
        h, hb = _combine_ln(h, ys, gate_w, ln2_g[layer], ln2_b[layer], alpha)
    return h.reshape(b, t, d)
```

```python
import functools
import math

import jax
import jax.numpy as jnp
import numpy as np
from jax import lax
from jax.experimental import pallas as pl
from jax.experimental.pallas import tpu as pltpu

F32 = jnp.float32
BF16 = jnp.bfloat16
I32 = jnp.int32

HEAD_DIM = 128
SB_HEADS = 16
HG_HEADS = 16
SB_WIDTH = SB_HEADS * HEAD_DIM
HG_WIDTH = HG_HEADS * HEAD_DIM
F_MIN = 1e-6
NSA_HEADS = 32
NSA_KV_GROUPS = 4
NSA_HPG = NSA_HEADS // NSA_KV_GROUPS
NSA_WIDTH = NSA_HEADS * HEAD_DIM
NSA_KV_WIDTH = NSA_KV_GROUPS * HEAD_DIM
CMP_LEN = 32
CMP_STRIDE = 16
SLC_BLOCK = 64
N_SELECT = 16
WINDOW = 512
REL_BUCKETS = 32
REL_MAX_DIST = 128
N_EXPERTS = 32
TOP_K = 4
EXPERT_FF = 384
SWIGLU_LIMIT = 7.0
SWIGLU_ALPHA = 1.702
LN_EPS = 1e-5
NEG_INF = -1e30
FORCED = 1e9

LANES = 128
VMEM_LIMIT = 56 * 1024 * 1024

HI = lax.Precision.HIGHEST
LOG2E = 1.4426950408889634


def _cparams(sem):
    return pltpu.CompilerParams(dimension_semantics=sem, vmem_limit_bytes=VMEM_LIMIT)


def _dot_nt(a, b, **kw):
    return lax.dot_general(a, b, (((1,), (1,)), ((), ())), preferred_element_type=F32, **kw)


def _dot(a, b, **kw):
    return jnp.dot(a, b, preferred_element_type=F32, **kw)


def _mm_kernel(a_ref, w_ref, o_ref):
    o_ref[...] = _dot(a_ref[...], w_ref[...].astype(BF16)).astype(o_ref.dtype)


def _matmul(a, w, layer=None, *, n=None, tm=1024, tn=512, out_dtype=F32):
    m, k = a.shape
    n = w.shape[-1] if n is None else n
    tn = min(tn, n)
    assert m % tm == 0 and n % tn == 0
    if w.ndim == 3:
        w_spec = pl.BlockSpec((None, k, tn), lambda i, j: (layer, 0, j))
    else:
        w_spec = pl.BlockSpec((k, tn), lambda i, j: (0, j))
    return pl.pallas_call(
        _mm_kernel,
        grid=(m // tm, n // tn),
        in_specs=[pl.BlockSpec((tm, k), lambda i, j: (i, 0)), w_spec],
        out_specs=pl.BlockSpec((tm, tn), lambda i, j: (i, j)),
        out_shape=jax.ShapeDtypeStruct((m, n), out_dtype),
        compiler_params=_cparams(("parallel", "arbitrary")),
        name="matmul",
    )(a, w)


def _mm2_kernel(a1_ref, a2_ref, w1_ref, w2_ref, o_ref):
    o_ref[...] = (_dot(a1_ref[...], w1_ref[...].astype(BF16))
                  + _dot(a2_ref[...], w2_ref[...].astype(BF16)))


def _matmul2(a1, a2, w, layer, *, tm=1024, tn=512):
    m, k1 = a1.shape
    k2 = a2.shape[1]
    n = w.shape[-1]
    assert k1 == k2 and w.shape[1] == k1 + k2
    return pl.pallas_call(
        _mm2_kernel,
        grid=(m // tm, n // tn),
        in_specs=[pl.BlockSpec((tm, k1), lambda i, j: (i, 0)),
                  pl.BlockSpec((tm, k2), lambda i, j: (i, 0)),
                  pl.BlockSpec((None, k1, tn), lambda i, j: (layer, 0, j)),
                  pl.BlockSpec((None, k2, tn), lambda i, j: (layer, 1, j))],
        out_specs=pl.BlockSpec((tm, tn), lambda i, j: (i, j)),
        out_shape=jax.ShapeDtypeStruct((m, n), F32),
        compiler_params=_cparams(("parallel", "arbitrary")),
        name="matmul2",
    )(a1, a2, w, w)


def _ln_store(x, g_ref, b_ref, of_ref, ob_ref):
    mu = jnp.mean(x, axis=-1, keepdims=True)
    xc = x - mu
    var = jnp.mean(xc * xc, axis=-1, keepdims=True)
    out = xc * lax.rsqrt(var + LN_EPS) * g_ref[...] + b_ref[...]
    of_ref[...] = out
    ob_ref[...] = out.astype(BF16)


def _add_ln_kernel(h_ref, y_ref, g_ref, b_ref, of_ref, ob_ref, *, alpha):
    _ln_store(alpha * h_ref[...] + y_ref[...], g_ref, b_ref, of_ref, ob_ref)


def _combine_ln_kernel(h_ref, *refs, alpha):
    ys, (w_ref, g_ref, b_ref, of_ref, ob_ref) = refs[:TOP_K], refs[TOP_K:]
    w = w_ref[...]
    x = alpha * h_ref[...]
    for k in range(TOP_K):
        x = x + w[:, k:k + 1] * ys[k][...].astype(F32)
    _ln_store(x, g_ref, b_ref, of_ref, ob_ref)


def _combine_ln(h, ys, w128, g, b, alpha, *, tm=256):
    n, d = h.shape
    nt = n // tm
    row = pl.BlockSpec((tm, d), lambda i: (i, 0))
    vec = pl.BlockSpec((1, d), lambda i: (0, 0))
    part = lambda k: pl.BlockSpec((tm, d), lambda i: (k * nt + i, 0))
    return pl.pallas_call(
        functools.partial(_combine_ln_kernel, alpha=alpha),
        grid=(nt,),
        in_specs=[row] + [part(k) for k in range(TOP_K)] + [pl.BlockSpec((tm, LANES), lambda i: (i, 0)), vec, vec],
        out_specs=[row, row],
        out_shape=[jax.ShapeDtypeStruct((n, d), F32), jax.ShapeDtypeStruct((n, d), BF16)],
        compiler_params=_cparams(("parallel",)),
        name="moe_combine_ln",
    )(h, *([ys] * TOP_K), w128, g.reshape(1, d), b.reshape(1, d))


def _add_ln(h, y, g, b, alpha, *, tm=256):
    n, d = h.shape
    row = pl.BlockSpec((tm, d), lambda i: (i, 0))
    vec = pl.BlockSpec((1, d), lambda i: (0, 0))
    return pl.pallas_call(
        functools.partial(_add_ln_kernel, alpha=alpha),
        grid=(n // tm,),
        in_specs=[row, row, vec, vec],
        out_specs=[row, row],
        out_shape=[jax.ShapeDtypeStruct((n, d), F32), jax.ShapeDtypeStruct((n, d), BF16)],
        compiler_params=_cparams(("parallel",)),
        name="add_ln",
    )(h, y, g.reshape(1, d), b.reshape(1, d))


SB_TQ = 512
SB_FAR = 256
SB_KB = LANES
SB_DEAD = -104.0


def _sb_kernel(q_ref, k_ref, v_ref, u_ref, o_ref, kb_ref, vb_ref, acc_ref, carry_ref, *, tq, scale):
    i = pl.program_id(2)
    kb = SB_KB
    nb = tq // kb

    @pl.when(i == 0)
    def _():
        kb_ref[...] = k_ref[...].astype(BF16)
        vb_ref[...] = v_ref[...].astype(BF16)

    q = q_ref[...].astype(BF16)
    u = u_ref[...]
    acc_ref[...] = jnp.zeros_like(acc_ref)
    carry_ref[...] = jnp.zeros_like(carry_ref)

    def block_terms(z, causal=None):
        sp = jnp.log(1.0 + jnp.exp(-jnp.abs(z)))
        ls = jnp.minimum(z, 0.0) - sp
        lk = ls - z
        if causal is not None:
            lk = jnp.where(causal, lk, 0.0)
        hi = lk.astype(BF16)
        lo = (lk - hi.astype(F32)).astype(BF16)
        rt = _dot(jnp.concatenate([hi, lo], axis=1), u)
        return ls, rt[:, :kb] - lk, rt[:, kb:]

    base = pl.multiple_of(i * tq, tq)
    for j in reversed(range(nb)):
        r0 = j * kb
        rows = tq - r0
        kj = kb_ref[pl.ds(base + r0, kb), :]
        vj = vb_ref[pl.ds(base + r0, kb), :]
        z = _dot_nt(q[r0:], kj) * scale
        causal = lax.broadcasted_iota(I32, (rows, kb), 1) < lax.broadcasted_iota(I32, (rows, kb), 0)
        ls, between, tot = block_terms(z, causal)
        carry = carry_ref[r0:, :]
        w = jnp.where(causal, jnp.exp(ls + between + carry), 0.0)
        acc_ref[r0:, :] += _dot(w.astype(BF16), vj)
        carry_ref[r0:, :] = carry + tot

    nf = SB_FAR // kb

    def far(state):
        c, _ = state
        start = pl.multiple_of(c * SB_FAR, SB_FAR)
        z = _dot_nt(q, kb_ref[pl.ds(start, SB_FAR), :]) * scale
        terms = [block_terms(z[:, j * kb:(j + 1) * kb]) for j in range(nf)]
        carry = carry_ref[...]
        ws = [None] * nf
        for j in reversed(range(nf)):
            ls, between, tot = terms[j]
            ws[j] = jnp.exp(ls + between + carry).astype(BF16)
            carry = carry + tot
        acc_ref[...] += _dot(jnp.concatenate(ws, axis=1), vb_ref[pl.ds(start, SB_FAR), :])
        carry_ref[...] = carry
        return c - 1, jnp.max(carry)

    lax.while_loop(lambda st: (st[0] >= 0) & (st[1] > SB_DEAD), far,
                   (i * (tq // SB_FAR) - 1, jnp.max(carry_ref[...])))
    o_ref[...] = acc_ref[...].astype(o_ref.dtype)


def _sb_attention(proj, b, t):
    tq, kb = min(SB_TQ, t), SB_KB
    nq = t // tq
    r = np.arange(2 * kb)[:, None] % kb
    c = np.arange(2 * kb)[None, :]
    u = jnp.asarray(np.where(c < kb, r >= c, True), BF16)
    kv_spec = lambda off: pl.BlockSpec((t, HEAD_DIM), lambda bi, h, i: (bi, off + h))
    return pl.pallas_call(
        functools.partial(_sb_kernel, tq=tq, scale=HEAD_DIM ** -0.5),
        grid=(b, SB_HEADS, nq),
        in_specs=[pl.BlockSpec((tq, HEAD_DIM), lambda bi, h, i: (bi * nq + i, h)),
                  kv_spec(SB_HEADS), kv_spec(2 * SB_HEADS),
                  pl.BlockSpec((2 * kb, 2 * kb), lambda bi, h, i: (0, 0))],
        out_specs=pl.BlockSpec((tq, HEAD_DIM), lambda bi, h, i: (bi * nq + i, h)),
        out_shape=jax.ShapeDtypeStruct((b * t, SB_WIDTH), BF16),
        scratch_shapes=[pltpu.VMEM((t, HEAD_DIM), BF16), pltpu.VMEM((t, HEAD_DIM), BF16),
                        pltpu.VMEM((tq, HEAD_DIM), F32), pltpu.VMEM((tq, kb), F32)],
        compiler_params=_cparams(("parallel", "parallel", "arbitrary")),
        name="sb_attention",
    )(proj, proj, proj, u)


HG_C = 128
HG_TB = 512
HG_NH = 8


def _hg_tables(c):
    levels = []
    m = c // 2
    while m >= 1:
        levels.append(m)
        m //= 2
    nl = len(levels)
    e = np.zeros((nl + 2, c, c), np.float32)
    mask = np.zeros((nl + 1, c, c), np.float32)
    p = np.arange(c)
    for li, m in enumerate(levels):
        blk = p // (2 * m)
        half = (p // m) % 2
        mid = blk * 2 * m + m - 1
        for r in range(c):
            if half[r] == 1:
                e[li, r, mid[r] + 1:r + 1] = 1.0
            else:
                e[li, r, r + 1:mid[r] + 1] = 1.0
        mask[li] = ((half[:, None] == 1) & (half[None, :] == 0) & (blk[:, None] == blk[None, :]))
    mask[nl] = np.eye(c)
    e[nl] = np.tril(np.ones((c, c)))
    e[nl + 1] = np.triu(np.ones((c, c)), 1)
    return e.reshape((nl + 2) * c, c), mask, nl


def _hg_kernel(qh_ref, fh_ref, ih_ref, gh_ref, lb_ref, ng_ref, e_ref, mask_ref, o_ref, st_ref,
               *, c, tb, nl, nh):
    @pl.when(pl.program_id(2) == 0)
    def _():
        st_ref[...] = jnp.zeros_like(st_ref)

    ng = ng_ref[...]
    emat = e_ref[...]

    def one_head(rows, hh):
        cols = slice(hh * HEAD_DIM, (hh + 1) * HEAD_DIM)
        lb = lb_ref[:, cols]
        qh = qh_ref[rows, cols]
        f = lb + (1.0 - lb) * jax.nn.sigmoid(fh_ref[rows, cols])
        g = jnp.log(jnp.maximum(f, F_MIN))
        kk = 1.0 - f
        q = qh * jax.nn.sigmoid(qh)
        v = ih_ref[rows, cols]
        vb = v.astype(BF16)
        g1 = g.astype(BF16)
        r1 = g - g1.astype(F32)
        g2 = r1.astype(BF16)
        g3 = (r1 - g2.astype(F32)).astype(BF16)
        ex = jnp.exp(_dot(emat, jnp.concatenate([g1, g2, g3], axis=0)))
        scores = mask_ref[nl] * _dot_nt(q.astype(BF16), kk.astype(BF16))
        for li in range(nl):
            a = ex[li * c:(li + 1) * c]
            scores = scores + mask_ref[li] * _dot_nt((q * a).astype(BF16), (kk * a).astype(BF16))
        ecum = ex[nl * c:(nl + 1) * c]
        erest = ex[(nl + 1) * c:(nl + 2) * c]
        st = st_ref[hh]
        o = _dot(scores.astype(BF16), vb) + _dot_nt((q * ecum).astype(BF16), st.astype(BF16))
        st_ref[hh] = ecum[c - 1:c, :] * st + _dot(v.T.astype(BF16), (kk * erest).astype(BF16))
        gh = gh_ref[rows, cols]
        o = o * lax.rsqrt(jnp.mean(o * o, axis=-1, keepdims=True) + LN_EPS) * ng
        o_ref[rows, cols] = (o * (gh * jax.nn.sigmoid(gh))).astype(o_ref.dtype)

    def chunk(ci, carry):
        rows = pl.ds(pl.multiple_of(ci * c, c), c)
        for hh in range(nh):
            one_head(rows, hh)
        return carry

    lax.fori_loop(0, tb // c, chunk, 0)


def _hgrn2(proj, lb, norm_g, b, t):
    c, tb, nh = HG_C, min(HG_TB, t), HG_NH
    e, mask, nl = _hg_tables(c)
    e3 = np.concatenate([e, e, e], axis=1)
    nt = t // tb
    base = 3 * SB_HEADS
    assert base % nh == 0 and HG_HEADS % nh == 0
    wide = nh * HEAD_DIM
    col = lambda k: pl.BlockSpec((tb, wide), lambda bi, h, i: (bi * nt + i, (base + k * HG_HEADS) // nh + h))
    const2 = lambda a: pl.BlockSpec(a.shape, lambda bi, h, i: (0,) * a.ndim)
    return pl.pallas_call(
        functools.partial(_hg_kernel, c=c, tb=tb, nl=nl, nh=nh),
        grid=(b, HG_HEADS // nh, nt),
        in_specs=[col(0), col(1), col(2), col(3),
                  pl.BlockSpec((1, wide), lambda bi, h, i: (0, h)),
                  pl.BlockSpec((1, HEAD_DIM), lambda bi, h, i: (0, 0)),
                  const2(e3), const2(mask)],
        out_specs=pl.BlockSpec((tb, wide), lambda bi, h, i: (bi * nt + i, h)),
        out_shape=jax.ShapeDtypeStruct((b * t, HG_WIDTH), BF16),
        scratch_shapes=[pltpu.VMEM((nh, HEAD_DIM, HEAD_DIM), F32)],
        compiler_params=_cparams(("parallel", "parallel", "arbitrary")),
        name="hgrn2",
    )(proj, proj, proj, proj, lb.reshape(1, HG_WIDTH), norm_g.reshape(1, HEAD_DIM),
      jnp.asarray(e3, BF16), jnp.asarray(mask))


def _cmp_kernel(a_ref, w1_ref, w2_ref, pos_ref, o_ref, *, ncp):
    half = CMP_LEN // 2
    p = jnp.zeros((ncp, HEAD_DIM), F32)
    q = jnp.zeros((ncp, HEAD_DIM), F32)
    for j in range(half):
        s = a_ref[pl.ds(j, ncp, stride=CMP_STRIDE), :]
        p = p + _dot((s + pos_ref[j:j + 1, :]).astype(BF16), w1_ref[j].astype(BF16))
        q = q + _dot((s + pos_ref[half + j:half + j + 1, :]).astype(BF16), w1_ref[half + j].astype(BF16))
    hid = p + pltpu.roll(q, ncp - 1, 0)
    hid = hid * jax.nn.sigmoid(hid)
    out = _dot(hid.astype(BF16), w2_ref[...].astype(BF16))
    row = lax.broadcasted_iota(I32, (ncp, HEAD_DIM), 0)
    o_ref[0, 0] = jnp.where(row < ncp - 1, out, 0.0)


def _compress(proj3, col0, w1, w2, pos):
    b, t, _ = proj3.shape
    ncp = t // CMP_STRIDE
    assert CMP_LEN == 2 * CMP_STRIDE
    return pl.pallas_call(
        functools.partial(_cmp_kernel, ncp=ncp),
        grid=(b, NSA_KV_GROUPS),
        in_specs=[pl.BlockSpec((None, t, HEAD_DIM), lambda bi, g: (bi, 0, col0 + g)),
                  pl.BlockSpec((CMP_LEN, HEAD_DIM, HEAD_DIM), lambda bi, g: (0, 0, 0)),
                  pl.BlockSpec((HEAD_DIM, HEAD_DIM), lambda bi, g: (0, 0)),
                  pl.BlockSpec((CMP_LEN, HEAD_DIM), lambda bi, g: (0, 0))],
        out_specs=pl.BlockSpec((1, 1, ncp, HEAD_DIM), lambda bi, g: (bi, g, 0, 0)),
        out_shape=jax.ShapeDtypeStruct((b, NSA_KV_GROUPS, ncp, HEAD_DIM), F32),
        compiler_params=_cparams(("parallel", "parallel")),
        name="nsa_compress",
    )(proj3, w1.reshape(CMP_LEN, HEAD_DIM, HEAD_DIM), w2, pos)


NSA_TQ = 128
CMP_NEAR = 16
CMP_NEAR_LO = 9


def _rel_bucket_np(dist):
    dist = np.maximum(dist, 0)
    max_exact = REL_BUCKETS // 2
    ratio = (np.log(np.maximum(dist, max_exact).astype(np.float32) / np.float32(max_exact))
             / np.float32(math.log(REL_MAX_DIST / max_exact)))
    large = np.minimum(max_exact + (ratio * np.float32(REL_BUCKETS - max_exact)).astype(np.int32),
                       REL_BUCKETS - 1)
    return np.where(dist < max_exact, dist, large).astype(np.int32)


def _cmpsel_kernel(q_ref, kc_ref, vc_ref, pb_ref, ovl_ref, oc_ref, sel_ref,
                   *, tq, ncp, nslc, nsel, scale):
    i = pl.program_id(2)
    kc = kc_ref[0, 0].astype(BF16)
    vc = vc_ref[0, 0].astype(BF16)
    qpos = i * tq + lax.broadcasted_iota(I32, (tq, ncp), 0)
    ncol = lax.broadcasted_iota(I32, (tq, ncp), 1)
    valid = (ncol * CMP_STRIDE + (CMP_LEN - 1) <= qpos) & (ncol < ncp - 1)
    sr = lax.broadcasted_iota(I32, (LANES, ncp), 0)
    sc = lax.broadcasted_iota(I32, (LANES, ncp), 1)
    first = i * (tq // CMP_STRIDE) - CMP_NEAR_LO
    place = jnp.where(sr < CMP_NEAR, jnp.where(sc == first + sr, 1.0, 0.0),
                      jnp.where(sr == CMP_NEAR, jnp.where(sc < first, 1.0, 0.0), 0.0))
    place = place.astype(BF16)
    bias = _dot(pb_ref[0], jnp.concatenate([place, place], axis=0))
    psum = jnp.zeros((tq, ncp), F32)
    for r in range(NSA_HPG):
        hs = slice(r * HEAD_DIM, (r + 1) * HEAD_DIM)
        s = _dot_nt(q_ref[0, :, hs].astype(BF16), kc) * scale + bias[r * tq:(r + 1) * tq]
        s = jnp.where(valid, s, NEG_INF)
        mx = jnp.max(s, axis=-1, keepdims=True)
        e = jnp.where(valid, jnp.exp(s - mx), 0.0)
        den = jnp.sum(e, axis=-1, keepdims=True)
        p = e / jnp.maximum(den, 1e-30)
        oc_ref[0, :, hs] = _dot(p.astype(BF16), vc)
        psum = psum + p
    imp = _dot_nt(ovl_ref[...], psum, precision=HI)
    jrow = lax.broadcasted_iota(I32, (nslc, tq), 0)
    qblk = (i * tq + lax.broadcasted_iota(I32, (nslc, tq), 1)) // SLC_BLOCK
    ok = jrow <= qblk
    forced = (jrow == 0) | (jrow == qblk) | (jrow == qblk - 1)
    imp = jnp.where(ok, jnp.where(forced, FORCED, imp), NEG_INF)
    rank = jnp.zeros((nslc, tq), F32)
    for j2 in range(nslc):
        row = imp[j2:j2 + 1, :]
        tie = jnp.where(jrow > j2, 1.0, 0.0)
        rank = rank + jnp.where(row > imp, 1.0, jnp.where(row == imp, tie, 0.0))
    sel_ref[0, 0] = jnp.where((rank < nsel) & ok, 1.0, 0.0)


def _cmp_select(proj3, k_cmp, v_cmp, pb, b, t):
    tq = NSA_TQ
    ncp = t // CMP_STRIDE
    nslc = t // SLC_BLOCK
    nsel = min(N_SELECT, nslc)
    n_idx = np.arange(ncp)
    slc_start = np.arange(nslc) * SLC_BLOCK
    cs = n_idx * CMP_STRIDE
    ovl = ((cs[None, :] < slc_start[:, None] + SLC_BLOCK)
           & (cs[None, :] + CMP_LEN - 1 >= slc_start[:, None])
           & (n_idx[None, :] < ncp - 1)).astype(np.float32)
    gw = NSA_HPG * HEAD_DIM
    return pl.pallas_call(
        functools.partial(_cmpsel_kernel, tq=tq, ncp=ncp, nslc=nslc, nsel=nsel, scale=HEAD_DIM ** -0.5),
        grid=(b, NSA_KV_GROUPS, t // tq),
        in_specs=[pl.BlockSpec((1, tq, gw), lambda bi, g, i: (bi, i, g)),
                  pl.BlockSpec((1, 1, ncp, HEAD_DIM), lambda bi, g, i: (bi, g, 0, 0)),
                  pl.BlockSpec((1, 1, ncp, HEAD_DIM), lambda bi, g, i: (bi, g, 0, 0)),
                  pl.BlockSpec((1, NSA_HPG * tq, 2 * LANES), lambda bi, g, i: (g, 0, 0)),
                  pl.BlockSpec((nslc, ncp), lambda bi, g, i: (0, 0))],
        out_specs=[pl.BlockSpec((1, tq, gw), lambda bi, g, i: (bi, i, g)),
                   pl.BlockSpec((1, 1, nslc, tq), lambda bi, g, i: (bi, g, 0, i))],
        out_shape=[jax.ShapeDtypeStruct((b, t, NSA_WIDTH), F32),
                   jax.ShapeDtypeStruct((b, NSA_KV_GROUPS, nslc, t), F32)],
        compiler_params=_cparams(("parallel", "parallel", "parallel")),
        name="nsa_cmp_select",
    )(proj3, k_cmp, v_cmp, pb, jnp.asarray(ovl))


SEL_CHUNK = 512
SEL_VEXT = 16
MASKED = -1e30
FAR_BIAS_COL = SLC_BLOCK


def _softmax_step(st, vt, m_ref, acc_ref, c1):
    m_old = m_ref[...]
    m_new = jnp.maximum(m_old, jnp.max(st, axis=0, keepdims=True))
    p = jnp.exp2((st - m_new) * c1)
    acc_ref[...] = jnp.exp2((m_old - m_new) * c1) * acc_ref[...] + _dot(vt, p.astype(BF16))
    m_ref[...] = m_new


def _sel_kernel(q_ref, k_ref, v_ref, kext_ref, vext_ref, bnear_ref, bfar_ref, sel_ref, o_ref,
                qa_ref, ka_ref, vt_ref, m_ref, acc_ref, s0_ref, s1_ref, *, tq, c1):
    i = pl.program_id(2)
    hpg = NSA_HPG
    d = HEAD_DIM

    @pl.when(i == 0)
    def _():
        ka_ref[:, :d] = k_ref[...].astype(BF16)
        ka_ref[:, d:] = kext_ref[...]
        for c in range(vt_ref.shape[0]):
            vt_ref[c, :d, :] = v_ref[c * SEL_CHUNK:(c + 1) * SEL_CHUNK, :].T.astype(BF16)
            vt_ref[c, d:, :] = vext_ref[...]

    sel = sel_ref[0, 0]
    nslc = sel.shape[1]
    blk = lax.broadcasted_iota(I32, (tq, nslc), 1)
    far_blocks = (i - 1) * (tq // SLC_BLOCK)
    maskq = jnp.where(blk < far_blocks, jnp.where(sel > 0.5, 0.0, MASKED), MASKED)
    ext = jnp.concatenate([maskq, jnp.zeros((tq, LANES - nslc), F32)], axis=1)
    for r in range(hpg):
        qa_ref[r * tq:(r + 1) * tq, :d] = q_ref[0, :, r * d:(r + 1) * d].astype(BF16)
        qa_ref[r * tq:(r + 1) * tq, d:] = (ext + bfar_ref[0, r:r + 1, :]).astype(BF16)
    m_ref[...] = jnp.full_like(m_ref, MASKED)
    acc_ref[...] = jnp.zeros_like(acc_ref)

    n_chunks = vt_ref.shape[0]

    def logits(c):
        start = pl.multiple_of(jnp.minimum(c, n_chunks - 1) * SEL_CHUNK, SEL_CHUNK)
        return _dot_nt(ka_ref[pl.ds(start, SEL_CHUNK), :], qa_ref[...])

    n_far = (jnp.maximum(i - 1, 0) * tq + SEL_CHUNK - 1) // SEL_CHUNK
    n_pairs = n_far // 2

    @pl.when(n_far > 0)
    def _():
        s0_ref[...] = logits(0)

    def pair(j, carry):
        c0 = 2 * j
        s1_ref[...] = logits(c0 + 1)
        _softmax_step(s0_ref[...], vt_ref[c0], m_ref, acc_ref, c1)
        s0_ref[...] = logits(c0 + 2)
        _softmax_step(s1_ref[...], vt_ref[c0 + 1], m_ref, acc_ref, c1)
        return carry

    lax.fori_loop(0, n_pairs, pair, 0)

    @pl.when(n_far % 2 == 1)
    def _():
        _softmax_step(s0_ref[...], vt_ref[n_far - 1], m_ref, acc_ref, c1)

    p0 = pl.multiple_of(jnp.maximum(i - 1, 0) * tq, tq)
    p1 = pl.multiple_of(i * tq, tq)
    kn = jnp.concatenate([ka_ref[pl.ds(p0, tq), :d], ka_ref[pl.ds(p1, tq), :d]], axis=0)
    vn = jnp.concatenate([v_ref[pl.ds(p0, tq), :], v_ref[pl.ds(p1, tq), :]], axis=0)
    vnt = jnp.concatenate([vn.T.astype(BF16), vext_ref[:, :2 * tq]], axis=0)
    er = lax.broadcasted_iota(I32, (2 * tq, nslc), 1)
    ec = lax.broadcasted_iota(I32, (2 * tq, nslc), 0) // SLC_BLOCK
    expand = jnp.where(er == far_blocks + ec, 1.0, 0.0).astype(BF16)
    picked = _dot_nt(expand, sel.astype(BF16))
    kj = lax.broadcasted_iota(I32, (2 * tq, tq), 0)
    qi = lax.broadcasted_iota(I32, (2 * tq, tq), 1)
    keep = jnp.where(kj <= qi + tq, picked, 0.0)
    keep = jnp.concatenate([keep] * hpg, axis=1) > 0.5
    st = _dot_nt(kn, qa_ref[:, :d]) + bnear_ref[0]
    _softmax_step(jnp.where(keep, st, MASKED), vnt, m_ref, acc_ref, c1)

    acc = acc_ref[...]
    out = acc[:d, :] / acc[d:d + 1, :]
    for r in range(hpg):
        o_ref[0, :, r * d:(r + 1) * d] = out[:, r * tq:(r + 1) * tq].T


def _nsa_selected(proj3, kcol, vcol, bnear, bfar, sel, b, t):
    tq = NSA_TQ
    d = HEAD_DIM
    gw = NSA_HPG * d
    nslc = t // SLC_BLOCK
    assert nslc <= FAR_BIAS_COL and t % SEL_CHUNK == 0
    rows = NSA_HPG * tq
    pos = np.arange(t)
    kext = np.zeros((t, LANES), np.float32)
    kext[pos, pos // SLC_BLOCK] = 1.0
    kext[:, FAR_BIAS_COL:FAR_BIAS_COL + 2] = 1.0
    vext = np.zeros((SEL_VEXT, SEL_CHUNK), np.float32)
    vext[0, :] = 1.0
    kv = lambda col: pl.BlockSpec((None, t, d), lambda bi, g, i: (bi, 0, col + g))
    const = lambda shape: pl.BlockSpec(shape, lambda bi, g, i: (0,) * len(shape))
    return pl.pallas_call(
        functools.partial(_sel_kernel, tq=tq, c1=d ** -0.5 * LOG2E),
        grid=(b, NSA_KV_GROUPS, t // tq),
        in_specs=[pl.BlockSpec((1, tq, gw), lambda bi, g, i: (bi, i, g)),
                  kv(kcol), kv(vcol), const((t, LANES)), const((SEL_VEXT, SEL_CHUNK)),
                  pl.BlockSpec((1, 2 * tq, rows), lambda bi, g, i: (g, 0, 0)),
                  pl.BlockSpec((1, NSA_HPG, LANES), lambda bi, g, i: (g, 0, 0)),
                  pl.BlockSpec((1, 1, tq, nslc), lambda bi, g, i: (bi, g, i, 0))],
        out_specs=pl.BlockSpec((1, tq, gw), lambda bi, g, i: (bi, i, g)),
        out_shape=jax.ShapeDtypeStruct((b, t, NSA_WIDTH), F32),
        scratch_shapes=[pltpu.VMEM((rows, 2 * d), BF16),
                        pltpu.VMEM((t, 2 * d), BF16),
                        pltpu.VMEM((t // SEL_CHUNK, d + SEL_VEXT, SEL_CHUNK), BF16),
                        pltpu.VMEM((1, rows), F32),
                        pltpu.VMEM((d + SEL_VEXT, rows), F32),
                        pltpu.VMEM((SEL_CHUNK, rows), F32),
                        pltpu.VMEM((SEL_CHUNK, rows), F32)],
        compiler_params=_cparams(("parallel", "parallel", "arbitrary")),
        name="nsa_selected",
    )(proj3, proj3, proj3, jnp.asarray(kext, BF16), jnp.asarray(vext, BF16),
      jnp.swapaxes(bnear, 1, 2), bfar, sel)


def _win_kernel(q_ref, k_ref, v_ref, bpat_ref, g_ref, oc_ref, os_ref, o_ref, kb_ref, vb_ref,
                *, tq, window, c1):
    i = pl.program_id(2)
    hpg = NSA_HPG
    d = HEAD_DIM
    nt = window // tq + 1

    @pl.when(i == 0)
    def _():
        kb_ref[...] = k_ref[...].astype(BF16)
        vb_ref[...] = v_ref[...].astype(BF16)

    qs = jnp.concatenate([q_ref[0, :, r * d:(r + 1) * d] for r in range(hpg)], axis=0).astype(BF16)
    t0 = jnp.maximum(i - (nt - 1), 0)
    qi = lax.broadcasted_iota(I32, (tq, tq), 0)
    kj = lax.broadcasted_iota(I32, (tq, tq), 1)
    parts = []
    for kk in range(nt):
        dt = i - (t0 + kk)
        start = pl.multiple_of((t0 + kk) * tq, tq)
        z = _dot_nt(qs, kb_ref[pl.ds(start, tq), :]) + bpat_ref[0, jnp.clip(dt, 0, 2)]
        dist = dt * tq + qi - kj
        keep = jnp.where(dist >= 0, jnp.where(dist < window, 1.0, 0.0), 0.0) > 0.5
        parts.append(jnp.where(keep[None], z.reshape(hpg, tq, tq), MASKED))
    s = jnp.concatenate(parts, axis=-1)
    m = jnp.max(s, axis=-1, keepdims=True)
    p = jnp.exp2((s - m) * c1)
    den = jnp.sum(p, axis=-1, keepdims=True)
    vwin = vb_ref[pl.ds(pl.multiple_of(t0 * tq, tq), nt * tq), :]
    out = (_dot(p.reshape(hpg * tq, nt * tq).astype(BF16), vwin).reshape(hpg, tq, d) / den)
    gate = jax.nn.sigmoid(g_ref[0, 0])
    for r in range(hpg):
        hs = slice(r * d, (r + 1) * d)
        mix = (gate[:, 3 * r:3 * r + 1] * oc_ref[0, :, hs] + gate[:, 3 * r + 1:3 * r + 2] * os_ref[0, :, hs]
               + gate[:, 3 * r + 2:3 * r + 3] * out[r])
        o_ref[0, :, hs] = mix.astype(o_ref.dtype)


def _nsa_window_mix(proj3, kcol, vcol, bpat, gate_logits, o_c, o_s, b, t):
    tq = NSA_TQ
    d = HEAD_DIM
    gw = NSA_HPG * d
    ng = 3 * NSA_HPG
    assert WINDOW % tq == 0 and t >= WINDOW + tq
    gates = jnp.transpose(gate_logits.reshape(b, t, NSA_KV_GROUPS, ng), (0, 2, 1, 3))
    kv = lambda col: pl.BlockSpec((None, t, d), lambda bi, g, i: (bi, 0, col + g))
    tile = pl.BlockSpec((1, tq, gw), lambda bi, g, i: (bi, i, g))
    return pl.pallas_call(
        functools.partial(_win_kernel, tq=tq, window=WINDOW, c1=d ** -0.5 * LOG2E),
        grid=(b, NSA_KV_GROUPS, t // tq),
        in_specs=[tile, kv(kcol), kv(vcol),
                  pl.BlockSpec((1, 3, NSA_HPG * tq, tq), lambda bi, g, i: (g, 0, 0, 0)),
                  pl.BlockSpec((1, 1, tq, ng), lambda bi, g, i: (bi, g, i, 0)),
                  tile, tile],
        out_specs=tile,
        out_shape=jax.ShapeDtypeStruct((b, t, NSA_WIDTH), BF16),
        scratch_shapes=[pltpu.VMEM((t, d), BF16), pltpu.VMEM((t, d), BF16)],
        compiler_params=_cparams(("parallel", "parallel", "arbitrary")),
        name="nsa_window_mix",
    )(proj3, proj3, proj3, bpat, gates, o_c, o_s)


def _nsa_bias_tables(rel_bias, t):
    tq = NSA_TQ
    g, hpg = NSA_KV_GROUPS, NSA_HPG
    inv_scale = HEAD_DIM ** 0.5
    tab = rel_bias.astype(F32)
    last = REL_BUCKETS - 1
    far_from = int(np.nonzero(_rel_bucket_np(np.arange(4 * REL_MAX_DIST)) < last)[0].max()) + 1
    assert (_rel_bucket_np(np.arange(far_from, t + tq)) == last).all()

    def by_group(a):
        a = jnp.moveaxis(a, -1, 0)
        return a.reshape((g, hpg) + a.shape[1:])

    iq = np.arange(tq)[:, None]
    m = np.arange(CMP_NEAR)[None, :]
    dist_c = iq - CMP_STRIDE * (m - CMP_NEAR_LO) - (CMP_LEN - 1)
    assert dist_c[:, 0].min() >= far_from
    assert (iq - CMP_STRIDE * (CMP_NEAR - CMP_NEAR_LO) - (CMP_LEN - 1)).max() < 0
    pb = jnp.zeros((g, hpg, tq, LANES), F32)
    pb = pb.at[..., :CMP_NEAR].set(by_group(tab[_rel_bucket_np(dist_c)]))
    pb = pb.at[..., CMP_NEAR].set(by_group(tab[last])[..., None])
    pb = pb.reshape(g, hpg * tq, LANES)
    pb_hi = pb.astype(BF16)
    pb = jnp.concatenate([pb_hi, (pb - pb_hi.astype(F32)).astype(BF16)], axis=-1)
    assert 2 * tq - (tq - 1) >= far_from
    jk = np.arange(tq)[None, :]
    idx = np.stack([_rel_bucket_np(iq - jk), _rel_bucket_np(tq + iq - jk), np.full((tq, tq), last, np.int32)])
    pat = by_group(tab[idx]) * inv_scale
    bpat = jnp.transpose(pat, (0, 2, 1, 3, 4)).reshape(g, 3, hpg * tq, tq)
    bnear = jnp.concatenate([pat[:, :, 1], pat[:, :, 0]], axis=-1).reshape(g, hpg * tq, 2 * tq)
    far = by_group(tab[last]) * inv_scale
    far_hi = far.astype(BF16).astype(F32)
    bfar = jnp.zeros((g, hpg, LANES), F32)
    bfar = bfar.at[..., FAR_BIAS_COL].set(far_hi).at[..., FAR_BIAS_COL + 1].set(far - far_hi)
    return pb, bnear, bfar, bpat


def _router_kernel(h_ref, w_ref, b_ref, tri_ref, idx_ref, wt_ref, rank_ref, cnt_ref, *, tm):
    @pl.when(pl.program_id(0) == 0)
    def _():
        cnt_ref[...] = jnp.zeros_like(cnt_ref)

    h = h_ref[...]
    h_hi = h.astype(BF16)
    h_lo = (h - h_hi.astype(F32)).astype(BF16)
    logits = (_dot(h_hi, w_ref[0]) + _dot(h_lo, w_ref[0]) + _dot(h_hi, w_ref[1])) + b_ref[...]
    lane = lax.broadcasted_iota(I32, (tm, N_EXPERTS), 1).astype(F32)
    out_lane = lax.broadcasted_iota(I32, (tm, LANES), 1)
    work = logits
    idx_out = jnp.zeros((tm, LANES), F32)
    val_out = jnp.zeros((tm, LANES), F32)
    top = None
    den = jnp.zeros((tm, 1), F32)
    hots = []
    for k in range(TOP_K):
        mx = jnp.max(work, axis=-1, keepdims=True)
        idx = jnp.min(jnp.where(work == mx, lane, float(N_EXPERTS)), axis=-1, keepdims=True)
        if top is None:
            top = mx
        e = jnp.exp(mx - top)
        den = den + e
        idx_out = jnp.where(out_lane == k, idx, idx_out)
        val_out = jnp.where(out_lane == k, e, val_out)
        hots.append(jnp.where(lane == idx, 1.0, 0.0))
        work = jnp.where(lane == idx, -jnp.inf, work)
    idx_ref[...] = idx_out.astype(I32)
    wt_ref[...] = val_out / den
    hot = hots[0] + hots[1] + hots[2] + hots[3]
    before = cnt_ref[...] + _dot(tri_ref[...], hot.astype(BF16))
    rank_out = jnp.zeros((tm, LANES), F32)
    for k in range(TOP_K):
        rank_out = jnp.where(out_lane == k, jnp.sum(hots[k] * before, axis=-1, keepdims=True), rank_out)
    rank_ref[...] = rank_out.astype(I32)
    cnt_ref[...] += jnp.sum(hot, axis=0, keepdims=True)


def _router(h, w, bias, *, tm=512):
    n, d = h.shape
    out = pl.BlockSpec((tm, LANES), lambda i: (i, 0))
    tri = jnp.asarray(np.tril(np.ones((tm, tm), np.float32), -1), BF16)
    w_hi = w.astype(BF16)
    return pl.pallas_call(
        functools.partial(_router_kernel, tm=tm),
        grid=(n // tm,),
        in_specs=[pl.BlockSpec((tm, d), lambda i: (i, 0)),
                  pl.BlockSpec((2, d, N_EXPERTS), lambda i: (0, 0, 0)),
                  pl.BlockSpec((1, N_EXPERTS), lambda i: (0, 0)),
                  pl.BlockSpec((tm, tm), lambda i: (0, 0))],
        out_specs=[out, out, out, pl.BlockSpec((1, N_EXPERTS), lambda i: (0, 0))],
        out_shape=[jax.ShapeDtypeStruct((n, LANES), I32), jax.ShapeDtypeStruct((n, LANES), F32),
                   jax.ShapeDtypeStruct((n, LANES), I32), jax.ShapeDtypeStruct((1, N_EXPERTS), F32)],
        compiler_params=_cparams(("arbitrary",)),
        name="moe_router",
    )(h, jnp.stack([w_hi, (w - w_hi.astype(F32)).astype(BF16)]), bias.reshape(1, N_EXPERTS), tri)


GU_BLK = 2 * LANES


def _gu_prep_kernel(w_ref, p_ref, o_ref):
    for c in range(w_ref.shape[-1] // GU_BLK):
        cols = slice(c * GU_BLK, (c + 1) * GU_BLK)
        o_ref[0, :, cols] = _dot(w_ref[0, :, cols].astype(BF16), p_ref[...]).astype(BF16)


def _gu_prep(w_gu, first, e, *, tk=4096):
    _, d, f2 = w_gu.shape
    assert f2 % GU_BLK == 0
    k = np.arange(GU_BLK)
    perm = np.zeros((GU_BLK, GU_BLK), np.float32)
    perm[k, (k % 2) * LANES + k // 2] = 1.0
    return pl.pallas_call(
        _gu_prep_kernel,
        grid=(e, d // tk),
        in_specs=[pl.BlockSpec((1, tk, f2), lambda ei, ki: (first + ei, ki, 0)),
                  pl.BlockSpec((GU_BLK, GU_BLK), lambda ei, ki: (0, 0))],
        out_specs=pl.BlockSpec((1, tk, f2), lambda ei, ki: (ei, ki, 0)),
        out_shape=jax.ShapeDtypeStruct((e, d, f2), BF16),
        compiler_params=_cparams(("parallel", "parallel")),
        name="moe_gu_prep",
    )(w_gu, jnp.asarray(perm, BF16))


MOE_TM = 512


def _expert_kernel(te_ref, x_ref, wgu_ref, bg_ref, bl_ref, wd_ref, bd_ref, o_ref):
    @pl.when(pl.program_id(0) < te_ref[pl.num_programs(0)])
    def _():
        wd = wd_ref[0].astype(BF16)
        half = x_ref.shape[0] // 2
        for part in range(2):
            rows = slice(part * half, (part + 1) * half)
            hgu = _dot(x_ref[rows, :], wgu_ref[0])
            nblk = hgu.shape[1] // GU_BLK
            glu = jnp.concatenate([hgu[:, c * GU_BLK:c * GU_BLK + LANES] for c in range(nblk)], axis=1)
            lin = jnp.concatenate([hgu[:, c * GU_BLK + LANES:(c + 1) * GU_BLK] for c in range(nblk)], axis=1)
            glu = jnp.minimum(glu + bg_ref[0], SWIGLU_LIMIT)
            lin = jnp.clip(lin + bl_ref[0], -SWIGLU_LIMIT, SWIGLU_LIMIT)
            act = glu * jax.nn.sigmoid(SWIGLU_ALPHA * glu) * (lin + 1.0)
            o_ref[rows, :] = (_dot(act.astype(BF16), wd) + bd_ref[0]).astype(o_ref.dtype)


def _experts(tile_expert, xs, w_gu, first, b_glu, b_lin, w_down, b_down):
    p, d = xs.shape
    tm = MOE_TM
    ff = w_down.shape[1]
    by_expert = lambda shape: pl.BlockSpec((1,) + shape, lambda i, te: (te[i], 0, 0))
    grid_spec = pltpu.PrefetchScalarGridSpec(
        num_scalar_prefetch=1,
        grid=(p // tm,),
        in_specs=[pl.BlockSpec((tm, d), lambda i, te: (i, 0)),
                  pl.BlockSpec((1, d, 2 * ff), lambda i, te: (te[i] - first, 0, 0)),
                  by_expert((1, ff)), by_expert((1, ff)),
                  by_expert((ff, d)), by_expert((1, d))],
        out_specs=pl.BlockSpec((tm, d), lambda i, te: (i, 0)),
    )
    return pl.pallas_call(
        _expert_kernel,
        grid_spec=grid_spec,
        out_shape=jax.ShapeDtypeStruct((p, d), BF16),
        compiler_params=_cparams(("arbitrary",)),
        name="moe_experts",
    )(tile_expert, xs, w_gu, b_glu, b_lin, w_down, b_down)


def _moe(h_f32, h_bf16, w_router, b_router, layer, w_gu, b_glu, b_lin, w_down, b_down):
    n, d = h_f32.shape
    tm = MOE_TM
    idx128, wt128, rank128, cnt = _router(h_f32, w_router, b_router)
    expert = idx128[:, :TOP_K]
    na = n * TOP_K
    p = na + N_EXPERTS * tm
    counts = cnt[0].astype(I32)
    padded = ((counts + tm - 1) // tm) * tm
    pend = jnp.cumsum(padded)
    pstart = pend - padded
    first = jnp.sum(jnp.where(expert[..., None] == jnp.arange(N_EXPERTS, dtype=I32), pstart, 0), axis=-1)
    slot = first + rank128[:, :TOP_K]
    token = jnp.arange(na, dtype=I32) // TOP_K
    row_token = (jnp.arange(p, dtype=I32) % n).at[slot.reshape(-1)].set(
        token, mode="promise_in_bounds", unique_indices=True)
    tile_start = jnp.arange(p // tm, dtype=I32) * tm
    tile_expert = jnp.minimum(jnp.sum((tile_start[:, None] >= pend[None, :]).astype(I32), axis=1), N_EXPERTS - 1)
    xs = jnp.take(h_bf16, row_token, axis=0, mode="clip")
    tile_info = jnp.concatenate([tile_expert + layer * N_EXPERTS, pend[-1:] // tm]).astype(I32)
    ys = _experts(tile_info, xs, w_gu, layer * N_EXPERTS, b_glu, b_lin, w_down, b_down)
    return ys.at[slot.T.reshape(-1)].get(mode="promise_in_bounds"), wt128


def kernel(x, ln1_g, ln1_b, ln2_g, ln2_b, ev_w_in, ev_w_out, hg_lb_raw, hg_norm_g, od_w_in, od_w_out,
           cmp_k_w1, cmp_k_w2, cmp_k_pos, cmp_v_w1, cmp_v_w2, cmp_v_pos, rel_bias, router_w, router_b,
           exp_w_gu, exp_b_gu, exp_w_down, exp_b_down):
    b, t, d = x.shape
    n = b * t
    depth = ln1_g.shape[0]
    alpha = (2 * depth) ** 0.25
    lb_soft = jax.nn.softmax(hg_lb_raw.astype(F32), axis=0)
    lower_bounds = jnp.cumsum(lb_soft, axis=0) - lb_soft[0]
    pb, bnear, bfar, bpat = _nsa_bias_tables(rel_bias, t)

    n_all = depth * N_EXPERTS
    ff = exp_w_down.shape[2]
    wgu_raw = exp_w_gu.reshape(n_all, d, 2 * ff)
    wgu = [_gu_prep(wgu_raw, layer * N_EXPERTS, N_EXPERTS) for layer in range(depth)]
    wd_all = exp_w_down.reshape(n_all, ff, d)
    bg_all = exp_b_gu[..., 0::2].reshape(n_all, 1, ff)
    bl_all = exp_b_gu[..., 1::2].reshape(n_all, 1, ff)
    bd_all = exp_b_down.reshape(n_all, 1, d)

    h = x.reshape(n, d)
    hb = h.astype(BF16)
    for layer in range(depth):
        if layer % 2 == 0:
            e = layer // 2
            proj = _matmul(hb, ev_w_in, e)
            o_a = _sb_attention(proj, b, t)
            o_b = _hgrn2(proj, lower_bounds[layer], hg_norm_g[e], b, t)
            mix = _matmul2(o_a, o_b, ev_w_out, e)
        else:
            o = layer // 2
            proj = _matmul(hb, od_w_in, o, n=NSA_WIDTH + 6 * NSA_KV_WIDTH)
            gate_logits = _matmul(hb, od_w_in[o, :, NSA_WIDTH + 6 * NSA_KV_WIDTH:])
            proj3 = proj.reshape(b, t, -1)
            c0 = NSA_WIDTH // HEAD_DIM
            k_cmp = _compress(proj3, c0, cmp_k_w1[o], cmp_k_w2[o], cmp_k_pos[o])
            v_cmp = _compress(proj3, c0 + NSA_KV_GROUPS, cmp_v_w1[o], cmp_v_w2[o], cmp_v_pos[o])
            o_c, sel_t = _cmp_select(proj3, k_cmp, v_cmp, pb, b, t)
            sel = jnp.swapaxes(sel_t, 2, 3)
            o_s = _nsa_selected(proj3, c0 + 2 * NSA_KV_GROUPS, c0 + 3 * NSA_KV_GROUPS, bnear, bfar, sel, b, t)
            mixed = _nsa_window_mix(proj3, c0 + 4 * NSA_KV_GROUPS, c0 + 5 * NSA_KV_GROUPS, bpat,
                                    gate_logits, o_c, o_s, b, t)
            mix = _matmul(mixed.reshape(n, -1), od_w_out, o)
        h, hb = _add_ln(h, mix, ln1_g[layer], ln1_b[layer], alpha)
        ys, gate_w = _moe(h, hb, router_w[layer], router_b[layer], layer,
                          wgu[layer], bg_all, bl_all, wd_all, bd_all)
        h, hb = _combine_ln(h, ys, gate_w, ln2_g[layer], ln2_b[layer], alpha)
    return h.reshape(b, t, d)
```
